```python
import jax, jax.numpy as jnp
from jax import lax
import numpy as np

D_MODEL = 1024
BATCH = 4
SEQ = 8192
DEPTH = 1

PLE_DIM = 256
MLA_HEADS = 4
QK_NOPE_DIM = 128
QK_ROPE_DIM = 64
QK_HEAD_DIM = QK_NOPE_DIM + QK_ROPE_DIM
V_HEAD_DIM = 128
Q_LORA_RANK = 256
KV_LORA_RANK = 128
ROPE_THETA = 10000.0
MLA_WIDTH = MLA_HEADS * V_HEAD_DIM
Q_BLOCK = 128
MLSTM_HEADS = 4
MLSTM_HEAD_DIM = 128
MLSTM_WIDTH = MLSTM_HEADS * MLSTM_HEAD_DIM
CONV_WIDTH = 4
CHUNK = 64
D_MIX = MLA_WIDTH + MLSTM_WIDTH
IN_SIZES = (Q_LORA_RANK, KV_LORA_RANK, QK_ROPE_DIM, MLSTM_WIDTH, MLSTM_WIDTH, MLSTM_WIDTH, MLSTM_WIDTH, MLSTM_HEADS, MLSTM_HEADS)
D_IN = sum(IN_SIZES)
D_FF = ((8 * D_MODEL + 3 * 256 - 1) // (3 * 256)) * 256
EPS = 1e-6

kernel_name = "hymba_mla_mlstm_layer"


def rms_norm(x, g):
    xf = x.astype(jnp.float32)
    y = xf * lax.rsqrt(jnp.mean(xf * xf, axis=-1, keepdims=True) + EPS)
    return (y * g.astype(jnp.float32)).astype(x.dtype)


def split_cols(z, sizes):
    offs = np.cumsum(sizes)[:-1].tolist()
    return jnp.split(z, offs, axis=-1)


def rope_cos_sin(positions):
    inv_freq = ROPE_THETA ** (-jnp.arange(0, QK_ROPE_DIM, 2, dtype=jnp.float32) / QK_ROPE_DIM)
    ang = positions.astype(jnp.float32)[..., None] * inv_freq
    return jnp.cos(ang), jnp.sin(ang)


def apply_rope(t, cos, sin):
    tf = t.astype(jnp.float32)
    t1, t2 = tf[..., :QK_ROPE_DIM // 2], tf[..., QK_ROPE_DIM // 2:]
    return jnp.concatenate([t1 * cos - t2 * sin, t2 * cos + t1 * sin], axis=-1).astype(t.dtype)


def causal_mla(q, k, v):
    B, S, H, _ = q.shape
    nb = S // Q_BLOCK
    qb = (q * (QK_HEAD_DIM ** -0.5)).reshape(B, nb, Q_BLOCK, H, QK_HEAD_DIM).transpose(1, 0, 3, 2, 4)
    key_pos = jnp.arange(S)

    def one_block(args):
        q_blk, blk = args
        s = jnp.einsum('bhqd,bkhd->bhqk', q_blk, k).astype(jnp.float32)
        q_pos = blk * Q_BLOCK + jnp.arange(Q_BLOCK)
        s = jnp.where(key_pos[None, :] <= q_pos[:, None], s, -jnp.inf)
        pr = jax.nn.softmax(s, axis=-1).astype(v.dtype)
        return jnp.einsum('bhqk,bkhv->bqhv', pr, v)

    out = lax.map(one_block, (qb, jnp.arange(nb)))
    return out.transpose(1, 0, 2, 3, 4).reshape(B, S, H * V_HEAD_DIM)


def causal_dwconv(t, w, b):
    S = t.shape[1]
    tp = jnp.pad(t, ((0, 0), (CONV_WIDTH - 1, 0), (0, 0)))
    acc = b
    for j in range(CONV_WIDTH):
        acc = acc + w[j] * tp[:, j:j + S]
    return acc


def mlstm_chunkwise(q, k, v, i_pre, f_pre):
    out_dtype = q.dtype
    B, S, NH, DH = q.shape
    nc = S // CHUNK
    f32 = jnp.float32

    def chunks4(t):
        return t.astype(f32).reshape(B, nc, CHUNK, NH, DH).transpose(1, 0, 3, 2, 4)

    def chunks3(t):
        return t.astype(f32).reshape(B, nc, CHUNK, NH).transpose(1, 0, 3, 2)

    qc_all = chunks4(q)
    kc_all = chunks4(k) * (DH ** -0.5)
    vc_all = chunks4(v)
    ic_all = chunks3(i_pre)
    fc_all = jax.nn.log_sigmoid(chunks3(f_pre))
    tril = jnp.tril(jnp.ones((CHUNK, CHUNK), dtype=bool))

    def step(carry, xs):
        C, n, m = carry
        qc, kc, vc, ic, fc = xs
        b = jnp.cumsum(fc, axis=-1)
        g = b[..., -1]
        d = b[..., :, None] - b[..., None, :] + ic[..., None, :]
        d = jnp.where(tril, d, -jnp.inf)
        m_inter = b + m[..., None]
        m_j = jnp.maximum(m_inter, d.max(axis=-1))
        s = jnp.einsum('bhld,bhsd->bhls', qc, kc) * jnp.exp(d - m_j[..., None])
        inter = jnp.exp(m_inter - m_j)
        num = jnp.einsum('bhls,bhsv->bhlv', s, vc) + inter[..., None] * jnp.einsum('bhld,bhdv->bhlv', qc, C)
        den = s.sum(axis=-1) + inter * jnp.einsum('bhld,bhd->bhl', qc, n)
        h = num / jnp.maximum(jnp.abs(den), jnp.exp(-m_j))[..., None]
        a = g[..., None] - b + ic
        m_new = jnp.maximum(g + m, a.max(axis=-1))
        decay = jnp.exp(g + m - m_new)
        w = jnp.exp(a - m_new[..., None])
        C_new = decay[..., None, None] * C + jnp.einsum('bhs,bhsd,bhsv->bhdv', w, kc, vc)
        n_new = decay[..., None] * n + jnp.einsum('bhs,bhsd->bhd', w, kc)
        return (C_new, n_new, m_new), h

    init = (jnp.zeros((B, NH, DH, DH), f32), jnp.zeros((B, NH, DH), f32), jnp.zeros((B, NH), f32))
    _, h = lax.scan(step, init, (qc_all, kc_all, vc_all, ic_all, fc_all))
    return h.transpose(1, 0, 3, 2, 4).reshape(B, S, NH, DH).astype(out_dtype)


def setup_inputs(seed: int = 0) -> dict:
    key = jax.random.key(seed)
    ks = jax.random.split(key, 26)
    f32 = jnp.float32

    def nrm(k, shape, scale):
        return jax.random.normal(k, shape, f32) * scale

    def gain(k, shape):
        return 1.0 + 0.05 * jax.random.normal(k, shape, f32)

    L = DEPTH
    x = nrm(ks[0], (BATCH, SEQ, D_MODEL), 1.0)
    p = nrm(ks[1], (DEPTH, BATCH, SEQ, PLE_DIM), 1.0)
    offset = jax.random.randint(ks[2], (BATCH, 1), 0, 1024, dtype=jnp.int32)
    positions = (offset + jnp.arange(SEQ, dtype=jnp.int32)[None, :]).astype(jnp.int32)
    return {
        "x": x,
        "p": p,
        "positions": positions,
        "attn_pre_norm": gain(ks[3], (L, D_MODEL)),
        "attn_post_norm": gain(ks[4], (L, D_MODEL)),
        "w_in": nrm(ks[5], (L, D_MODEL, D_IN), D_MODEL ** -0.5),
        "q_norm": gain(ks[6], (L, Q_LORA_RANK)),
        "kv_norm": gain(ks[7], (L, KV_LORA_RANK)),
        "w_uq": nrm(ks[8], (L, Q_LORA_RANK, MLA_HEADS * QK_HEAD_DIM), Q_LORA_RANK ** -0.5),
        "w_ukv": nrm(ks[9], (L, KV_LORA_RANK, MLA_HEADS * (QK_NOPE_DIM + V_HEAD_DIM)), KV_LORA_RANK ** -0.5),
        "conv_w": nrm(ks[10], (L, CONV_WIDTH, 2 * MLSTM_WIDTH), CONV_WIDTH ** -0.5),
        "conv_b": nrm(ks[11], (L, 2 * MLSTM_WIDTH), 0.02),
        "gate_bias_i": nrm(ks[12], (L, MLSTM_HEADS), 0.1),
        "gate_bias_f": 3.0 + nrm(ks[13], (L, MLSTM_HEADS), 0.5),
        "mlstm_norm": gain(ks[14], (L, MLSTM_WIDTH)),
        "w_out": nrm(ks[15], (L, D_MIX, D_MODEL), D_MIX ** -0.5),
        "ffn_pre_norm": gain(ks[16], (L, D_MODEL)),
        "ffn_post_norm": gain(ks[17], (L, D_MODEL)),
        "w_gate": nrm(ks[18], (L, D_MODEL, D_FF), D_MODEL ** -0.5),
        "w_up": nrm(ks[19], (L, D_MODEL, D_FF), D_MODEL ** -0.5),
        "w_down": nrm(ks[20], (L, D_FF, D_MODEL), D_FF ** -0.5),
        "w_ple_proj": nrm(ks[21], (L, PLE_DIM, D_MODEL), PLE_DIM ** -0.5),
        "w_ple_gate": nrm(ks[22], (L, D_MODEL, D_MODEL), D_MODEL ** -0.5),
    }


def reference(x, p, positions, attn_pre_norm, attn_post_norm, w_in, q_norm, kv_norm, w_uq, w_ukv,
              conv_w, conv_b, gate_bias_i, gate_bias_f, mlstm_norm, w_out, ffn_pre_norm, ffn_post_norm,
              w_gate, w_up, w_down, w_ple_proj, w_ple_gate):
    B, S, _ = x.shape
    cos, sin = rope_cos_sin(positions)
    hd = (B, S, MLSTM_HEADS, MLSTM_HEAD_DIM)
    h = x
    for i in range(DEPTH):
        u = rms_norm(h, attn_pre_norm[i])
        c_q, c_kv, k_rope, m_q, m_k, m_v, m_o, m_i, m_f = split_cols(u @ w_in[i], IN_SIZES)

        q = (rms_norm(c_q, q_norm[i]) @ w_uq[i]).reshape(B, S, MLA_HEADS, QK_HEAD_DIM)
        kv = (rms_norm(c_kv, kv_norm[i]) @ w_ukv[i]).reshape(B, S, MLA_HEADS, QK_NOPE_DIM + V_HEAD_DIM)
        q_nope, q_pe = q[..., :QK_NOPE_DIM], q[..., QK_NOPE_DIM:]
        k_nope, v = kv[..., :QK_NOPE_DIM], kv[..., QK_NOPE_DIM:]
        q_pe = apply_rope(q_pe, cos[:, :, None], sin[:, :, None])
        k_pe = apply_rope(k_rope, cos, sin)[:, :, None]
        q = jnp.concatenate([q_nope, q_pe], axis=-1)
        k = jnp.concatenate([k_nope, jnp.broadcast_to(k_pe, (B, S, MLA_HEADS, QK_ROPE_DIM))], axis=-1)
        mla_out = causal_mla(q, k, v)

        qk = jax.nn.silu(causal_dwconv(jnp.concatenate([m_q, m_k], axis=-1), conv_w[i], conv_b[i]))
        mq, mk = qk[..., :MLSTM_WIDTH], qk[..., MLSTM_WIDTH:]
        hm = mlstm_chunkwise(mq.reshape(hd), mk.reshape(hd), m_v.reshape(hd),
                             m_i + gate_bias_i[i], m_f + gate_bias_f[i])
        hm = rms_norm(hm, mlstm_norm[i].reshape(MLSTM_HEADS, MLSTM_HEAD_DIM)).reshape(B, S, MLSTM_WIDTH)
        mlstm_out = jax.nn.sigmoid(m_o) * hm

        mix = jnp.concatenate([mla_out, mlstm_out], axis=-1) @ w_out[i]
        h = h + rms_norm(mix, attn_post_norm[i])

        f = rms_norm(h, ffn_pre_norm[i])
        f = (jax.nn.silu(f @ w_gate[i]) * (f @ w_up[i])) @ w_down[i]
        h = h + rms_norm(f, ffn_post_norm[i])

        h = h + jax.nn.sigmoid(h @ w_ple_gate[i]) * (p[i] @ w_ple_proj[i])
    return h
```

```python
import functools

import jax
import jax.numpy as jnp
from jax import lax
from jax.experimental import pallas as pl
from jax.experimental.pallas import tpu as pltpu

F32 = jnp.float32
BF16 = jnp.bfloat16

D_MODEL = 1024
PLE_DIM = 256
MLA_HEADS = 4
QK_NOPE_DIM = 128
QK_ROPE_DIM = 64
QK_HEAD_DIM = QK_NOPE_DIM + QK_ROPE_DIM
V_HEAD_DIM = 128
Q_LORA_RANK = 256
KV_LORA_RANK = 128
ROPE_THETA = 10000.0
MLA_WIDTH = MLA_HEADS * V_HEAD_DIM
MLSTM_HEADS = 4
MLSTM_HEAD_DIM = 128
MLSTM_WIDTH = MLSTM_HEADS * MLSTM_HEAD_DIM
CONV_WIDTH = 4
D_FF = 2816
EPS = 1e-6

LANES = 128
SUBLANES = 8
QK_PAD = 2 * LANES
VMEM_LIMIT = 56 * 1024 * 1024

C_Q = 0
C_KV = C_Q + Q_LORA_RANK
C_MQK = C_KV + KV_LORA_RANK
C_MV = C_MQK + 2 * MLSTM_WIDTH
C_MO = C_MV + MLSTM_WIDTH
C_TAIL = C_MO + MLSTM_WIDTH
C_ROT = C_TAIL + LANES
D_IN_PACKED = C_ROT + LANES
GATE_I = QK_ROPE_DIM
GATE_F = GATE_I + MLSTM_HEADS

TM_IN = 512
TQ = 512
CHUNK = 256
TM_OUT = 512
FF_CHUNKS = (1024, 1024, 768)


def _rms(x, g):
    return x * lax.rsqrt(jnp.mean(x * x, axis=-1, keepdims=True) + EPS) * g


def _dot(a, b):
    return jnp.dot(a, b, preferred_element_type=F32)


def _dot_nt(a, b):
    return lax.dot_general(a, b, (((1,), (1,)), ((), ())), preferred_element_type=F32)


def _const_spec(shape):
    return pl.BlockSpec(shape, lambda *_: (0,) * len(shape), pipeline_mode=pl.Buffered(1))


def _in_proj_body(x_ref, pos_ref, invf_ref, g_pre_ref, w_in_ref, qn_ref, kvn_ref, w_uqa_ref,
                  w_uqb_ref, w_uk_ref, w_uv_ref, conv_w_ref, conv_b_ref, gbias_ref,
                  q_out, k_out, v_out, mq_out, mk_out, mv_out, mo_out, gate_out,
                  ext_ref, *, tm, tiles_per_seq):
    i = pl.program_id(0)
    halo = SUBLANES

    @pl.when(lax.rem(i, tiles_per_seq) == 0)
    def _():
        ext_ref[0:halo, :] = jnp.zeros((halo, 2 * MLSTM_WIDTH), F32)

    u = _rms(x_ref[...], g_pre_ref[...]).astype(BF16)

    ang = pos_ref[...] * invf_ref[...]
    cos = jnp.cos(ang)
    sin = jnp.sin(ang)
    scale = QK_HEAD_DIM ** -0.5

    cq = _rms(_dot(u, w_in_ref[:, C_Q:C_Q + Q_LORA_RANK]), qn_ref[...]).astype(BF16)
    qa = _dot(cq, w_uqa_ref[...])
    qb = _dot(cq, w_uqb_ref[...])
    for h in range(MLA_HEADS):
        o = h * QK_PAD
        q_out[:, o:o + LANES] = (qa[:, o:o + LANES] * scale).astype(BF16)
        pe = qa[:, o + LANES:o + QK_PAD] * cos + qb[:, h * LANES:(h + 1) * LANES] * sin
        q_out[:, o + LANES:o + QK_PAD] = (pe * scale).astype(BF16)

    ckv = _rms(_dot(u, w_in_ref[:, C_KV:C_KV + KV_LORA_RANK]), kvn_ref[...]).astype(BF16)
    kn = _dot(ckv, w_uk_ref[...])
    v_out[...] = _dot(ckv, w_uv_ref[...]).astype(BF16)
    tail = _dot(u, w_in_ref[:, C_TAIL:C_TAIL + LANES])
    rot = _dot(u, w_in_ref[:, C_ROT:C_ROT + LANES])
    lane = lax.broadcasted_iota(jnp.int32, tail.shape, 1)
    kpe = jnp.where(lane < QK_ROPE_DIM, tail * cos + rot * sin, 0.0).astype(BF16)
    for h in range(MLA_HEADS):
        o = h * QK_PAD
        k_out[:, o:o + LANES] = kn[:, h * LANES:(h + 1) * LANES].astype(BF16)
        k_out[:, o + LANES:o + QK_PAD] = kpe
    gate_out[...] = tail + gbias_ref[...]

    mv_out[...] = _dot(u, w_in_ref[:, C_MV:C_MV + MLSTM_WIDTH]).astype(BF16)
    mo_out[...] = jax.nn.sigmoid(_dot(u, w_in_ref[:, C_MO:C_MO + MLSTM_WIDTH])).astype(BF16)

    ext_ref[halo:halo + tm, :] = _dot(u, w_in_ref[:, C_MQK:C_MQK + 2 * MLSTM_WIDTH])
    acc = conv_b_ref[...]
    for j in range(CONV_WIDTH):
        start = halo - (CONV_WIDTH - 1) + j
        acc = acc + conv_w_ref[j:j + 1, :] * ext_ref[start:start + tm, :]
    qk = acc * jax.nn.sigmoid(acc)
    mq_out[...] = qk[:, :MLSTM_WIDTH].astype(BF16)
    mk_out[...] = (qk[:, MLSTM_WIDTH:] * (MLSTM_HEAD_DIM ** -0.5)).astype(BF16)
    ext_ref[0:halo, :] = ext_ref[tm:tm + halo, :]


def _in_proj(x2, pos_b, invf, g_pre, w_in_p, qn, kvn, w_uqa, w_uqb, w_uk, w_uv, conv_w, conv_b,
             gbias, seq):
    t = x2.shape[0]
    tm = TM_IN
    row = lambda width: pl.BlockSpec((tm, width), lambda i: (i, 0))
    out_widths = (MLA_HEADS * QK_PAD, MLA_HEADS * QK_PAD, MLA_WIDTH, MLSTM_WIDTH, MLSTM_WIDTH,
                  MLSTM_WIDTH, MLSTM_WIDTH)
    out_shape = [jax.ShapeDtypeStruct((t, w), BF16) for w in out_widths]
    out_shape.append(jax.ShapeDtypeStruct((t, LANES), F32))
    out_specs = [row(w) for w in out_widths] + [row(LANES)]
    consts = (invf, g_pre, w_in_p, qn, kvn, w_uqa, w_uqb, w_uk, w_uv, conv_w, conv_b, gbias)
    return pl.pallas_call(
        functools.partial(_in_proj_body, tm=tm, tiles_per_seq=seq // tm),
        out_shape=out_shape,
        grid=(t // tm,),
        in_specs=[row(D_MODEL), row(LANES)] + [_const_spec(c.shape) for c in consts],
        out_specs=out_specs,
        scratch_shapes=[pltpu.VMEM((tm + 2 * SUBLANES, 2 * MLSTM_WIDTH), F32)],
        compiler_params=pltpu.CompilerParams(dimension_semantics=("arbitrary",),
                                             vmem_limit_bytes=VMEM_LIMIT),
        name="in_proj",
    )(x2, pos_b, *consts)


def _attn_body(q_ref, k_ref, v_ref, o_ref, m_ref, l_ref, acc_ref, *, tq):
    qi = pl.program_id(2)
    q = q_ref[...]
    m_ref[...] = jnp.full(m_ref.shape, -jnp.inf, F32)
    l_ref[...] = jnp.zeros(l_ref.shape, F32)
    acc_ref[...] = jnp.zeros(acc_ref.shape, F32)

    def block(ki, diagonal):
        start = pl.multiple_of(ki * tq, tq)
        s = _dot_nt(q, k_ref[pl.ds(start, tq), :])
        if diagonal:
            r = lax.broadcasted_iota(jnp.int32, s.shape, 0)
            c = lax.broadcasted_iota(jnp.int32, s.shape, 1)
            s = jnp.where(c <= r, s, -jnp.inf)
        m_old = m_ref[...]
        m_new = jnp.maximum(m_old, jnp.max(s, axis=-1, keepdims=True))
        alpha = jnp.exp(m_old - m_new)
        p = jnp.exp(s - m_new)
        l_ref[...] = alpha * l_ref[...] + jnp.sum(p, axis=-1, keepdims=True)
        acc_ref[...] = alpha * acc_ref[...] + _dot(p.astype(BF16), v_ref[pl.ds(start, tq), :])
        m_ref[...] = m_new

    def body(ki, carry):
        block(ki, False)
        return carry

    lax.fori_loop(0, qi, body, 0)
    block(qi, True)
    o_ref[...] = (acc_ref[...] / l_ref[...]).astype(o_ref.dtype)


def _attention(q, k, v, batch, seq):
    t = q.shape[0]
    tq = TQ
    nq = seq // tq
    return pl.pallas_call(
        functools.partial(_attn_body, tq=tq),
        out_shape=jax.ShapeDtypeStruct((t, MLA_WIDTH), BF16),
        grid=(batch, MLA_HEADS, nq),
        in_specs=[
            pl.BlockSpec((tq, QK_PAD), lambda b, h, i: (b * nq + i, h)),
            pl.BlockSpec((seq, QK_PAD), lambda b, h, i: (b, h)),
            pl.BlockSpec((seq, V_HEAD_DIM), lambda b, h, i: (b, h)),
        ],
        out_specs=pl.BlockSpec((tq, V_HEAD_DIM), lambda b, h, i: (b * nq + i, h)),
        scratch_shapes=[pltpu.VMEM((tq, 1), F32), pltpu.VMEM((tq, 1), F32),
                        pltpu.VMEM((tq, V_HEAD_DIM), F32)],
        compiler_params=pltpu.CompilerParams(
            dimension_semantics=("arbitrary", "arbitrary", "arbitrary"),
            vmem_limit_bytes=VMEM_LIMIT),
        name="mla_attention",
    )(q, k, v)


def _mlstm_body(q_ref, k_ref, v_ref, og_ref, gate_ref, norm_ref, out_ref, c_ref, n_ref, m_ref,
                *, chunk):
    @pl.when(pl.program_id(1) == 0)
    def _():
        c_ref[...] = jnp.zeros(c_ref.shape, F32)
        n_ref[...] = jnp.zeros(n_ref.shape, F32)
        m_ref[...] = jnp.zeros(m_ref.shape, F32)

    slab = gate_ref[...]
    logf = jax.nn.log_sigmoid(slab)
    r = lax.broadcasted_iota(jnp.int32, (chunk, chunk), 0)
    c = lax.broadcasted_iota(jnp.int32, (chunk, chunk), 1)
    tril = c <= r
    bslab = jnp.dot(tril.astype(F32), logf, precision=lax.Precision.HIGHEST,
                    preferred_element_type=F32)
    slab_t = slab.T
    bslab_t = bslab.T

    for h in range(MLSTM_HEADS):
        hs = slice(h * MLSTM_HEAD_DIM, (h + 1) * MLSTM_HEAD_DIM)
        q = q_ref[:, hs]
        k = k_ref[:, hs]
        v = v_ref[:, hs]
        i_col = slab[:, GATE_I + h:GATE_I + h + 1]
        b_col = bslab[:, GATE_F + h:GATE_F + h + 1]
        i_row = slab_t[GATE_I + h:GATE_I + h + 1, :]
        b_row = bslab_t[GATE_F + h:GATE_F + h + 1, :]
        g = b_col[chunk - 1:chunk, :]
        m_prev = m_ref[h, 0:1, 0:1]

        d = jnp.where(tril, b_col - b_row + i_row, -jnp.inf)
        m_inter = b_col + m_prev
        m_j = jnp.maximum(m_inter, jnp.max(d, axis=-1, keepdims=True))
        s = _dot_nt(q, k) * jnp.exp(d - m_j)
        inter = jnp.exp(m_inter - m_j)
        cmat = c_ref[h]
        num = _dot(s.astype(BF16), v) + inter * _dot(q, cmat.astype(BF16))
        qn = jnp.sum(q.astype(F32) * n_ref[h, 0:1, :], axis=-1, keepdims=True)
        den = jnp.sum(s, axis=-1, keepdims=True) + inter * qn
        hh = num / jnp.maximum(jnp.abs(den), jnp.exp(-m_j))

        a_col = g - b_col + i_col
        a_row = g - b_row + i_row
        m_new = jnp.maximum(g + m_prev, jnp.max(a_row, axis=-1, keepdims=True))
        decay = jnp.exp(g + m_prev - m_new)
        kw = k.astype(F32) * jnp.exp(a_col - m_new)
        c_ref[h] = decay * cmat + _dot(kw.T.astype(BF16), v)
        n_ref[h, 0:1, :] = decay * n_ref[h, 0:1, :] + jnp.sum(kw, axis=0, keepdims=True)
        m_ref[h] = jnp.broadcast_to(m_new, m_ref.shape[1:])

        hn = _rms(hh, norm_ref[0:1, hs])
        out_ref[:, hs] = (og_ref[:, hs].astype(F32) * hn).astype(out_ref.dtype)


def _mlstm(mq, mk, mv, mo, gates, norm, batch, seq):
    t = mq.shape[0]
    chunk = CHUNK
    nc = seq // chunk
    row = lambda width: pl.BlockSpec((chunk, width), lambda b, j: (b * nc + j, 0))
    return pl.pallas_call(
        functools.partial(_mlstm_body, chunk=chunk),
        out_shape=jax.ShapeDtypeStruct((t, MLSTM_WIDTH), BF16),
        grid=(batch, nc),
        in_specs=[row(MLSTM_WIDTH)] * 4 + [row(LANES), _const_spec(norm.shape)],
        out_specs=row(MLSTM_WIDTH),
        scratch_shapes=[
            pltpu.VMEM((MLSTM_HEADS, MLSTM_HEAD_DIM, MLSTM_HEAD_DIM), F32),
            pltpu.VMEM((MLSTM_HEADS, SUBLANES, MLSTM_HEAD_DIM), F32),
            pltpu.VMEM((MLSTM_HEADS, SUBLANES, LANES), F32),
        ],
        compiler_params=pltpu.CompilerParams(dimension_semantics=("arbitrary", "arbitrary"),
                                             vmem_limit_bytes=VMEM_LIMIT),
        name="mlstm",
    )(mq, mk, mv, mo, gates, norm)


def _out_ffn_body(x_ref, mla_ref, mlstm_ref, p_ref, w_out_ref, g_post_ref, g_fpre_ref, g_fpost_ref,
                  w_gate_ref, w_up_ref, w_down_ref, w_pg_ref, w_pp_ref, o_ref):
    mix = (_dot(mla_ref[...], w_out_ref[0:MLA_WIDTH, :])
           + _dot(mlstm_ref[...], w_out_ref[MLA_WIDTH:MLA_WIDTH + MLSTM_WIDTH, :]))
    h1 = x_ref[...] + _rms(mix, g_post_ref[...])
    f = _rms(h1, g_fpre_ref[...]).astype(BF16)
    ffn = None
    off = 0
    for width in FF_CHUNKS:
        gate = _dot(f, w_gate_ref[:, off:off + width])
        up = _dot(f, w_up_ref[:, off:off + width])
        act = (gate * jax.nn.sigmoid(gate) * up).astype(BF16)
        part = _dot(act, w_down_ref[off:off + width, :])
        ffn = part if ffn is None else ffn + part
        off += width
    h2 = h1 + _rms(ffn, g_fpost_ref[...])
    pgate = jax.nn.sigmoid(_dot(h2.astype(BF16), w_pg_ref[...]))
    o_ref[...] = h2 + pgate * _dot(p_ref[...].astype(BF16), w_pp_ref[...])


def _out_ffn(x2, mla, mlstm, p2, w_out, g_post, g_fpre, g_fpost, w_gate, w_up, w_down, w_pg, w_pp):
    t = x2.shape[0]
    tm = TM_OUT
    row = lambda width: pl.BlockSpec((tm, width), lambda i: (i, 0))
    consts = (w_out, g_post, g_fpre, g_fpost, w_gate, w_up, w_down, w_pg, w_pp)
    return pl.pallas_call(
        _out_ffn_body,
        out_shape=jax.ShapeDtypeStruct((t, D_MODEL), F32),
        grid=(t // tm,),
        in_specs=[row(D_MODEL), row(MLA_WIDTH), row(MLSTM_WIDTH), row(PLE_DIM)]
        + [_const_spec(c.shape) for c in consts],
        out_specs=row(D_MODEL),
        compiler_params=pltpu.CompilerParams(dimension_semantics=("arbitrary",),
                                             vmem_limit_bytes=VMEM_LIMIT),
        name="out_ffn",
    )(x2, mla, mlstm, p2, *consts)


def _rotate_half_cols(w):
    half = QK_ROPE_DIM // 2
    return jnp.concatenate([-w[:, half:], w[:, :half]], axis=1)


def _pack_w_in(w_in):
    zeros = lambda n: jnp.zeros((D_MODEL, n), w_in.dtype)
    o_krope = Q_LORA_RANK + KV_LORA_RANK
    o_mq = o_krope + QK_ROPE_DIM
    o_gates = o_mq + 4 * MLSTM_WIDTH
    k_rope = w_in[:, o_krope:o_mq]
    return jnp.concatenate([
        w_in[:, :o_krope],
        w_in[:, o_mq:o_gates],
        k_rope, w_in[:, o_gates:], zeros(LANES - QK_ROPE_DIM - 2 * MLSTM_HEADS),
        _rotate_half_cols(k_rope), zeros(LANES - QK_ROPE_DIM),
    ], axis=1).astype(BF16)


def _pack_w_uq(w_uq):
    a, b = [], []
    for h in range(MLA_HEADS):
        o = h * QK_HEAD_DIM
        pe = w_uq[:, o + QK_NOPE_DIM:o + QK_HEAD_DIM]
        pad = jnp.zeros((Q_LORA_RANK, LANES - QK_ROPE_DIM), w_uq.dtype)
        a += [w_uq[:, o:o + QK_NOPE_DIM], pe, pad]
        b += [_rotate_half_cols(pe), pad]
    return jnp.concatenate(a, axis=1).astype(BF16), jnp.concatenate(b, axis=1).astype(BF16)


def _pack_w_ukv(w_ukv):
    w = w_ukv.reshape(KV_LORA_RANK, MLA_HEADS, QK_NOPE_DIM + V_HEAD_DIM)
    w_uk = w[:, :, :QK_NOPE_DIM].reshape(KV_LORA_RANK, MLA_HEADS * QK_NOPE_DIM)
    w_uv = w[:, :, QK_NOPE_DIM:].reshape(KV_LORA_RANK, MLA_WIDTH)
    return w_uk.astype(BF16), w_uv.astype(BF16)


def _layer(h, p_i, pos_b, invf, attn_pre_norm, attn_post_norm, w_in, q_norm, kv_norm, w_uq, w_ukv,
           conv_w, conv_b, gate_bias_i, gate_bias_f, mlstm_norm, w_out, ffn_pre_norm, ffn_post_norm,
           w_gate, w_up, w_down, w_ple_proj, w_ple_gate, batch, seq):
    row = lambda a: a.reshape(1, -1)
    w_uqa, w_uqb = _pack_w_uq(w_uq)
    w_uk, w_uv = _pack_w_ukv(w_ukv)
    gbias = jnp.zeros((1, LANES), F32)
    gbias = gbias.at[0, GATE_I:GATE_I + MLSTM_HEADS].set(gate_bias_i)
    gbias = gbias.at[0, GATE_F:GATE_F + MLSTM_HEADS].set(gate_bias_f)
    q, k, v, mq, mk, mv, mo, gates = _in_proj(
        h, pos_b, invf, row(attn_pre_norm), _pack_w_in(w_in), row(q_norm), row(kv_norm), w_uqa,
        w_uqb, w_uk, w_uv, conv_w, row(conv_b), gbias, seq)
    mla = _attention(q, k, v, batch, seq)
    mlstm = _mlstm(mq, mk, mv, mo, gates, row(mlstm_norm), batch, seq)
    return _out_ffn(h, mla, mlstm, p_i, w_out.astype(BF16), row(attn_post_norm), row(ffn_pre_norm),
                    row(ffn_post_norm), w_gate.astype(BF16), w_up.astype(BF16),
                    w_down.astype(BF16), w_ple_gate.astype(BF16), w_ple_proj.astype(BF16))


@jax.jit
def kernel(x, p, positions, attn_pre_norm, attn_post_norm, w_in, q_norm, kv_norm, w_uq, w_ukv,
           conv_w, conv_b, gate_bias_i, gate_bias_f, mlstm_norm, w_out, ffn_pre_norm, ffn_post_norm,
           w_gate, w_up, w_down, w_ple_proj, w_ple_gate):
    batch, seq, _ = x.shape
    t = batch * seq
    depth = p.shape[0]
    inv_freq = ROPE_THETA ** (-jnp.arange(0, QK_ROPE_DIM, 2, dtype=F32) / QK_ROPE_DIM)
    invf = jnp.concatenate([inv_freq, inv_freq, jnp.zeros((LANES - QK_ROPE_DIM,), F32)]).reshape(1, LANES)
    pos_b = jnp.broadcast_to(positions.astype(F32).reshape(t, 1), (t, LANES))
    h = x.reshape(t, D_MODEL)
    for i in range(depth):
        h = _layer(h, p[i].reshape(t, PLE_DIM), pos_b, invf, attn_pre_norm[i], attn_post_norm[i],
                   w_in[i], q_norm[i], kv_norm[i], w_uq[i], w_ukv[i], conv_w[i], conv_b[i],
                   gate_bias_i[i], gate_bias_f[i], mlstm_norm[i], w_out[i], ffn_pre_norm[i],
                   ffn_post_norm[i], w_gate[i], w_up[i], w_down[i], w_ple_proj[i], w_ple_gate[i],
                   batch, seq)
    return h.reshape(batch, seq, D_MODEL)
```

```python
import functools
import math

import jax
import jax.numpy as jnp
from jax import lax
from jax.experimental import pallas as pl
from jax.experimental.pallas import tpu as pltpu

F32 = jnp.float32
BF16 = jnp.bfloat16

D_MODEL = 1024
PLE_DIM = 256
MLA_HEADS = 4
QK_NOPE_DIM = 128
QK_ROPE_DIM = 64
QK_HEAD_DIM = QK_NOPE_DIM + QK_ROPE_DIM
V_HEAD_DIM = 128
Q_LORA_RANK = 256
KV_LORA_RANK = 128
ROPE_THETA = 10000.0
MLA_WIDTH = MLA_HEADS * V_HEAD_DIM
MLSTM_HEADS = 4
MLSTM_HEAD_DIM = 128
MLSTM_WIDTH = MLSTM_HEADS * MLSTM_HEAD_DIM
CONV_WIDTH = 4
D_FF = 2816
EPS = 1e-6

LANES = 128
SUBLANES = 8
BF16_ROWS = 16
QK_PAD = 2 * LANES
V_ROWS = V_HEAD_DIM + BF16_ROWS
ROPE_HALF = QK_ROPE_DIM // 2
VMEM_LIMIT = 56 * 1024 * 1024

C_Q = 0
C_KV = C_Q + Q_LORA_RANK
C_MQK = C_KV + KV_LORA_RANK
C_MV = C_MQK + 2 * MLSTM_WIDTH
C_MO = C_MV + MLSTM_WIDTH
D_IN_PACKED = C_MO + MLSTM_WIDTH
GATE_I = QK_ROPE_DIM
GATE_F = GATE_I + MLSTM_HEADS

TM_IN = 512
TQ = 512
HEADS_PER_STEP = 2
CHUNK = 256
TM_OUT = 512
FF_CHUNKS = (1024, 1024, 768)


def _rms(x, g):
    return x * lax.rsqrt(jnp.mean(x * x, axis=-1, keepdims=True) + EPS) * g


def _dot(a, b):
    return jnp.dot(a, b, preferred_element_type=F32)


def _dot_nt(a, b):
    return lax.dot_general(a, b, (((1,), (1,)), ((), ())), preferred_element_type=F32)


def _const_spec(shape):
    return pl.BlockSpec(shape, lambda *_: (0,) * len(shape), pipeline_mode=pl.Buffered(1))


def _in_proj_body(x_ref, pos_ref, invf_ref, g_pre_ref, w_in_ref, w_tail_ref, qn_ref, kvn_ref,
                  w_uq_ref, w_uk_ref, w_uv_ref, vones_ref, conv_w_ref, conv_b_ref, gbias_ref,
                  qt_out, k_out, vt_out, mq_out, mk_out, mv_out, mo_out, gate_out,
                  ext_ref, *, tm, tiles_per_seq):
    i = pl.program_id(0)
    halo = SUBLANES

    @pl.when(lax.rem(i, tiles_per_seq) == 0)
    def _():
        ext_ref[0:halo, :] = jnp.zeros((halo, 2 * MLSTM_WIDTH), F32)

    u = _rms(x_ref[...], g_pre_ref[...]).astype(BF16)

    ang = invf_ref[...] * pos_ref[...]
    cos = jnp.cos(ang)
    sin = jnp.sin(ang)

    def rope_t(t1, t2):
        return t1 * cos - t2 * sin, t2 * cos + t1 * sin

    scale = QK_HEAD_DIM ** -0.5 * math.log2(math.e)
    cq = _rms(_dot(u, w_in_ref[:, C_Q:C_Q + Q_LORA_RANK]), qn_ref[...]).astype(BF16)
    qt = _dot_nt(w_uq_ref[...], cq)
    for h in range(MLA_HEADS):
        o = h * QK_PAD
        pe = o + QK_NOPE_DIM
        r1, r2 = rope_t(qt[pe:pe + ROPE_HALF], qt[pe + ROPE_HALF:pe + QK_ROPE_DIM])
        qt_out[0, o:pe, :] = (qt[o:pe] * scale).astype(BF16)
        qt_out[0, pe:pe + ROPE_HALF, :] = (r1 * scale).astype(BF16)
        qt_out[0, pe + ROPE_HALF:pe + QK_ROPE_DIM, :] = (r2 * scale).astype(BF16)
        qt_out[0, pe + QK_ROPE_DIM:o + QK_PAD, :] = jnp.zeros((QK_PAD - QK_HEAD_DIM, tm), BF16)

    ckv = _rms(_dot(u, w_in_ref[:, C_KV:C_KV + KV_LORA_RANK]), kvn_ref[...]).astype(BF16)
    kn = _dot(ckv, w_uk_ref[...])
    vt_out[0] = (_dot_nt(w_uv_ref[...], ckv) + vones_ref[...]).astype(BF16)

    tail_t = _dot_nt(w_tail_ref[...], u)
    r1, r2 = rope_t(tail_t[0:ROPE_HALF], tail_t[ROPE_HALF:QK_ROPE_DIM])
    comb = jnp.concatenate([r1, r2, tail_t[QK_ROPE_DIM:] + gbias_ref[QK_ROPE_DIM:, :]], axis=0).T
    lane = lax.broadcasted_iota(jnp.int32, comb.shape, 1)
    kpe = jnp.where(lane < QK_ROPE_DIM, comb, 0.0).astype(BF16)
    for h in range(MLA_HEADS):
        o = h * QK_PAD
        k_out[:, o:o + LANES] = kn[:, h * LANES:(h + 1) * LANES].astype(BF16)
        k_out[:, o + LANES:o + QK_PAD] = kpe
    gate_out[...] = comb

    mv_out[...] = _dot(u, w_in_ref[:, C_MV:C_MV + MLSTM_WIDTH]).astype(BF16)
    mo_out[...] = jax.nn.sigmoid(_dot(u, w_in_ref[:, C_MO:C_MO + MLSTM_WIDTH])).astype(BF16)

    ext_ref[halo:halo + tm, :] = _dot(u, w_in_ref[:, C_MQK:C_MQK + 2 * MLSTM_WIDTH])
    acc = conv_b_ref[...]
    for j in range(CONV_WIDTH):
        start = halo - (CONV_WIDTH - 1) + j
        acc = acc + conv_w_ref[j:j + 1, :] * ext_ref[start:start + tm, :]
    qk = acc * jax.nn.sigmoid(acc)
    mq_out[...] = qk[:, :MLSTM_WIDTH].astype(BF16)
    mk_out[...] = (qk[:, MLSTM_WIDTH:] * (MLSTM_HEAD_DIM ** -0.5)).astype(BF16)
    ext_ref[0:halo, :] = ext_ref[tm:tm + halo, :]


def _in_proj(x2, pos_row, invf, g_pre, w_in_p, w_tail, qn, kvn, w_uq_t, w_uk, w_uv_t, vones, conv_w,
             conv_b, gbias, seq):
    t = x2.shape[0]
    tm = TM_IN
    nt = t // tm
    row = lambda width: pl.BlockSpec((tm, width), lambda i: (i, 0))
    tile_t = lambda rows: pl.BlockSpec((1, rows, tm), lambda i: (i, 0, 0))
    bf = lambda *shape: jax.ShapeDtypeStruct(shape, BF16)
    out_shape = [
        bf(nt, MLA_HEADS * QK_PAD, tm), bf(t, MLA_HEADS * QK_PAD), bf(nt, MLA_HEADS * V_ROWS, tm),
        bf(t, MLSTM_WIDTH), bf(t, MLSTM_WIDTH), bf(t, MLSTM_WIDTH), bf(t, MLSTM_WIDTH),
        jax.ShapeDtypeStruct((t, LANES), F32),
    ]
    out_specs = [
        tile_t(MLA_HEADS * QK_PAD), row(MLA_HEADS * QK_PAD), tile_t(MLA_HEADS * V_ROWS),
        row(MLSTM_WIDTH), row(MLSTM_WIDTH), row(MLSTM_WIDTH), row(MLSTM_WIDTH), row(LANES),
    ]
    consts = (invf, g_pre, w_in_p, w_tail, qn, kvn, w_uq_t, w_uk, w_uv_t, vones, conv_w, conv_b, gbias)
    return pl.pallas_call(
        functools.partial(_in_proj_body, tm=tm, tiles_per_seq=seq // tm),
        out_shape=out_shape,
        grid=(nt,),
        in_specs=[row(D_MODEL), pl.BlockSpec((1, tm), lambda i: (0, i))]
        + [_const_spec(c.shape) for c in consts],
        out_specs=out_specs,
        scratch_shapes=[pltpu.VMEM((tm + 2 * SUBLANES, 2 * MLSTM_WIDTH), F32)],
        compiler_params=pltpu.CompilerParams(dimension_semantics=("arbitrary",),
                                             vmem_limit_bytes=VMEM_LIMIT),
        name="in_proj",
    )(x2, pos_row, *consts)


def _attn_body(qt_ref, k_ref, vt_ref, o_ref, m_ref, acc_ref, *, tq, heads):
    qi = pl.program_id(2)
    m_ref[...] = jnp.full(m_ref.shape, -jnp.inf, F32)
    acc_ref[...] = jnp.zeros(acc_ref.shape, F32)

    def block(ki, diagonal):
        start = pl.multiple_of(ki * tq, tq)
        for h in range(heads):
            k = k_ref[pl.ds(start, tq), h * QK_PAD:(h + 1) * QK_PAD]
            s = _dot(k, qt_ref[0, h * QK_PAD:(h + 1) * QK_PAD, :])
            if diagonal:
                r = lax.broadcasted_iota(jnp.int32, s.shape, 0)
                c = lax.broadcasted_iota(jnp.int32, s.shape, 1)
                s = jnp.where(r <= c, s, -jnp.inf)
            m_old = m_ref[h, 0:1, :]
            m_new = jnp.maximum(m_old, jnp.max(s, axis=0, keepdims=True))
            alpha = jnp.exp2(m_old - m_new)
            p = jnp.exp2(s - m_new).astype(BF16)
            pv = _dot(vt_ref[ki, h * V_ROWS:(h + 1) * V_ROWS, :], p)
            acc_ref[h] = alpha * acc_ref[h] + pv
            m_ref[h] = jnp.broadcast_to(m_new, m_ref.shape[1:])

    def body(ki, carry):
        block(ki, False)
        return carry

    lax.fori_loop(0, qi, body, 0)
    block(qi, True)
    for h in range(heads):
        acc = acc_ref[h]
        out_t = acc[0:V_HEAD_DIM] / acc[V_HEAD_DIM:V_HEAD_DIM + 1]
        o_ref[:, h * V_HEAD_DIM:(h + 1) * V_HEAD_DIM] = out_t.T.astype(o_ref.dtype)


def _attention(qt, k, vt, batch, seq):
    t = k.shape[0]
    tq = TQ
    nq = seq // tq
    heads = HEADS_PER_STEP
    return pl.pallas_call(
        functools.partial(_attn_body, tq=tq, heads=heads),
        out_shape=jax.ShapeDtypeStruct((t, MLA_WIDTH), BF16),
        grid=(batch, MLA_HEADS // heads, nq),
        in_specs=[
            pl.BlockSpec((1, heads * QK_PAD, tq), lambda b, g, i: (b * nq + i, g, 0)),
            pl.BlockSpec((seq, heads * QK_PAD), lambda b, g, i: (b, g)),
            pl.BlockSpec((nq, heads * V_ROWS, tq), lambda b, g, i: (b, g, 0)),
        ],
        out_specs=pl.BlockSpec((tq, heads * V_HEAD_DIM), lambda b, g, i: (b * nq + i, g)),
        scratch_shapes=[pltpu.VMEM((heads, SUBLANES, tq), F32),
                        pltpu.VMEM((heads, V_ROWS, tq), F32)],
        compiler_params=pltpu.CompilerParams(
            dimension_semantics=("arbitrary", "arbitrary", "arbitrary"),
            vmem_limit_bytes=VMEM_LIMIT),
        name="mla_attention",
    )(qt, k, vt)


def _mlstm_body(q_ref, k_ref, v_ref, og_ref, gate_ref, norm_ref, out_ref, c_ref, n_ref, m_ref,
                *, chunk):
    @pl.when(pl.program_id(1) == 0)
    def _():
        c_ref[...] = jnp.zeros(c_ref.shape, F32)
        n_ref[...] = jnp.zeros(n_ref.shape, F32)
        m_ref[...] = jnp.zeros(m_ref.shape, F32)

    slab = gate_ref[...]
    logf = jax.nn.log_sigmoid(slab)
    r = lax.broadcasted_iota(jnp.int32, (chunk, chunk), 0)
    c = lax.broadcasted_iota(jnp.int32, (chunk, chunk), 1)
    tril = c <= r
    bslab = jnp.dot(tril.astype(F32), logf, precision=lax.Precision.HIGHEST,
                    preferred_element_type=F32)
    slab_t = slab.T
    bslab_t = bslab.T

    for h in range(MLSTM_HEADS):
        hs = slice(h * MLSTM_HEAD_DIM, (h + 1) * MLSTM_HEAD_DIM)
        q = q_ref[:, hs]
        k = k_ref[:, hs]
        v = v_ref[:, hs]
        i_col = slab[:, GATE_I + h:GATE_I + h + 1]
        b_col = bslab[:, GATE_F + h:GATE_F + h + 1]
        i_row = slab_t[GATE_I + h:GATE_I + h + 1, :]
        b_row = bslab_t[GATE_F + h:GATE_F + h + 1, :]
        g = b_col[chunk - 1:chunk, :]
        m_prev = m_ref[h, 0:1, 0:1]

        d = jnp.where(tril, b_col - b_row + i_row, -jnp.inf)
        m_inter = b_col + m_prev
        m_j = jnp.maximum(m_inter, jnp.max(d, axis=-1, keepdims=True))
        s = _dot_nt(q, k) * jnp.exp(d - m_j)
        inter = jnp.exp(m_inter - m_j)
        cmat = c_ref[h]
        num = _dot(s.astype(BF16), v) + inter * _dot(q, cmat.astype(BF16))
        qn = jnp.sum(q.astype(F32) * n_ref[h, 0:1, :], axis=-1, keepdims=True)
        den = jnp.sum(s, axis=-1, keepdims=True) + inter * qn
        hh = num / jnp.maximum(jnp.abs(den), jnp.exp(-m_j))

        a_col = g - b_col + i_col
        a_row = g - b_row + i_row
        m_new = jnp.maximum(g + m_prev, jnp.max(a_row, axis=-1, keepdims=True))
        decay = jnp.exp(g + m_prev - m_new)
        kw = k.astype(F32) * jnp.exp(a_col - m_new)
        c_ref[h] = decay * cmat + _dot(kw.T.astype(BF16), v)
        n_ref[h, 0:1, :] = decay * n_ref[h, 0:1, :] + jnp.sum(kw, axis=0, keepdims=True)
        m_ref[h] = jnp.broadcast_to(m_new, m_ref.shape[1:])

        hn = _rms(hh, norm_ref[0:1, hs])
        out_ref[:, hs] = (og_ref[:, hs].astype(F32) * hn).astype(out_ref.dtype)


def _mlstm(mq, mk, mv, mo, gates, norm, batch, seq):
    t = mq.shape[0]
    chunk = CHUNK
    nc = seq // chunk
    row = lambda width: pl.BlockSpec((chunk, width), lambda b, j: (b * nc + j, 0))
    return pl.pallas_call(
        functools.partial(_mlstm_body, chunk=chunk),
        out_shape=jax.ShapeDtypeStruct((t, MLSTM_WIDTH), BF16),
        grid=(batch, nc),
        in_specs=[row(MLSTM_WIDTH)] * 4 + [row(LANES), _const_spec(norm.shape)],
        out_specs=row(MLSTM_WIDTH),
        scratch_shapes=[
            pltpu.VMEM((MLSTM_HEADS, MLSTM_HEAD_DIM, MLSTM_HEAD_DIM), F32),
            pltpu.VMEM((MLSTM_HEADS, SUBLANES, MLSTM_HEAD_DIM), F32),
            pltpu.VMEM((MLSTM_HEADS, SUBLANES, LANES), F32),
        ],
        compiler_params=pltpu.CompilerParams(dimension_semantics=("arbitrary", "arbitrary"),
                                             vmem_limit_bytes=VMEM_LIMIT),
        name="mlstm",
    )(mq, mk, mv, mo, gates, norm)


def _out_ffn_body(x_ref, mla_ref, mlstm_ref, p_ref, w_out_ref, g_post_ref, g_fpre_ref, g_fpost_ref,
                  w_gate_ref, w_up_ref, w_down_ref, w_pg_ref, w_pp_ref, o_ref):
    mix = (_dot(mla_ref[...], w_out_ref[0:MLA_WIDTH, :])
           + _dot(mlstm_ref[...], w_out_ref[MLA_WIDTH:MLA_WIDTH + MLSTM_WIDTH, :]))
    h1 = x_ref[...] + _rms(mix, g_post_ref[...])
    f = _rms(h1, g_fpre_ref[...]).astype(BF16)
    ffn = None
    off = 0
    for width in FF_CHUNKS:
        gate = _dot(f, w_gate_ref[:, off:off + width])
        up = _dot(f, w_up_ref[:, off:off + width])
        act = (gate * jax.nn.sigmoid(gate) * up).astype(BF16)
        part = _dot(act, w_down_ref[off:off + width, :])
        ffn = part if ffn is None else ffn + part
        off += width
    h2 = h1 + _rms(ffn, g_fpost_ref[...])
    pgate = jax.nn.sigmoid(_dot(h2.astype(BF16), w_pg_ref[...]))
    o_ref[...] = h2 + pgate * _dot(p_ref[...].astype(BF16), w_pp_ref[...])


def _out_ffn(x2, mla, mlstm, p2, w_out, g_post, g_fpre, g_fpost, w_gate, w_up, w_down, w_pg, w_pp):
    t = x2.shape[0]
    tm = TM_OUT
    row = lambda width: pl.BlockSpec((tm, width), lambda i: (i, 0))
    consts = (w_out, g_post, g_fpre, g_fpost, w_gate, w_up, w_down, w_pg, w_pp)
    return pl.pallas_call(
        _out_ffn_body,
        out_shape=jax.ShapeDtypeStruct((t, D_MODEL), F32),
        grid=(t // tm,),
        in_specs=[row(D_MODEL), row(MLA_WIDTH), row(MLSTM_WIDTH), row(PLE_DIM)]
        + [_const_spec(c.shape) for c in consts],
        out_specs=row(D_MODEL),
        compiler_params=pltpu.CompilerParams(dimension_semantics=("arbitrary",),
                                             vmem_limit_bytes=VMEM_LIMIT),
        name="out_ffn",
    )(x2, mla, mlstm, p2, *consts)


def _pack_w_in(w_in):
    o_krope = Q_LORA_RANK + KV_LORA_RANK
    o_mq = o_krope + QK_ROPE_DIM
    o_gates = o_mq + 4 * MLSTM_WIDTH
    main = jnp.concatenate([w_in[:, :o_krope], w_in[:, o_mq:o_gates]], axis=1)
    tail = jnp.concatenate([w_in[:, o_krope:o_mq], w_in[:, o_gates:]], axis=1).T
    tail = jnp.pad(tail, ((0, LANES - tail.shape[0]), (0, 0)))
    return main.astype(BF16), tail.astype(BF16)


def _pack_w_uq_t(w_uq):
    w = w_uq.T.reshape(MLA_HEADS, QK_HEAD_DIM, Q_LORA_RANK)
    w = jnp.pad(w, ((0, 0), (0, QK_PAD - QK_HEAD_DIM), (0, 0)))
    return w.reshape(MLA_HEADS * QK_PAD, Q_LORA_RANK).astype(BF16)


def _pack_w_ukv(w_ukv):
    w = w_ukv.reshape(KV_LORA_RANK, MLA_HEADS, QK_NOPE_DIM + V_HEAD_DIM)
    w_uk = w[:, :, :QK_NOPE_DIM].reshape(KV_LORA_RANK, MLA_HEADS * QK_NOPE_DIM)
    w_uv_t = jnp.transpose(w[:, :, QK_NOPE_DIM:], (1, 2, 0))
    w_uv_t = jnp.pad(w_uv_t, ((0, 0), (0, V_ROWS - V_HEAD_DIM), (0, 0)))
    return w_uk.astype(BF16), w_uv_t.reshape(MLA_HEADS * V_ROWS, KV_LORA_RANK).astype(BF16)


def _layer(h, p_i, pos_row, invf, attn_pre_norm, attn_post_norm, w_in, q_norm, kv_norm, w_uq, w_ukv,
           conv_w, conv_b, gate_bias_i, gate_bias_f, mlstm_norm, w_out, ffn_pre_norm, ffn_post_norm,
           w_gate, w_up, w_down, w_ple_proj, w_ple_gate, batch, seq):
    row = lambda a: a.reshape(1, -1)
    w_in_p, w_tail = _pack_w_in(w_in)
    w_uk, w_uv_t = _pack_w_ukv(w_ukv)
    gbias = jnp.zeros((LANES, 1), F32)
    gbias = gbias.at[GATE_I:GATE_I + MLSTM_HEADS, 0].set(gate_bias_i)
    gbias = gbias.at[GATE_F:GATE_F + MLSTM_HEADS, 0].set(gate_bias_f)
    vones = jnp.zeros((MLA_HEADS, V_ROWS, 1), F32).at[:, V_HEAD_DIM, 0].set(1.0)
    qt, k, vt, mq, mk, mv, mo, gates = _in_proj(
        h, pos_row, invf, row(attn_pre_norm), w_in_p, w_tail, row(q_norm), row(kv_norm),
        _pack_w_uq_t(w_uq), w_uk, w_uv_t, vones.reshape(MLA_HEADS * V_ROWS, 1), conv_w, row(conv_b),
        gbias, seq)
    mla = _attention(qt, k, vt, batch, seq)
    mlstm = _mlstm(mq, mk, mv, mo, gates, row(mlstm_norm), batch, seq)
    return _out_ffn(h, mla, mlstm, p_i, w_out.astype(BF16), row(attn_post_norm), row(ffn_pre_norm),
                    row(ffn_post_norm), w_gate.astype(BF16), w_up.astype(BF16),
                    w_down.astype(BF16), w_ple_gate.astype(BF16), w_ple_proj.astype(BF16))


@jax.jit
def kernel(x, p, positions, attn_pre_norm, attn_post_norm, w_in, q_norm, kv_norm, w_uq, w_ukv,
           conv_w, conv_b, gate_bias_i, gate_bias_f, mlstm_norm, w_out, ffn_pre_norm, ffn_post_norm,
           w_gate, w_up, w_down, w_ple_proj, w_ple_gate):
    batch, seq, _ = x.shape
    t = batch * seq
    depth = p.shape[0]
    inv_freq = ROPE_THETA ** (-jnp.arange(0, QK_ROPE_DIM, 2, dtype=F32) / QK_ROPE_DIM)
    invf = inv_freq.reshape(ROPE_HALF, 1)
    pos_row = positions.astype(F32).reshape(1, t)
    h = x.reshape(t, D_MODEL)
    for i in range(depth):
        h = _layer(h, p[i].reshape(t, PLE_DIM), pos_row, invf, attn_pre_norm[i], attn_post_norm[i],
                   w_in[i], q_norm[i], kv_norm[i], w_uq[i], w_ukv[i], conv_w[i], conv_b[i],
                   gate_bias_i[i], gate_bias_f[i], mlstm_norm[i], w_out[i], ffn_pre_norm[i],
                   ffn_post_norm[i], w_gate[i], w_up[i], w_down[i], w_ple_proj[i], w_ple_gate[i],
                   batch, seq)
    return h.reshape(batch, seq, D_MODEL)
```

```python
import functools
import math

import jax
import jax.numpy as jnp
from jax import lax
from jax.experimental import pallas as pl
from jax.experimental.pallas import tpu as pltpu

F32 = jnp.float32
BF16 = jnp.bfloat16

D_MODEL = 1024
PLE_DIM = 256
MLA_HEADS = 4
QK_NOPE_DIM = 128
QK_ROPE_DIM = 64
QK_HEAD_DIM = QK_NOPE_DIM + QK_ROPE_DIM
V_HEAD_DIM = 128
Q_LORA_RANK = 256
KV_LORA_RANK = 128
ROPE_THETA = 10000.0
MLA_WIDTH = MLA_HEADS * V_HEAD_DIM
MLSTM_HEADS = 4
MLSTM_HEAD_DIM = 128
MLSTM_WIDTH = MLSTM_HEADS * MLSTM_HEAD_DIM
CONV_WIDTH = 4
D_FF = 2816
EPS = 1e-6

LANES = 128
SUBLANES = 8
BF16_ROWS = 16
QK_PAD = 2 * LANES
V_ROWS = V_HEAD_DIM + BF16_ROWS
ROPE_HALF = QK_ROPE_DIM // 2
VMEM_LIMIT = 56 * 1024 * 1024

C_Q = 0
C_KV = C_Q + Q_LORA_RANK
C_MQK = C_KV + KV_LORA_RANK
C_MV = C_MQK + 2 * MLSTM_WIDTH
C_MO = C_MV + MLSTM_WIDTH
D_IN_PACKED = C_MO + MLSTM_WIDTH
GATE_I = QK_ROPE_DIM
GATE_F = GATE_I + MLSTM_HEADS

TM_IN = 512
TQ = 512
HEADS_PER_STEP = 2
CHUNK = 256
TM_OUT = 512
FF_CHUNKS = (1024, 1024, 768)


def _rms(x, g):
    return x * lax.rsqrt(jnp.mean(x * x, axis=-1, keepdims=True) + EPS) * g


def _dot(a, b):
    return jnp.dot(a, b, preferred_element_type=F32)


def _dot_nt(a, b):
    return lax.dot_general(a, b, (((1,), (1,)), ((), ())), preferred_element_type=F32)


def _const_spec(shape):
    return pl.BlockSpec(shape, lambda *_: (0,) * len(shape), pipeline_mode=pl.Buffered(1))


def _in_proj_body(x_ref, pos_ref, invf_ref, g_pre_ref, w_in_ref, w_tail_ref, qn_ref, kvn_ref,
                  w_uq_ref, w_uk_ref, w_uv_ref, vones_ref, conv_w_ref, conv_b_ref, gbias_ref,
                  qt_out, k_out, vt_out, mq_out, mk_out, mv_out, mo_out, gate_out,
                  ext_ref, *, tm, tiles_per_seq):
    i = pl.program_id(0)
    halo = SUBLANES

    @pl.when(lax.rem(i, tiles_per_seq) == 0)
    def _():
        ext_ref[0:halo, :] = jnp.zeros((halo, 2 * MLSTM_WIDTH), F32)

    u = _rms(x_ref[...], g_pre_ref[...]).astype(BF16)

    ang = invf_ref[...] * pos_ref[...]
    cos = jnp.cos(ang)
    sin = jnp.sin(ang)

    def rope_t(t1, t2):
        return t1 * cos - t2 * sin, t2 * cos + t1 * sin

    scale = QK_HEAD_DIM ** -0.5 * math.log2(math.e)
    cq = _rms(_dot(u, w_in_ref[:, C_Q:C_Q + Q_LORA_RANK]), qn_ref[...]).astype(BF16)
    qt = _dot_nt(w_uq_ref[...], cq)
    for h in range(MLA_HEADS):
        o = h * QK_PAD
        pe = o + QK_NOPE_DIM
        r1, r2 = rope_t(qt[pe:pe + ROPE_HALF], qt[pe + ROPE_HALF:pe + QK_ROPE_DIM])
        qt_out[0, o:pe, :] = (qt[o:pe] * scale).astype(BF16)
        qt_out[0, pe:pe + ROPE_HALF, :] = (r1 * scale).astype(BF16)
        qt_out[0, pe + ROPE_HALF:pe + QK_ROPE_DIM, :] = (r2 * scale).astype(BF16)
        qt_out[0, pe + QK_ROPE_DIM:o + QK_PAD, :] = jnp.zeros((QK_PAD - QK_HEAD_DIM, tm), BF16)

    ckv = _rms(_dot(u, w_in_ref[:, C_KV:C_KV + KV_LORA_RANK]), kvn_ref[...]).astype(BF16)
    kn = _dot(ckv, w_uk_ref[...])
    vt_out[0] = (_dot_nt(w_uv_ref[...], ckv) + vones_ref[...]).astype(BF16)

    tail_t = _dot_nt(w_tail_ref[...], u)
    r1, r2 = rope_t(tail_t[0:ROPE_HALF], tail_t[ROPE_HALF:QK_ROPE_DIM])
    comb = jnp.concatenate([r1, r2, tail_t[QK_ROPE_DIM:] + gbias_ref[QK_ROPE_DIM:, :]], axis=0).T
    lane = lax.broadcasted_iota(jnp.int32, comb.shape, 1)
    kpe = jnp.where(lane < QK_ROPE_DIM, comb, 0.0).astype(BF16)
    for h in range(MLA_HEADS):
        o = h * QK_PAD
        k_out[:, o:o + LANES] = kn[:, h * LANES:(h + 1) * LANES].astype(BF16)
        k_out[:, o + LANES:o + QK_PAD] = kpe
    gate_out[...] = comb

    mv_out[...] = _dot(u, w_in_ref[:, C_MV:C_MV + MLSTM_WIDTH]).astype(BF16)
    mo_out[...] = jax.nn.sigmoid(_dot(u, w_in_ref[:, C_MO:C_MO + MLSTM_WIDTH])).astype(BF16)

    ext_ref[halo:halo + tm, :] = _dot(u, w_in_ref[:, C_MQK:C_MQK + 2 * MLSTM_WIDTH])
    acc = conv_b_ref[...]
    for j in range(CONV_WIDTH):
        start = halo - (CONV_WIDTH - 1) + j
        acc = acc + conv_w_ref[j:j + 1, :] * ext_ref[start:start + tm, :]
    qk = acc * jax.nn.sigmoid(acc)
    mq_out[...] = qk[:, :MLSTM_WIDTH].astype(BF16)
    mk_out[...] = (qk[:, MLSTM_WIDTH:] * (MLSTM_HEAD_DIM ** -0.5)).astype(BF16)
    ext_ref[0:halo, :] = ext_ref[tm:tm + halo, :]


def _in_proj(x2, pos_row, invf, g_pre, w_in_p, w_tail, qn, kvn, w_uq_t, w_uk, w_uv_t, vones, conv_w,
             conv_b, gbias, seq):
    t = x2.shape[0]
    tm = TM_IN
    nt = t // tm
    row = lambda width: pl.BlockSpec((tm, width), lambda i: (i, 0))
    tile_t = lambda rows: pl.BlockSpec((1, rows, tm), lambda i: (i, 0, 0))
    bf = lambda *shape: jax.ShapeDtypeStruct(shape, BF16)
    out_shape = [
        bf(nt, MLA_HEADS * QK_PAD, tm), bf(t, MLA_HEADS * QK_PAD), bf(nt, MLA_HEADS * V_ROWS, tm),
        bf(t, MLSTM_WIDTH), bf(t, MLSTM_WIDTH), bf(t, MLSTM_WIDTH), bf(t, MLSTM_WIDTH),
        jax.ShapeDtypeStruct((t, LANES), F32),
    ]
    out_specs = [
        tile_t(MLA_HEADS * QK_PAD), row(MLA_HEADS * QK_PAD), tile_t(MLA_HEADS * V_ROWS),
        row(MLSTM_WIDTH), row(MLSTM_WIDTH), row(MLSTM_WIDTH), row(MLSTM_WIDTH), row(LANES),
    ]
    consts = (invf, g_pre, w_in_p, w_tail, qn, kvn, w_uq_t, w_uk, w_uv_t, vones, conv_w, conv_b, gbias)
    return pl.pallas_call(
        functools.partial(_in_proj_body, tm=tm, tiles_per_seq=seq // tm),
        out_shape=out_shape,
        grid=(nt,),
        in_specs=[row(D_MODEL), pl.BlockSpec((1, tm), lambda i: (0, i))]
        + [_const_spec(c.shape) for c in consts],
        out_specs=out_specs,
        scratch_shapes=[pltpu.VMEM((tm + 2 * SUBLANES, 2 * MLSTM_WIDTH), F32)],
        compiler_params=pltpu.CompilerParams(dimension_semantics=("arbitrary",),
                                             vmem_limit_bytes=VMEM_LIMIT),
        name="in_proj",
    )(x2, pos_row, *consts)


def _attn_body(qt_ref, k_ref, vt_ref, o_ref, m_ref, acc_ref, sa_ref, sb_ref, *, tq, heads):
    qi = pl.program_id(2)
    m_ref[...] = jnp.full(m_ref.shape, -jnp.inf, F32)
    acc_ref[...] = jnp.zeros(acc_ref.shape, F32)

    def scores(ki, s_ref):
        start = pl.multiple_of(ki * tq, tq)
        for h in range(heads):
            k = k_ref[pl.ds(start, tq), h * QK_PAD:(h + 1) * QK_PAD]
            s_ref[h] = _dot(k, qt_ref[0, h * QK_PAD:(h + 1) * QK_PAD, :])

    def softmax_pv(ki, s_ref, diagonal):
        for h in range(heads):
            s = s_ref[h]
            if diagonal:
                r = lax.broadcasted_iota(jnp.int32, s.shape, 0)
                c = lax.broadcasted_iota(jnp.int32, s.shape, 1)
                s = jnp.where(r <= c, s, -jnp.inf)
            m_old = m_ref[h, 0:1, :]
            m_new = jnp.maximum(m_old, jnp.max(s, axis=0, keepdims=True))
            alpha = jnp.exp2(m_old - m_new)
            p = jnp.exp2(s - m_new).astype(BF16)
            pv = _dot(vt_ref[ki, h * V_ROWS:(h + 1) * V_ROWS, :], p)
            acc_ref[h] = alpha * acc_ref[h] + pv
            m_ref[h] = jnp.broadcast_to(m_new, m_ref.shape[1:])

    scores(0, sa_ref)

    def pair(j, carry):
        ki = 2 * j
        scores(ki + 1, sb_ref)
        softmax_pv(ki, sa_ref, False)
        scores(ki + 2, sa_ref)
        softmax_pv(ki + 1, sb_ref, False)
        return carry

    lax.fori_loop(0, qi // 2, pair, 0)

    @pl.when(lax.rem(qi, 2) == 1)
    def _():
        scores(qi, sb_ref)
        softmax_pv(qi - 1, sa_ref, False)
        softmax_pv(qi, sb_ref, True)

    @pl.when(lax.rem(qi, 2) == 0)
    def _():
        softmax_pv(qi, sa_ref, True)

    for h in range(heads):
        acc = acc_ref[h]
        out_t = acc[0:V_HEAD_DIM] / acc[V_HEAD_DIM:V_HEAD_DIM + 1]
        o_ref[:, h * V_HEAD_DIM:(h + 1) * V_HEAD_DIM] = out_t.T.astype(o_ref.dtype)


def _attention(qt, k, vt, batch, seq):
    t = k.shape[0]
    tq = TQ
    nq = seq // tq
    heads = HEADS_PER_STEP
    return pl.pallas_call(
        functools.partial(_attn_body, tq=tq, heads=heads),
        out_shape=jax.ShapeDtypeStruct((t, MLA_WIDTH), BF16),
        grid=(batch, MLA_HEADS // heads, nq),
        in_specs=[
            pl.BlockSpec((1, heads * QK_PAD, tq), lambda b, g, i: (b * nq + i, g, 0)),
            pl.BlockSpec((seq, heads * QK_PAD), lambda b, g, i: (b, g)),
            pl.BlockSpec((nq, heads * V_ROWS, tq), lambda b, g, i: (b, g, 0)),
        ],
        out_specs=pl.BlockSpec((tq, heads * V_HEAD_DIM), lambda b, g, i: (b * nq + i, g)),
        scratch_shapes=[pltpu.VMEM((heads, SUBLANES, tq), F32),
                        pltpu.VMEM((heads, V_ROWS, tq), F32),
                        pltpu.VMEM((heads, tq, tq), F32),
                        pltpu.VMEM((heads, tq, tq), F32)],
        compiler_params=pltpu.CompilerParams(
            dimension_semantics=("arbitrary", "arbitrary", "arbitrary"),
            vmem_limit_bytes=VMEM_LIMIT),
        name="mla_attention",
    )(qt, k, vt)


def _mlstm_body(q_ref, k_ref, v_ref, og_ref, gate_ref, norm_ref, out_ref, c_ref, n_ref, m_ref,
                *, chunk):
    @pl.when(pl.program_id(1) == 0)
    def _():
        c_ref[...] = jnp.zeros(c_ref.shape, F32)
        n_ref[...] = jnp.zeros(n_ref.shape, F32)
        m_ref[...] = jnp.zeros(m_ref.shape, F32)

    slab = gate_ref[...]
    logf = jax.nn.log_sigmoid(slab)
    r = lax.broadcasted_iota(jnp.int32, (chunk, chunk), 0)
    c = lax.broadcasted_iota(jnp.int32, (chunk, chunk), 1)
    tril = c <= r
    bslab = jnp.dot(tril.astype(F32), logf, precision=lax.Precision.HIGHEST,
                    preferred_element_type=F32)
    slab_t = slab.T
    bslab_t = bslab.T

    for h in range(MLSTM_HEADS):
        hs = slice(h * MLSTM_HEAD_DIM, (h + 1) * MLSTM_HEAD_DIM)
        q = q_ref[:, hs]
        k = k_ref[:, hs]
        v = v_ref[:, hs]
        i_col = slab[:, GATE_I + h:GATE_I + h + 1]
        b_col = bslab[:, GATE_F + h:GATE_F + h + 1]
        i_row = slab_t[GATE_I + h:GATE_I + h + 1, :]
        b_row = bslab_t[GATE_F + h:GATE_F + h + 1, :]
        g = b_col[chunk - 1:chunk, :]
        m_prev = m_ref[h, 0:1, 0:1]

        d = jnp.where(tril, b_col - b_row + i_row, -jnp.inf)
        m_inter = b_col + m_prev
        m_j = jnp.maximum(m_inter, jnp.max(d, axis=-1, keepdims=True))
        s = _dot_nt(q, k) * jnp.exp(d - m_j)
        inter = jnp.exp(m_inter - m_j)
        cmat = c_ref[h]
        num = _dot(s.astype(BF16), v) + inter * _dot(q, cmat.astype(BF16))
        qn = jnp.sum(q.astype(F32) * n_ref[h, 0:1, :], axis=-1, keepdims=True)
        den = jnp.sum(s, axis=-1, keepdims=True) + inter * qn
        hh = num / jnp.maximum(jnp.abs(den), jnp.exp(-m_j))

        a_col = g - b_col + i_col
        a_row = g - b_row + i_row
        m_new = jnp.maximum(g + m_prev, jnp.max(a_row, axis=-1, keepdims=True))
        decay = jnp.exp(g + m_prev - m_new)
        kw = k.astype(F32) * jnp.exp(a_col - m_new)
        c_ref[h] = decay * cmat + _dot(kw.T.astype(BF16), v)
        n_ref[h, 0:1, :] = decay * n_ref[h, 0:1, :] + jnp.sum(kw, axis=0, keepdims=True)
        m_ref[h] = jnp.broadcast_to(m_new, m_ref.shape[1:])

        hn = _rms(hh, norm_ref[0:1, hs])
        out_ref[:, hs] = (og_ref[:, hs].astype(F32) * hn).astype(out_ref.dtype)


def _mlstm(mq, mk, mv, mo, gates, norm, batch, seq):
    t = mq.shape[0]
    chunk = CHUNK
    nc = seq // chunk
    row = lambda width: pl.BlockSpec((chunk, width), lambda b, j: (b * nc + j, 0))
    return pl.pallas_call(
        functools.partial(_mlstm_body, chunk=chunk),
        out_shape=jax.ShapeDtypeStruct((t, MLSTM_WIDTH), BF16),
        grid=(batch, nc),
        in_specs=[row(MLSTM_WIDTH)] * 4 + [row(LANES), _const_spec(norm.shape)],
        out_specs=row(MLSTM_WIDTH),
        scratch_shapes=[
            pltpu.VMEM((MLSTM_HEADS, MLSTM_HEAD_DIM, MLSTM_HEAD_DIM), F32),
            pltpu.VMEM((MLSTM_HEADS, SUBLANES, MLSTM_HEAD_DIM), F32),
            pltpu.VMEM((MLSTM_HEADS, SUBLANES, LANES), F32),
        ],
        compiler_params=pltpu.CompilerParams(dimension_semantics=("arbitrary", "arbitrary"),
                                             vmem_limit_bytes=VMEM_LIMIT),
        name="mlstm",
    )(mq, mk, mv, mo, gates, norm)


def _out_ffn_body(x_ref, mla_ref, mlstm_ref, p_ref, w_out_ref, g_post_ref, g_fpre_ref, g_fpost_ref,
                  w_gate_ref, w_up_ref, w_down_ref, w_pg_ref, w_pp_ref, o_ref):
    mix = (_dot(mla_ref[...], w_out_ref[0:MLA_WIDTH, :])
           + _dot(mlstm_ref[...], w_out_ref[MLA_WIDTH:MLA_WIDTH + MLSTM_WIDTH, :]))
    h1 = x_ref[...] + _rms(mix, g_post_ref[...])
    f = _rms(h1, g_fpre_ref[...]).astype(BF16)
    ffn = None
    off = 0
    for width in FF_CHUNKS:
        gate = _dot(f, w_gate_ref[:, off:off + width])
        up = _dot(f, w_up_ref[:, off:off + width])
        act = (gate * jax.nn.sigmoid(gate) * up).astype(BF16)
        part = _dot(act, w_down_ref[off:off + width, :])
        ffn = part if ffn is None else ffn + part
        off += width
    h2 = h1 + _rms(ffn, g_fpost_ref[...])
    pgate = jax.nn.sigmoid(_dot(h2.astype(BF16), w_pg_ref[...]))
    o_ref[...] = h2 + pgate * _dot(p_ref[...].astype(BF16), w_pp_ref[...])


def _out_ffn(x2, mla, mlstm, p2, w_out, g_post, g_fpre, g_fpost, w_gate, w_up, w_down, w_pg, w_pp):
    t = x2.shape[0]
    tm = TM_OUT
    row = lambda width: pl.BlockSpec((tm, width), lambda i: (i, 0))
    consts = (w_out, g_post, g_fpre, g_fpost, w_gate, w_up, w_down, w_pg, w_pp)
    return pl.pallas_call(
        _out_ffn_body,
        out_shape=jax.ShapeDtypeStruct((t, D_MODEL), F32),
        grid=(t // tm,),
        in_specs=[row(D_MODEL), row(MLA_WIDTH), row(MLSTM_WIDTH), row(PLE_DIM)]
        + [_const_spec(c.shape) for c in consts],
        out_specs=row(D_MODEL),
        compiler_params=pltpu.CompilerParams(dimension_semantics=("arbitrary",),
                                             vmem_limit_bytes=VMEM_LIMIT),
        name="out_ffn",
    )(x2, mla, mlstm, p2, *consts)


def _pack_w_in(w_in):
    o_krope = Q_LORA_RANK + KV_LORA_RANK
    o_mq = o_krope + QK_ROPE_DIM
    o_gates = o_mq + 4 * MLSTM_WIDTH
    main = jnp.concatenate([w_in[:, :o_krope], w_in[:, o_mq:o_gates]], axis=1)
    tail = jnp.concatenate([w_in[:, o_krope:o_mq], w_in[:, o_gates:]], axis=1).T
    tail = jnp.pad(tail, ((0, LANES - tail.shape[0]), (0, 0)))
    return main.astype(BF16), tail.astype(BF16)


def _pack_w_uq_t(w_uq):
    w = w_uq.T.reshape(MLA_HEADS, QK_HEAD_DIM, Q_LORA_RANK)
    w = jnp.pad(w, ((0, 0), (0, QK_PAD - QK_HEAD_DIM), (0, 0)))
    return w.reshape(MLA_HEADS * QK_PAD, Q_LORA_RANK).astype(BF16)


def _pack_w_ukv(w_ukv):
    w = w_ukv.reshape(KV_LORA_RANK, MLA_HEADS, QK_NOPE_DIM + V_HEAD_DIM)
    w_uk = w[:, :, :QK_NOPE_DIM].reshape(KV_LORA_RANK, MLA_HEADS * QK_NOPE_DIM)
    w_uv_t = jnp.transpose(w[:, :, QK_NOPE_DIM:], (1, 2, 0))
    w_uv_t = jnp.pad(w_uv_t, ((0, 0), (0, V_ROWS - V_HEAD_DIM), (0, 0)))
    return w_uk.astype(BF16), w_uv_t.reshape(MLA_HEADS * V_ROWS, KV_LORA_RANK).astype(BF16)


def _layer(h, p_i, pos_row, invf, attn_pre_norm, attn_post_norm, w_in, q_norm, kv_norm, w_uq, w_ukv,
           conv_w, conv_b, gate_bias_i, gate_bias_f, mlstm_norm, w_out, ffn_pre_norm, ffn_post_norm,
           w_gate, w_up, w_down, w_ple_proj, w_ple_gate, batch, seq):
    row = lambda a: a.reshape(1, -1)
    w_in_p, w_tail = _pack_w_in(w_in)
    w_uk, w_uv_t = _pack_w_ukv(w_ukv)
    gbias = jnp.zeros((LANES, 1), F32)
    gbias = gbias.at[GATE_I:GATE_I + MLSTM_HEADS, 0].set(gate_bias_i)
    gbias = gbias.at[GATE_F:GATE_F + MLSTM_HEADS, 0].set(gate_bias_f)
    vones = jnp.zeros((MLA_HEADS, V_ROWS, 1), F32).at[:, V_HEAD_DIM, 0].set(1.0)
    qt, k, vt, mq, mk, mv, mo, gates = _in_proj(
        h, pos_row, invf, row(attn_pre_norm), w_in_p, w_tail, row(q_norm), row(kv_norm),
        _pack_w_uq_t(w_uq), w_uk, w_uv_t, vones.reshape(MLA_HEADS * V_ROWS, 1), conv_w, row(conv_b),
        gbias, seq)
    mla = _attention(qt, k, vt, batch, seq)
    mlstm = _mlstm(mq, mk, mv, mo, gates, row(mlstm_norm), batch, seq)
    return _out_ffn(h, mla, mlstm, p_i, w_out.astype(BF16), row(attn_post_norm), row(ffn_pre_norm),
                    row(ffn_post_norm), w_gate.astype(BF16), w_up.astype(BF16),
                    w_down.astype(BF16), w_ple_gate.astype(BF16), w_ple_proj.astype(BF16))


@jax.jit
def kernel(x, p, positions, attn_pre_norm, attn_post_norm, w_in, q_norm, kv_norm, w_uq, w_ukv,
           conv_w, conv_b, gate_bias_i, gate_bias_f, mlstm_norm, w_out, ffn_pre_norm, ffn_post_norm,
           w_gate, w_up, w_down, w_ple_proj, w_ple_gate):
    batch, seq, _ = x.shape
    t = batch * seq
    depth = p.shape[0]
    inv_freq = ROPE_THETA ** (-jnp.arange(0, QK_ROPE_DIM, 2, dtype=F32) / QK_ROPE_DIM)
    invf = inv_freq.reshape(ROPE_HALF, 1)
    pos_row = positions.astype(F32).reshape(1, t)
    h = x.reshape(t, D_MODEL)
    for i in range(depth):
        h = _layer(h, p[i].reshape(t, PLE_DIM), pos_row, invf, attn_pre_norm[i], attn_post_norm[i],
                   w_in[i], q_norm[i], kv_norm[i], w_uq[i], w_ukv[i], conv_w[i], conv_b[i],
                   gate_bias_i[i], gate_bias_f[i], mlstm_norm[i], w_out[i], ffn_pre_norm[i],
                   ffn_post_norm[i], w_gate[i], w_up[i], w_down[i], w_ple_proj[i], w_ple_gate[i],
                   batch, seq)
    return h.reshape(batch, seq, D_MODEL)
```

```python
import functools
import math

import jax
import jax.numpy as jnp
from jax import lax
from jax.experimental import pallas as pl
from jax.experimental.pallas import tpu as pltpu

F32 = jnp.float32
BF16 = jnp.bfloat16

D_MODEL = 1024
PLE_DIM = 256
MLA_HEADS = 4
QK_NOPE_DIM = 128
QK_ROPE_DIM = 64
QK_HEAD_DIM = QK_NOPE_DIM + QK_ROPE_DIM
V_HEAD_DIM = 128
Q_LORA_RANK = 256
KV_LORA_RANK = 128
ROPE_THETA = 10000.0
MLA_WIDTH = MLA_HEADS * V_HEAD_DIM
MLSTM_HEADS = 4
MLSTM_HEAD_DIM = 128
MLSTM_WIDTH = MLSTM_HEADS * MLSTM_HEAD_DIM
CONV_WIDTH = 4
D_FF = 2816
EPS = 1e-6

LANES = 128
SUBLANES = 8
BF16_ROWS = 16
QK_PAD = 2 * LANES
V_ROWS = V_HEAD_DIM + BF16_ROWS
ROPE_HALF = QK_ROPE_DIM // 2
VMEM_LIMIT = 56 * 1024 * 1024

C_Q = 0
C_KV = C_Q + Q_LORA_RANK
C_MQK = C_KV + KV_LORA_RANK
C_MO = C_MQK + 2 * MLSTM_WIDTH
D_IN_PACKED = C_MO + MLSTM_WIDTH
GATE_I = QK_ROPE_DIM
GATE_F = GATE_I + SUBLANES
ROW_KEY, ROW_KMAX, ROW_B, GATE_ROWS = 0, SUBLANES, 2 * SUBLANES, 3 * SUBLANES

TM_IN = 512
TQ = 512
HEADS_PER_STEP = 2
CHUNK = 256
TM_OUT = 512
FF_CHUNKS = (1024, 1024, 768)


def _rms(x, g):
    return x * lax.rsqrt(jnp.mean(x * x, axis=-1, keepdims=True) + EPS) * g


def _dot(a, b):
    return jnp.dot(a, b, preferred_element_type=F32)


def _dot_nt(a, b):
    return lax.dot_general(a, b, (((1,), (1,)), ((), ())), preferred_element_type=F32)


def _const_spec(shape):
    return pl.BlockSpec(shape, lambda *_: (0,) * len(shape), pipeline_mode=pl.Buffered(1))


def _in_proj_body(x_ref, pos_ref, invf_ref, g_pre_ref, w_in_ref, w_tail_ref, qn_ref, kvn_ref,
                  w_uq_ref, w_uk_ref, w_uv_ref, w_mv_ref, vones_ref, conv_w_ref, conv_b_ref, gbias_ref,
                  qt_out, k_out, vt_out, mqt_out, mk_out, mvt_out, mo_out, grow_out, keycol_out,
                  ext_ref, *, tm, tiles_per_seq, chunk):
    i = pl.program_id(0)
    halo = SUBLANES

    @pl.when(lax.rem(i, tiles_per_seq) == 0)
    def _():
        ext_ref[0:halo, :] = jnp.zeros((halo, 2 * MLSTM_WIDTH), F32)

    u = _rms(x_ref[...], g_pre_ref[...]).astype(BF16)

    ang = invf_ref[...] * pos_ref[...]
    cos = jnp.cos(ang)
    sin = jnp.sin(ang)

    def rope_t(t1, t2):
        return t1 * cos - t2 * sin, t2 * cos + t1 * sin

    scale = QK_HEAD_DIM ** -0.5 * math.log2(math.e)
    cq = _rms(_dot(u, w_in_ref[:, C_Q:C_Q + Q_LORA_RANK]), qn_ref[...]).astype(BF16)
    qt = _dot_nt(w_uq_ref[...], cq)
    for h in range(MLA_HEADS):
        o = h * QK_PAD
        pe = o + QK_NOPE_DIM
        r1, r2 = rope_t(qt[pe:pe + ROPE_HALF], qt[pe + ROPE_HALF:pe + QK_ROPE_DIM])
        qt_out[0, o:pe, :] = (qt[o:pe] * scale).astype(BF16)
        qt_out[0, pe:pe + ROPE_HALF, :] = (r1 * scale).astype(BF16)
        qt_out[0, pe + ROPE_HALF:pe + QK_ROPE_DIM, :] = (r2 * scale).astype(BF16)
        qt_out[0, pe + QK_ROPE_DIM:o + QK_PAD, :] = jnp.zeros((QK_PAD - QK_HEAD_DIM, tm), BF16)

    ckv = _rms(_dot(u, w_in_ref[:, C_KV:C_KV + KV_LORA_RANK]), kvn_ref[...]).astype(BF16)
    kn = _dot(ckv, w_uk_ref[...])
    vt_out[0] = (_dot_nt(w_uv_ref[...], ckv) + vones_ref[...]).astype(BF16)

    tail_t = _dot_nt(w_tail_ref[...], u)
    r1, r2 = rope_t(tail_t[0:ROPE_HALF], tail_t[ROPE_HALF:QK_ROPE_DIM])
    kpe_t = jnp.concatenate([r1, r2, jnp.zeros((LANES - QK_ROPE_DIM, tm), F32)], axis=0)
    kpe = kpe_t.T.astype(BF16)
    for h in range(MLA_HEADS):
        o = h * QK_PAD
        k_out[:, o:o + LANES] = kn[:, h * LANES:(h + 1) * LANES].astype(BF16)
        k_out[:, o + LANES:o + QK_PAD] = kpe

    gates = tail_t[GATE_I:GATE_I + 2 * SUBLANES] + gbias_ref[...]
    pos_in_chunk = lax.rem(lax.broadcasted_iota(jnp.int32, (SUBLANES, tm), 1), chunk)

    def chunk_scan(v, op, identity):
        shift = 1
        while shift < chunk:
            v = op(v, jnp.where(pos_in_chunk >= shift, pltpu.roll(v, shift, 1), identity))
            shift *= 2
        return v

    b8 = chunk_scan(jax.nn.log_sigmoid(gates[SUBLANES:]), jnp.add, 0.0)
    key8 = gates[:SUBLANES] - b8
    grow_out[...] = jnp.concatenate([key8, chunk_scan(key8, jnp.maximum, -jnp.inf), b8], axis=0)
    keycol_out[...] = jnp.concatenate([key8, jnp.zeros((LANES - SUBLANES, tm), F32)], axis=0).T

    mvt_out[0] = (_dot_nt(w_mv_ref[...], u) + vones_ref[...]).astype(BF16)
    mo_out[...] = jax.nn.sigmoid(_dot(u, w_in_ref[:, C_MO:C_MO + MLSTM_WIDTH])).astype(BF16)

    ext_ref[halo:halo + tm, :] = _dot(u, w_in_ref[:, C_MQK:C_MQK + 2 * MLSTM_WIDTH])
    acc = conv_b_ref[...]
    for j in range(CONV_WIDTH):
        start = halo - (CONV_WIDTH - 1) + j
        acc = acc + conv_w_ref[j:j + 1, :] * ext_ref[start:start + tm, :]
    qk = acc * jax.nn.sigmoid(acc)
    mqt_out[0] = qk[:, :MLSTM_WIDTH].T.astype(BF16)
    mk_out[...] = (qk[:, MLSTM_WIDTH:] * (MLSTM_HEAD_DIM ** -0.5)).astype(BF16)
    ext_ref[0:halo, :] = ext_ref[tm:tm + halo, :]


def _in_proj(x2, pos_row, invf, g_pre, w_in_p, w_tail, qn, kvn, w_uq_t, w_uk, w_uv_t, w_mv_t, vones,
             conv_w, conv_b, gbias, seq):
    t = x2.shape[0]
    tm = TM_IN
    nt = t // tm
    row = lambda width: pl.BlockSpec((tm, width), lambda i: (i, 0))
    tile_t = lambda rows: pl.BlockSpec((1, rows, tm), lambda i: (i, 0, 0))
    bf = lambda *shape: jax.ShapeDtypeStruct(shape, BF16)
    out_shape = [
        bf(nt, MLA_HEADS * QK_PAD, tm), bf(t, MLA_HEADS * QK_PAD), bf(nt, MLA_HEADS * V_ROWS, tm),
        bf(nt, MLSTM_WIDTH, tm), bf(t, MLSTM_WIDTH), bf(nt, MLSTM_HEADS * V_ROWS, tm),
        bf(t, MLSTM_WIDTH), jax.ShapeDtypeStruct((GATE_ROWS, t), F32),
        jax.ShapeDtypeStruct((t, LANES), F32),
    ]
    out_specs = [
        tile_t(MLA_HEADS * QK_PAD), row(MLA_HEADS * QK_PAD), tile_t(MLA_HEADS * V_ROWS),
        tile_t(MLSTM_WIDTH), row(MLSTM_WIDTH), tile_t(MLSTM_HEADS * V_ROWS), row(MLSTM_WIDTH),
        pl.BlockSpec((GATE_ROWS, tm), lambda i: (0, i)), row(LANES),
    ]
    consts = (invf, g_pre, w_in_p, w_tail, qn, kvn, w_uq_t, w_uk, w_uv_t, w_mv_t, vones, conv_w,
              conv_b, gbias)
    return pl.pallas_call(
        functools.partial(_in_proj_body, tm=tm, tiles_per_seq=seq // tm, chunk=CHUNK),
        out_shape=out_shape,
        grid=(nt,),
        in_specs=[row(D_MODEL), pl.BlockSpec((1, tm), lambda i: (0, i))]
        + [_const_spec(c.shape) for c in consts],
        out_specs=out_specs,
        scratch_shapes=[pltpu.VMEM((tm + 2 * SUBLANES, 2 * MLSTM_WIDTH), F32)],
        compiler_params=pltpu.CompilerParams(dimension_semantics=("arbitrary",),
                                             vmem_limit_bytes=VMEM_LIMIT),
        name="in_proj",
    )(x2, pos_row, *consts)


def _attn_body(qt_ref, k_ref, vt_ref, o_ref, m_ref, acc_ref, sa_ref, sb_ref, *, tq, heads):
    qi = pl.program_id(2)
    m_ref[...] = jnp.full(m_ref.shape, -jnp.inf, F32)
    acc_ref[...] = jnp.zeros(acc_ref.shape, F32)

    def scores(ki, s_ref):
        start = pl.multiple_of(ki * tq, tq)
        for h in range(heads):
            k = k_ref[pl.ds(start, tq), h * QK_PAD:(h + 1) * QK_PAD]
            s_ref[h] = _dot(k, qt_ref[0, h * QK_PAD:(h + 1) * QK_PAD, :])

    def softmax_pv(ki, s_ref, diagonal):
        for h in range(heads):
            s = s_ref[h]
            if diagonal:
                r = lax.broadcasted_iota(jnp.int32, s.shape, 0)
                c = lax.broadcasted_iota(jnp.int32, s.shape, 1)
                s = jnp.where(r <= c, s, -jnp.inf)
            m_old = m_ref[h, 0:1, :]
            m_new = jnp.maximum(m_old, jnp.max(s, axis=0, keepdims=True))
            alpha = jnp.exp2(m_old - m_new)
            p = jnp.exp2(s - m_new).astype(BF16)
            pv = _dot(vt_ref[ki, h * V_ROWS:(h + 1) * V_ROWS, :], p)
            acc_ref[h] = alpha * acc_ref[h] + pv
            m_ref[h] = jnp.broadcast_to(m_new, m_ref.shape[1:])

    scores(0, sa_ref)

    def pair(j, carry):
        ki = 2 * j
        scores(ki + 1, sb_ref)
        softmax_pv(ki, sa_ref, False)
        scores(ki + 2, sa_ref)
        softmax_pv(ki + 1, sb_ref, False)
        return carry

    lax.fori_loop(0, qi // 2, pair, 0)

    @pl.when(lax.rem(qi, 2) == 1)
    def _():
        scores(qi, sb_ref)
        softmax_pv(qi - 1, sa_ref, False)
        softmax_pv(qi, sb_ref, True)

    @pl.when(lax.rem(qi, 2) == 0)
    def _():
        softmax_pv(qi, sa_ref, True)

    for h in range(heads):
        acc = acc_ref[h]
        out_t = acc[0:V_HEAD_DIM] / acc[V_HEAD_DIM:V_HEAD_DIM + 1]
        o_ref[:, h * V_HEAD_DIM:(h + 1) * V_HEAD_DIM] = out_t.T.astype(o_ref.dtype)


def _attention(qt, k, vt, batch, seq):
    t = k.shape[0]
    tq = TQ
    nq = seq // tq
    heads = HEADS_PER_STEP
    return pl.pallas_call(
        functools.partial(_attn_body, tq=tq, heads=heads),
        out_shape=jax.ShapeDtypeStruct((t, MLA_WIDTH), BF16),
        grid=(batch, MLA_HEADS // heads, nq),
        in_specs=[
            pl.BlockSpec((1, heads * QK_PAD, tq), lambda b, g, i: (b * nq + i, g, 0)),
            pl.BlockSpec((seq, heads * QK_PAD), lambda b, g, i: (b, g)),
            pl.BlockSpec((nq, heads * V_ROWS, tq), lambda b, g, i: (b, g, 0)),
        ],
        out_specs=pl.BlockSpec((tq, heads * V_HEAD_DIM), lambda b, g, i: (b * nq + i, g)),
        scratch_shapes=[pltpu.VMEM((heads, SUBLANES, tq), F32),
                        pltpu.VMEM((heads, V_ROWS, tq), F32),
                        pltpu.VMEM((heads, tq, tq), F32),
                        pltpu.VMEM((heads, tq, tq), F32)],
        compiler_params=pltpu.CompilerParams(
            dimension_semantics=("arbitrary", "arbitrary", "arbitrary"),
            vmem_limit_bytes=VMEM_LIMIT),
        name="mla_attention",
    )(qt, k, vt)


def _mlstm_body(qt_ref, k_ref, vt_ref, og_ref, grow_ref, keycol_ref, norm_ref, out_ref, ct_ref,
                m_ref, *, chunk):
    @pl.when(pl.program_id(1) == 0)
    def _():
        ct_ref[...] = jnp.zeros(ct_ref.shape, F32)
        m_ref[...] = jnp.zeros(m_ref.shape, F32)

    heads = range(MLSTM_HEADS)
    hs = [slice(h * MLSTM_HEAD_DIM, (h + 1) * MLSTM_HEAD_DIM) for h in heads]
    qt = [qt_ref[0, hs[h], :] for h in heads]
    k = [k_ref[:, hs[h]] for h in heads]
    vt = [vt_ref[0, h * V_ROWS:(h + 1) * V_ROWS, :] for h in heads]
    ct = [ct_ref[h] for h in heads]
    m_prev = [m_ref[h, 0:1, 0:1] for h in heads]

    s_raw = [_dot(k[h], qt[h]) for h in heads]
    cq = [_dot(ct[h].astype(BF16), qt[h]) for h in heads]

    grow = grow_ref[...]
    mm = [jnp.maximum(m_prev[h], grow[ROW_KMAX + h:ROW_KMAX + h + 1]) for h in heads]
    b_row = [grow[ROW_B + h:ROW_B + h + 1] for h in heads]
    for h in heads:
        mm_last = mm[h][:, chunk - 1:chunk]
        w_row = jnp.exp(grow[ROW_KEY + h:ROW_KEY + h + 1] - mm_last)
        vtw = (vt[h].astype(F32) * w_row).astype(BF16)
        ct_ref[h] = jnp.exp(m_prev[h] - mm_last) * ct[h] + _dot(vtw, k[h])
        m_ref[h] = jnp.broadcast_to(b_row[h][:, chunk - 1:chunk] + mm_last, m_ref.shape[1:])

    r = lax.broadcasted_iota(jnp.int32, (chunk, chunk), 0)
    c = lax.broadcasted_iota(jnp.int32, (chunk, chunk), 1)
    upper = r <= c
    keycol = keycol_ref[...]
    for h in heads:
        p = jnp.exp(jnp.where(upper, keycol[:, h:h + 1] - mm[h], -jnp.inf))
        st = (s_raw[h] * p).astype(BF16)
        tot = _dot(vt[h], st) + jnp.exp(m_prev[h] - mm[h]) * cq[h]
        den = tot[MLSTM_HEAD_DIM:MLSTM_HEAD_DIM + 1]
        ht = tot[0:MLSTM_HEAD_DIM] / jnp.maximum(jnp.abs(den), jnp.exp(-(b_row[h] + mm[h])))
        hn_t = ht * lax.rsqrt(jnp.mean(ht * ht, axis=0, keepdims=True) + EPS)
        out_ref[:, hs[h]] = (hn_t.T * norm_ref[0:1, hs[h]]
                             * og_ref[:, hs[h]].astype(F32)).astype(out_ref.dtype)


def _mlstm(mqt, mk, mvt, mo, grow, keycol, norm, batch, seq):
    t = mk.shape[0]
    chunk = CHUNK
    nc = seq // chunk
    per_tile = TM_IN // chunk
    row = lambda width: pl.BlockSpec((chunk, width), lambda b, j: (b * nc + j, 0))
    tile_t = lambda rows: pl.BlockSpec(
        (1, rows, chunk), lambda b, j: ((b * nc + j) // per_tile, 0, (b * nc + j) % per_tile))
    return pl.pallas_call(
        functools.partial(_mlstm_body, chunk=chunk),
        out_shape=jax.ShapeDtypeStruct((t, MLSTM_WIDTH), BF16),
        grid=(batch, nc),
        in_specs=[tile_t(MLSTM_WIDTH), row(MLSTM_WIDTH), tile_t(MLSTM_HEADS * V_ROWS),
                  row(MLSTM_WIDTH), pl.BlockSpec((GATE_ROWS, chunk), lambda b, j: (0, b * nc + j)),
                  row(LANES), _const_spec(norm.shape)],
        out_specs=row(MLSTM_WIDTH),
        scratch_shapes=[
            pltpu.VMEM((MLSTM_HEADS, V_ROWS, MLSTM_HEAD_DIM), F32),
            pltpu.VMEM((MLSTM_HEADS, SUBLANES, LANES), F32),
        ],
        compiler_params=pltpu.CompilerParams(dimension_semantics=("arbitrary", "arbitrary"),
                                             vmem_limit_bytes=VMEM_LIMIT),
        name="mlstm",
    )(mqt, mk, mvt, mo, grow, keycol, norm)


def _out_ffn_body(x_ref, mla_ref, mlstm_ref, p_ref, w_out_ref, g_post_ref, g_fpre_ref, g_fpost_ref,
                  w_gate_ref, w_up_ref, w_down_ref, w_pg_ref, w_pp_ref, o_ref):
    mix = (_dot(mla_ref[...], w_out_ref[0:MLA_WIDTH, :])
           + _dot(mlstm_ref[...], w_out_ref[MLA_WIDTH:MLA_WIDTH + MLSTM_WIDTH, :]))
    h1 = x_ref[...] + _rms(mix, g_post_ref[...])
    f = _rms(h1, g_fpre_ref[...]).astype(BF16)
    ffn = None
    off = 0
    for width in FF_CHUNKS:
        gate = _dot(f, w_gate_ref[:, off:off + width])
        up = _dot(f, w_up_ref[:, off:off + width])
        act = (gate * jax.nn.sigmoid(gate) * up).astype(BF16)
        part = _dot(act, w_down_ref[off:off + width, :])
        ffn = part if ffn is None else ffn + part
        off += width
    h2 = h1 + _rms(ffn, g_fpost_ref[...])
    pgate = jax.nn.sigmoid(_dot(h2.astype(BF16), w_pg_ref[...]))
    o_ref[...] = h2 + pgate * _dot(p_ref[...].astype(BF16), w_pp_ref[...])


def _out_ffn(x2, mla, mlstm, p2, w_out, g_post, g_fpre, g_fpost, w_gate, w_up, w_down, w_pg, w_pp):
    t = x2.shape[0]
    tm = TM_OUT
    row = lambda width: pl.BlockSpec((tm, width), lambda i: (i, 0))
    consts = (w_out, g_post, g_fpre, g_fpost, w_gate, w_up, w_down, w_pg, w_pp)
    return pl.pallas_call(
        _out_ffn_body,
        out_shape=jax.ShapeDtypeStruct((t, D_MODEL), F32),
        grid=(t // tm,),
        in_specs=[row(D_MODEL), row(MLA_WIDTH), row(MLSTM_WIDTH), row(PLE_DIM)]
        + [_const_spec(c.shape) for c in consts],
        out_specs=row(D_MODEL),
        compiler_params=pltpu.CompilerParams(dimension_semantics=("arbitrary",),
                                             vmem_limit_bytes=VMEM_LIMIT),
        name="out_ffn",
    )(x2, mla, mlstm, p2, *consts)


def _pack_w_in(w_in):
    o_krope = Q_LORA_RANK + KV_LORA_RANK
    o_mq = o_krope + QK_ROPE_DIM
    o_mv = o_mq + 2 * MLSTM_WIDTH
    o_mo = o_mv + MLSTM_WIDTH
    o_gates = o_mo + MLSTM_WIDTH
    main = jnp.concatenate([w_in[:, :o_krope], w_in[:, o_mq:o_mv], w_in[:, o_mo:o_gates]], axis=1)
    gap = jnp.zeros((D_MODEL, SUBLANES - MLSTM_HEADS), w_in.dtype)
    tail = jnp.concatenate([w_in[:, o_krope:o_mq], w_in[:, o_gates:o_gates + MLSTM_HEADS], gap,
                            w_in[:, o_gates + MLSTM_HEADS:], gap], axis=1).T
    tail = jnp.pad(tail, ((0, LANES - tail.shape[0]), (0, 0)))
    mv_t = w_in[:, o_mv:o_mo].T.reshape(MLSTM_HEADS, MLSTM_HEAD_DIM, D_MODEL)
    mv_t = jnp.pad(mv_t, ((0, 0), (0, V_ROWS - MLSTM_HEAD_DIM), (0, 0)))
    return main.astype(BF16), tail.astype(BF16), mv_t.reshape(MLSTM_HEADS * V_ROWS, D_MODEL).astype(BF16)


def _pack_w_uq_t(w_uq):
    w = w_uq.T.reshape(MLA_HEADS, QK_HEAD_DIM, Q_LORA_RANK)
    w = jnp.pad(w, ((0, 0), (0, QK_PAD - QK_HEAD_DIM), (0, 0)))
    return w.reshape(MLA_HEADS * QK_PAD, Q_LORA_RANK).astype(BF16)


def _pack_w_ukv(w_ukv):
    w = w_ukv.reshape(KV_LORA_RANK, MLA_HEADS, QK_NOPE_DIM + V_HEAD_DIM)
    w_uk = w[:, :, :QK_NOPE_DIM].reshape(KV_LORA_RANK, MLA_HEADS * QK_NOPE_DIM)
    w_uv_t = jnp.transpose(w[:, :, QK_NOPE_DIM:], (1, 2, 0))
    w_uv_t = jnp.pad(w_uv_t, ((0, 0), (0, V_ROWS - V_HEAD_DIM), (0, 0)))
    return w_uk.astype(BF16), w_uv_t.reshape(MLA_HEADS * V_ROWS, KV_LORA_RANK).astype(BF16)


def _layer(h, p_i, pos_row, invf, attn_pre_norm, attn_post_norm, w_in, q_norm, kv_norm, w_uq, w_ukv,
           conv_w, conv_b, gate_bias_i, gate_bias_f, mlstm_norm, w_out, ffn_pre_norm, ffn_post_norm,
           w_gate, w_up, w_down, w_ple_proj, w_ple_gate, batch, seq):
    row = lambda a: a.reshape(1, -1)
    w_in_p, w_tail, w_mv_t = _pack_w_in(w_in)
    w_uk, w_uv_t = _pack_w_ukv(w_ukv)
    gap = jnp.zeros((SUBLANES - MLSTM_HEADS,), F32)
    gbias = jnp.concatenate([gate_bias_i, gap, gate_bias_f, gap]).astype(F32).reshape(2 * SUBLANES, 1)
    vones = jnp.zeros((MLA_HEADS, V_ROWS, 1), F32).at[:, V_HEAD_DIM, 0].set(1.0)
    qt, k, vt, mqt, mk, mvt, mo, grow, keycol = _in_proj(
        h, pos_row, invf, row(attn_pre_norm), w_in_p, w_tail, row(q_norm), row(kv_norm),
        _pack_w_uq_t(w_uq), w_uk, w_uv_t, w_mv_t, vones.reshape(MLA_HEADS * V_ROWS, 1), conv_w,
        row(conv_b), gbias, seq)
    mla = _attention(qt, k, vt, batch, seq)
    mlstm = _mlstm(mqt, mk, mvt, mo, grow, keycol, row(mlstm_norm), batch, seq)
    return _out_ffn(h, mla, mlstm, p_i, w_out.astype(BF16), row(attn_post_norm), row(ffn_pre_norm),
                    row(ffn_post_norm), w_gate.astype(BF16), w_up.astype(BF16),
                    w_down.astype(BF16), w_ple_gate.astype(BF16), w_ple_proj.astype(BF16))


@jax.jit
def kernel(x, p, positions, attn_pre_norm, attn_post_norm, w_in, q_norm, kv_norm, w_uq, w_ukv,
           conv_w, conv_b, gate_bias_i, gate_bias_f, mlstm_norm, w_out, ffn_pre_norm, ffn_post_norm,
           w_gate, w_up, w_down, w_ple_proj, w_ple_gate):
    batch, seq, _ = x.shape
    t = batch * seq
    depth = p.shape[0]
    inv_freq = ROPE_THETA ** (-jnp.arange(0, QK_ROPE_DIM, 2, dtype=F32) / QK_ROPE_DIM)
    invf = inv_freq.reshape(ROPE_HALF, 1)
    pos_row = positions.astype(F32).reshape(1, t)
    h = x.reshape(t, D_MODEL)
    for i in range(depth):
        h = _layer(h, p[i].reshape(t, PLE_DIM), pos_row, invf, attn_pre_norm[i], attn_post_norm[i],
                   w_in[i], q_norm[i], kv_norm[i], w_uq[i], w_ukv[i], conv_w[i], conv_b[i],
                   gate_bias_i[i], gate_bias_f[i], mlstm_norm[i], w_out[i], ffn_pre_norm[i],
                   ffn_post_norm[i], w_gate[i], w_up[i], w_down[i], w_ple_proj[i], w_ple_gate[i],
                   batch, seq)
    return h.reshape(batch, seq, D_MODEL)
```

```python
import functools
import math

import jax
import jax.numpy as jnp
from jax import lax
from jax.experimental import pallas as pl
from jax.experimental.pallas import tpu as pltpu

F32 = jnp.float32
BF16 = jnp.bfloat16

D_MODEL = 1024
PLE_DIM = 256
MLA_HEADS = 4
QK_NOPE_DIM = 128
QK_ROPE_DIM = 64
QK_HEAD_DIM = QK_NOPE_DIM + QK_ROPE_DIM
V_HEAD_DIM = 128
Q_LORA_RANK = 256
KV_LORA_RANK = 128
ROPE_THETA = 10000.0
MLA_WIDTH = MLA_HEADS * V_HEAD_DIM
MLSTM_HEADS = 4
MLSTM_HEAD_DIM = 128
MLSTM_WIDTH = MLSTM_HEADS * MLSTM_HEAD_DIM
CONV_WIDTH = 4
D_FF = 2816
EPS = 1e-6

LANES = 128
SUBLANES = 8
BF16_ROWS = 16
QK_PAD = 2 * LANES
V_ROWS = V_HEAD_DIM + BF16_ROWS
ROPE_HALF = QK_ROPE_DIM // 2
VMEM_LIMIT = 56 * 1024 * 1024

C_Q = 0
C_KV = C_Q + Q_LORA_RANK
C_MQK = C_KV + KV_LORA_RANK
C_MO = C_MQK + 2 * MLSTM_WIDTH
D_IN_PACKED = C_MO + MLSTM_WIDTH
GATE_I = QK_ROPE_DIM
GATE_F = GATE_I + SUBLANES
ROW_KEY, ROW_KMAX, ROW_B, GATE_ROWS = 0, SUBLANES, 2 * SUBLANES, 3 * SUBLANES

TILE_T = 512
TM_IN = 512
SUB_IN = 256
TQ = TILE_T
HEADS_PER_STEP = 2
CHUNK = 256
TM_OUT = 512
SUB_OUT = 256
FF_CHUNKS = (1024, 1024, 768)


def _rms(x, g):
    return x * lax.rsqrt(jnp.mean(x * x, axis=-1, keepdims=True) + EPS) * g


def _dot(a, b):
    return jnp.dot(a, b, preferred_element_type=F32)


def _dot_nt(a, b):
    return lax.dot_general(a, b, (((1,), (1,)), ((), ())), preferred_element_type=F32)


def _const_spec(shape):
    return pl.BlockSpec(shape, lambda *_: (0,) * len(shape), pipeline_mode=pl.Buffered(1))


def _in_proj_body(x_ref, pos_ref, invf_ref, g_pre_ref, w_in_ref, w_tail_ref, qn_ref, kvn_ref,
                  w_uq_ref, w_uk_ref, w_uv_ref, w_mv_ref, vones_ref, conv_w_ref, conv_b_ref, gbias_ref,
                  qt_out, k_out, vt_out, mqt_out, mk_out, mvt_out, mo_out, grow_out, keycol_out,
                  ext_ref, *, tm, sub, tiles_per_seq, chunk):
    assert tm % sub == 0 and sub % chunk == 0
    i = pl.program_id(0)
    halo = SUBLANES

    @pl.when(lax.rem(i, tiles_per_seq) == 0)
    def _():
        ext_ref[0:halo, :] = jnp.zeros((halo, 2 * MLSTM_WIDTH), F32)

    for r0 in range(0, tm, sub):
        _in_proj_rows(r0, sub, halo, chunk, x_ref, pos_ref, invf_ref, g_pre_ref, w_in_ref, w_tail_ref,
                      qn_ref, kvn_ref, w_uq_ref, w_uk_ref, w_uv_ref, w_mv_ref, vones_ref, conv_w_ref,
                      conv_b_ref, gbias_ref, qt_out, k_out, vt_out, mqt_out, mk_out, mvt_out, mo_out,
                      grow_out, keycol_out, ext_ref)
    ext_ref[0:halo, :] = ext_ref[tm:tm + halo, :]


def _in_proj_rows(r0, ts, halo, chunk, x_ref, pos_ref, invf_ref, g_pre_ref, w_in_ref, w_tail_ref,
                  qn_ref, kvn_ref, w_uq_ref, w_uk_ref, w_uv_ref, w_mv_ref, vones_ref, conv_w_ref,
                  conv_b_ref, gbias_ref, qt_out, k_out, vt_out, mqt_out, mk_out, mvt_out, mo_out,
                  grow_out, keycol_out, ext_ref):
    rows = slice(r0, r0 + ts)
    tile, lanes = r0 // TILE_T, slice(r0 % TILE_T, r0 % TILE_T + ts)
    u = _rms(x_ref[rows, :], g_pre_ref[...]).astype(BF16)

    cq_raw = _dot(u, w_in_ref[:, C_Q:C_Q + Q_LORA_RANK])
    ckv_raw = _dot(u, w_in_ref[:, C_KV:C_KV + KV_LORA_RANK])
    tail_t = _dot_nt(w_tail_ref[...], u)
    ext_ref[halo + r0:halo + r0 + ts, :] = _dot(u, w_in_ref[:, C_MQK:C_MQK + 2 * MLSTM_WIDTH])
    mvt_out[tile, :, lanes] = (_dot_nt(w_mv_ref[...], u) + vones_ref[...]).astype(BF16)
    mo_out[rows, :] = jax.nn.sigmoid(_dot(u, w_in_ref[:, C_MO:C_MO + MLSTM_WIDTH])).astype(BF16)
    cq = _rms(cq_raw, qn_ref[...]).astype(BF16)
    ckv = _rms(ckv_raw, kvn_ref[...]).astype(BF16)
    qt = _dot_nt(w_uq_ref[...], cq)
    kn = _dot(ckv, w_uk_ref[...])
    vt_out[tile, :, lanes] = (_dot_nt(w_uv_ref[...], ckv) + vones_ref[...]).astype(BF16)

    ang = invf_ref[...] * pos_ref[:, rows]
    cos = jnp.cos(ang)
    sin = jnp.sin(ang)

    def rope_t(t1, t2):
        return t1 * cos - t2 * sin, t2 * cos + t1 * sin

    scale = QK_HEAD_DIM ** -0.5 * math.log2(math.e)
    for h in range(MLA_HEADS):
        o = h * QK_PAD
        pe = o + QK_NOPE_DIM
        r1, r2 = rope_t(qt[pe:pe + ROPE_HALF], qt[pe + ROPE_HALF:pe + QK_ROPE_DIM])
        qt_out[tile, o:pe, lanes] = (qt[o:pe] * scale).astype(BF16)
        qt_out[tile, pe:pe + ROPE_HALF, lanes] = (r1 * scale).astype(BF16)
        qt_out[tile, pe + ROPE_HALF:pe + QK_ROPE_DIM, lanes] = (r2 * scale).astype(BF16)
        qt_out[tile, pe + QK_ROPE_DIM:o + QK_PAD, lanes] = jnp.zeros((QK_PAD - QK_HEAD_DIM, ts), BF16)

    r1, r2 = rope_t(tail_t[0:ROPE_HALF], tail_t[ROPE_HALF:QK_ROPE_DIM])
    kpe_t = jnp.concatenate([r1, r2, jnp.zeros((LANES - QK_ROPE_DIM, ts), F32)], axis=0)
    kpe = kpe_t.T.astype(BF16)
    for h in range(MLA_HEADS):
        o = h * QK_PAD
        k_out[rows, o:o + LANES] = kn[:, h * LANES:(h + 1) * LANES].astype(BF16)
        k_out[rows, o + LANES:o + QK_PAD] = kpe

    gates = tail_t[GATE_I:GATE_I + 2 * SUBLANES] + gbias_ref[...]
    pos_in_chunk = lax.rem(r0 + lax.broadcasted_iota(jnp.int32, (SUBLANES, ts), 1), chunk)

    def chunk_scan(v, op, identity):
        shift = 1
        while shift < chunk:
            v = op(v, jnp.where(pos_in_chunk >= shift, pltpu.roll(v, shift, 1), identity))
            shift *= 2
        return v

    b8 = chunk_scan(jax.nn.log_sigmoid(gates[SUBLANES:]), jnp.add, 0.0)
    key8 = gates[:SUBLANES] - b8
    grow_out[:, rows] = jnp.concatenate([key8, chunk_scan(key8, jnp.maximum, -jnp.inf), b8], axis=0)
    keycol_out[rows, :] = jnp.concatenate([key8, jnp.zeros((LANES - SUBLANES, ts), F32)], axis=0).T

    acc = conv_b_ref[...]
    for j in range(CONV_WIDTH):
        start = halo - (CONV_WIDTH - 1) + j + r0
        acc = acc + conv_w_ref[j:j + 1, :] * ext_ref[start:start + ts, :]
    qk = acc * jax.nn.sigmoid(acc)
    mqt_out[tile, :, lanes] = qk[:, :MLSTM_WIDTH].T.astype(BF16)
    mk_out[rows, :] = (qk[:, MLSTM_WIDTH:] * (MLSTM_HEAD_DIM ** -0.5)).astype(BF16)


def _in_proj(x2, pos_row, invf, g_pre, w_in_p, w_tail, qn, kvn, w_uq_t, w_uk, w_uv_t, w_mv_t, vones,
             conv_w, conv_b, gbias, seq):
    t = x2.shape[0]
    tm = TM_IN
    nt = t // TILE_T
    row = lambda width: pl.BlockSpec((tm, width), lambda i: (i, 0))
    tile_t = lambda rows: pl.BlockSpec((tm // TILE_T, rows, TILE_T), lambda i: (i, 0, 0))
    bf = lambda *shape: jax.ShapeDtypeStruct(shape, BF16)
    out_shape = [
        bf(nt, MLA_HEADS * QK_PAD, TILE_T), bf(t, MLA_HEADS * QK_PAD), bf(nt, MLA_HEADS * V_ROWS, TILE_T),
        bf(nt, MLSTM_WIDTH, TILE_T), bf(t, MLSTM_WIDTH), bf(nt, MLSTM_HEADS * V_ROWS, TILE_T),
        bf(t, MLSTM_WIDTH), jax.ShapeDtypeStruct((GATE_ROWS, t), F32),
        jax.ShapeDtypeStruct((t, LANES), F32),
    ]
    out_specs = [
        tile_t(MLA_HEADS * QK_PAD), row(MLA_HEADS * QK_PAD), tile_t(MLA_HEADS * V_ROWS),
        tile_t(MLSTM_WIDTH), row(MLSTM_WIDTH), tile_t(MLSTM_HEADS * V_ROWS), row(MLSTM_WIDTH),
        pl.BlockSpec((GATE_ROWS, tm), lambda i: (0, i)), row(LANES),
    ]
    consts = (invf, g_pre, w_in_p, w_tail, qn, kvn, w_uq_t, w_uk, w_uv_t, w_mv_t, vones, conv_w,
              conv_b, gbias)
    return pl.pallas_call(
        functools.partial(_in_proj_body, tm=tm, sub=SUB_IN, tiles_per_seq=seq // tm, chunk=CHUNK),
        out_shape=out_shape,
        grid=(t // tm,),
        in_specs=[row(D_MODEL), pl.BlockSpec((1, tm), lambda i: (0, i))]
        + [_const_spec(c.shape) for c in consts],
        out_specs=out_specs,
        scratch_shapes=[pltpu.VMEM((tm + 2 * SUBLANES, 2 * MLSTM_WIDTH), F32)],
        compiler_params=pltpu.CompilerParams(dimension_semantics=("arbitrary",),
                                             vmem_limit_bytes=VMEM_LIMIT),
        name="in_proj",
    )(x2, pos_row, *consts)


def _attn_body(qt_ref, k_ref, vt_ref, o_ref, m_ref, acc_ref, sa_ref, sb_ref, *, tq, heads):
    qi = pl.program_id(2)
    m_ref[...] = jnp.full(m_ref.shape, -jnp.inf, F32)
    acc_ref[...] = jnp.zeros(acc_ref.shape, F32)

    def scores(ki, s_ref):
        start = pl.multiple_of(ki * tq, tq)
        for h in range(heads):
            k = k_ref[pl.ds(start, tq), h * QK_PAD:(h + 1) * QK_PAD]
            s_ref[h] = _dot(k, qt_ref[0, h * QK_PAD:(h + 1) * QK_PAD, :])

    def softmax_pv(ki, s_ref, diagonal):
        for h in range(heads):
            s = s_ref[h]
            if diagonal:
                r = lax.broadcasted_iota(jnp.int32, s.shape, 0)
                c = lax.broadcasted_iota(jnp.int32, s.shape, 1)
                s = jnp.where(r <= c, s, -jnp.inf)
            m_old = m_ref[h, 0:1, :]
            m_new = jnp.maximum(m_old, jnp.max(s, axis=0, keepdims=True))
            alpha = jnp.exp2(m_old - m_new)
            p = jnp.exp2(s - m_new).astype(BF16)
            pv = _dot(vt_ref[ki, h * V_ROWS:(h + 1) * V_ROWS, :], p)
            acc_ref[h] = alpha * acc_ref[h] + pv
            m_ref[h] = jnp.broadcast_to(m_new, m_ref.shape[1:])

    scores(0, sa_ref)

    def pair(j, carry):
        ki = 2 * j
        scores(ki + 1, sb_ref)
        softmax_pv(ki, sa_ref, False)
        scores(ki + 2, sa_ref)
        softmax_pv(ki + 1, sb_ref, False)
        return carry

    lax.fori_loop(0, qi // 2, pair, 0)

    @pl.when(lax.rem(qi, 2) == 1)
    def _():
        scores(qi, sb_ref)
        softmax_pv(qi - 1, sa_ref, False)
        softmax_pv(qi, sb_ref, True)

    @pl.when(lax.rem(qi, 2) == 0)
    def _():
        softmax_pv(qi, sa_ref, True)

    for h in range(heads):
        acc = acc_ref[h]
        out_t = acc[0:V_HEAD_DIM] / acc[V_HEAD_DIM:V_HEAD_DIM + 1]
        o_ref[:, h * V_HEAD_DIM:(h + 1) * V_HEAD_DIM] = out_t.T.astype(o_ref.dtype)


def _attention(qt, k, vt, batch, seq):
    t = k.shape[0]
    tq = TQ
    nq = seq // tq
    heads = HEADS_PER_STEP
    return pl.pallas_call(
        functools.partial(_attn_body, tq=tq, heads=heads),
        out_shape=jax.ShapeDtypeStruct((t, MLA_WIDTH), BF16),
        grid=(batch, MLA_HEADS // heads, nq),
        in_specs=[
            pl.BlockSpec((1, heads * QK_PAD, tq), lambda b, g, i: (b * nq + i, g, 0)),
            pl.BlockSpec((seq, heads * QK_PAD), lambda b, g, i: (b, g)),
            pl.BlockSpec((nq, heads * V_ROWS, tq), lambda b, g, i: (b, g, 0)),
        ],
        out_specs=pl.BlockSpec((tq, heads * V_HEAD_DIM), lambda b, g, i: (b * nq + i, g)),
        scratch_shapes=[pltpu.VMEM((heads, SUBLANES, tq), F32),
                        pltpu.VMEM((heads, V_ROWS, tq), F32),
                        pltpu.VMEM((heads, tq, tq), F32),
                        pltpu.VMEM((heads, tq, tq), F32)],
        compiler_params=pltpu.CompilerParams(
            dimension_semantics=("arbitrary", "arbitrary", "arbitrary"),
            vmem_limit_bytes=VMEM_LIMIT),
        name="mla_attention",
    )(qt, k, vt)


def _mlstm_body(qt_ref, k_ref, vt_ref, og_ref, grow_ref, keycol_ref, norm_ref, out_ref, ct_ref,
                m_ref, *, chunk):
    @pl.when(pl.program_id(1) == 0)
    def _():
        ct_ref[...] = jnp.zeros(ct_ref.shape, F32)
        m_ref[...] = jnp.zeros(m_ref.shape, F32)

    heads = range(MLSTM_HEADS)
    hs = [slice(h * MLSTM_HEAD_DIM, (h + 1) * MLSTM_HEAD_DIM) for h in heads]
    qt = [qt_ref[0, hs[h], :] for h in heads]
    k = [k_ref[:, hs[h]] for h in heads]
    vt = [vt_ref[0, h * V_ROWS:(h + 1) * V_ROWS, :] for h in heads]
    ct = [ct_ref[h] for h in heads]
    m_prev = [m_ref[h, 0:1, 0:1] for h in heads]

    s_raw = [_dot(k[h], qt[h]) for h in heads]
    cq = [_dot(ct[h].astype(BF16), qt[h]) for h in heads]

    grow = grow_ref[...]
    mm = [jnp.maximum(m_prev[h], grow[ROW_KMAX + h:ROW_KMAX + h + 1]) for h in heads]
    b_row = [grow[ROW_B + h:ROW_B + h + 1] for h in heads]
    for h in heads:
        mm_last = mm[h][:, chunk - 1:chunk]
        w_row = jnp.exp(grow[ROW_KEY + h:ROW_KEY + h + 1] - mm_last)
        vtw = (vt[h].astype(F32) * w_row).astype(BF16)
        ct_ref[h] = jnp.exp(m_prev[h] - mm_last) * ct[h] + _dot(vtw, k[h])
        m_ref[h] = jnp.broadcast_to(b_row[h][:, chunk - 1:chunk] + mm_last, m_ref.shape[1:])

    r = lax.broadcasted_iota(jnp.int32, (chunk, chunk), 0)
    c = lax.broadcasted_iota(jnp.int32, (chunk, chunk), 1)
    upper = r <= c
    keycol = keycol_ref[...]
    for h in heads:
        p = jnp.exp(jnp.where(upper, keycol[:, h:h + 1] - mm[h], -jnp.inf))
        st = (s_raw[h] * p).astype(BF16)
        tot = _dot(vt[h], st) + jnp.exp(m_prev[h] - mm[h]) * cq[h]
        den = tot[MLSTM_HEAD_DIM:MLSTM_HEAD_DIM + 1]
        ht = tot[0:MLSTM_HEAD_DIM] / jnp.maximum(jnp.abs(den), jnp.exp(-(b_row[h] + mm[h])))
        hn_t = ht * lax.rsqrt(jnp.mean(ht * ht, axis=0, keepdims=True) + EPS)
        out_ref[:, hs[h]] = (hn_t.T * norm_ref[0:1, hs[h]]
                             * og_ref[:, hs[h]].astype(F32)).astype(out_ref.dtype)


def _mlstm(mqt, mk, mvt, mo, grow, keycol, norm, batch, seq):
    t = mk.shape[0]
    chunk = CHUNK
    nc = seq // chunk
    per_tile = TILE_T // chunk
    row = lambda width: pl.BlockSpec((chunk, width), lambda b, j: (b * nc + j, 0))
    tile_t = lambda rows: pl.BlockSpec(
        (1, rows, chunk), lambda b, j: ((b * nc + j) // per_tile, 0, (b * nc + j) % per_tile))
    return pl.pallas_call(
        functools.partial(_mlstm_body, chunk=chunk),
        out_shape=jax.ShapeDtypeStruct((t, MLSTM_WIDTH), BF16),
        grid=(batch, nc),
        in_specs=[tile_t(MLSTM_WIDTH), row(MLSTM_WIDTH), tile_t(MLSTM_HEADS * V_ROWS),
                  row(MLSTM_WIDTH), pl.BlockSpec((GATE_ROWS, chunk), lambda b, j: (0, b * nc + j)),
                  row(LANES), _const_spec(norm.shape)],
        out_specs=row(MLSTM_WIDTH),
        scratch_shapes=[
            pltpu.VMEM((MLSTM_HEADS, V_ROWS, MLSTM_HEAD_DIM), F32),
            pltpu.VMEM((MLSTM_HEADS, SUBLANES, LANES), F32),
        ],
        compiler_params=pltpu.CompilerParams(dimension_semantics=("arbitrary", "arbitrary"),
                                             vmem_limit_bytes=VMEM_LIMIT),
        name="mlstm",
    )(mqt, mk, mvt, mo, grow, keycol, norm)


def _out_ffn_body(x_ref, mla_ref, mlstm_ref, p_ref, w_out_ref, g_post_ref, g_fpre_ref, g_fpost_ref,
                  w_gate_ref, w_up_ref, w_down_ref, w_pg_ref, w_pp_ref, o_ref):
    tm = x_ref.shape[0]
    subs = [slice(r0, r0 + SUB_OUT) for r0 in range(0, tm, SUB_OUT)]
    mix = [_dot(mla_ref[s, :], w_out_ref[0:MLA_WIDTH, :])
           + _dot(mlstm_ref[s, :], w_out_ref[MLA_WIDTH:MLA_WIDTH + MLSTM_WIDTH, :]) for s in subs]
    h1 = [x_ref[s, :] + _rms(m, g_post_ref[...]) for s, m in zip(subs, mix)]
    f = [_rms(h, g_fpre_ref[...]).astype(BF16) for h in h1]
    ffn = [None] * len(subs)
    off = 0
    for width in FF_CHUNKS:
        cols = slice(off, off + width)
        gate = [_dot(v, w_gate_ref[:, cols]) for v in f]
        up = [_dot(v, w_up_ref[:, cols]) for v in f]
        act = [(g * jax.nn.sigmoid(g) * u).astype(BF16) for g, u in zip(gate, up)]
        part = [_dot(a, w_down_ref[cols, :]) for a in act]
        ffn = [p if acc is None else acc + p for acc, p in zip(ffn, part)]
        off += width
    h2 = [h + _rms(v, g_fpost_ref[...]) for h, v in zip(h1, ffn)]
    pgate = [jax.nn.sigmoid(_dot(h.astype(BF16), w_pg_ref[...])) for h in h2]
    for s, h, g in zip(subs, h2, pgate):
        o_ref[s, :] = h + g * _dot(p_ref[s, :].astype(BF16), w_pp_ref[...])


def _out_ffn(x2, mla, mlstm, p2, w_out, g_post, g_fpre, g_fpost, w_gate, w_up, w_down, w_pg, w_pp):
    t = x2.shape[0]
    tm = TM_OUT
    row = lambda width: pl.BlockSpec((tm, width), lambda i: (i, 0))
    consts = (w_out, g_post, g_fpre, g_fpost, w_gate, w_up, w_down, w_pg, w_pp)
    return pl.pallas_call(
        _out_ffn_body,
        out_shape=jax.ShapeDtypeStruct((t, D_MODEL), F32),
        grid=(t // tm,),
        in_specs=[row(D_MODEL), row(MLA_WIDTH), row(MLSTM_WIDTH), row(PLE_DIM)]
        + [_const_spec(c.shape) for c in consts],
        out_specs=row(D_MODEL),
        compiler_params=pltpu.CompilerParams(dimension_semantics=("arbitrary",),
                                             vmem_limit_bytes=VMEM_LIMIT),
        name="out_ffn",
    )(x2, mla, mlstm, p2, *consts)


def _pack_w_in(w_in):
    o_krope = Q_LORA_RANK + KV_LORA_RANK
    o_mq = o_krope + QK_ROPE_DIM
    o_mv = o_mq + 2 * MLSTM_WIDTH
    o_mo = o_mv + MLSTM_WIDTH
    o_gates = o_mo + MLSTM_WIDTH
    main = jnp.concatenate([w_in[:, :o_krope], w_in[:, o_mq:o_mv], w_in[:, o_mo:o_gates]], axis=1)
    gap = jnp.zeros((D_MODEL, SUBLANES - MLSTM_HEADS), w_in.dtype)
    tail = jnp.concatenate([w_in[:, o_krope:o_mq], w_in[:, o_gates:o_gates + MLSTM_HEADS], gap,
                            w_in[:, o_gates + MLSTM_HEADS:], gap], axis=1).T
    tail = jnp.pad(tail, ((0, LANES - tail.shape[0]), (0, 0)))
    mv_t = w_in[:, o_mv:o_mo].T.reshape(MLSTM_HEADS, MLSTM_HEAD_DIM, D_MODEL)
    mv_t = jnp.pad(mv_t, ((0, 0), (0, V_ROWS - MLSTM_HEAD_DIM), (0, 0)))
    return main.astype(BF16), tail.astype(BF16), mv_t.reshape(MLSTM_HEADS * V_ROWS, D_MODEL).astype(BF16)


def _pack_w_uq_t(w_uq):
    w = w_uq.T.reshape(MLA_HEADS, QK_HEAD_DIM, Q_LORA_RANK)
    w = jnp.pad(w, ((0, 0), (0, QK_PAD - QK_HEAD_DIM), (0, 0)))
    return w.reshape(MLA_HEADS * QK_PAD, Q_LORA_RANK).astype(BF16)


def _pack_w_ukv(w_ukv):
    w = w_ukv.reshape(KV_LORA_RANK, MLA_HEADS, QK_NOPE_DIM + V_HEAD_DIM)
    w_uk = w[:, :, :QK_NOPE_DIM].reshape(KV_LORA_RANK, MLA_HEADS * QK_NOPE_DIM)
    w_uv_t = jnp.transpose(w[:, :, QK_NOPE_DIM:], (1, 2, 0))
    w_uv_t = jnp.pad(w_uv_t, ((0, 0), (0, V_ROWS - V_HEAD_DIM), (0, 0)))
    return w_uk.astype(BF16), w_uv_t.reshape(MLA_HEADS * V_ROWS, KV_LORA_RANK).astype(BF16)


def _layer(h, p_i, pos_row, invf, attn_pre_norm, attn_post_norm, w_in, q_norm, kv_norm, w_uq, w_ukv,
           conv_w, conv_b, gate_bias_i, gate_bias_f, mlstm_norm, w_out, ffn_pre_norm, ffn_post_norm,
           w_gate, w_up, w_down, w_ple_proj, w_ple_gate, batch, seq):
    row = lambda a: a.reshape(1, -1)
    w_in_p, w_tail, w_mv_t = _pack_w_in(w_in)
    w_uk, w_uv_t = _pack_w_ukv(w_ukv)
    gap = jnp.zeros((SUBLANES - MLSTM_HEADS,), F32)
    gbias = jnp.concatenate([gate_bias_i, gap, gate_bias_f, gap]).astype(F32).reshape(2 * SUBLANES, 1)
    vones = jnp.zeros((MLA_HEADS, V_ROWS, 1), F32).at[:, V_HEAD_DIM, 0].set(1.0)
    qt, k, vt, mqt, mk, mvt, mo, grow, keycol = _in_proj(
        h, pos_row, invf, row(attn_pre_norm), w_in_p, w_tail, row(q_norm), row(kv_norm),
        _pack_w_uq_t(w_uq), w_uk, w_uv_t, w_mv_t, vones.reshape(MLA_HEADS * V_ROWS, 1), conv_w,
        row(conv_b), gbias, seq)
    mla = _attention(qt, k, vt, batch, seq)
    mlstm = _mlstm(mqt, mk, mvt, mo, grow, keycol, row(mlstm_norm), batch, seq)
    return _out_ffn(h, mla, mlstm, p_i, w_out.astype(BF16), row(attn_post_norm), row(ffn_pre_norm),
                    row(ffn_post_norm), w_gate.astype(BF16), w_up.astype(BF16),
                    w_down.astype(BF16), w_ple_gate.astype(BF16), w_ple_proj.astype(BF16))


@jax.jit
def kernel(x, p, positions, attn_pre_norm, attn_post_norm, w_in, q_norm, kv_norm, w_uq, w_ukv,
           conv_w, conv_b, gate_bias_i, gate_bias_f, mlstm_norm, w_out, ffn_pre_norm, ffn_post_norm,
           w_gate, w_up, w_down, w_ple_proj, w_ple_gate):
    batch, seq, _ = x.shape
    t = batch * seq
    depth = p.shape[0]
    inv_freq = ROPE_THETA ** (-jnp.arange(0, QK_ROPE_DIM, 2, dtype=F32) / QK_ROPE_DIM)
    invf = inv_freq.reshape(ROPE_HALF, 1)
    pos_row = positions.astype(F32).reshape(1, t)
    h = x.reshape(t, D_MODEL)
    for i in range(depth):
        h = _layer(h, p[i].reshape(t, PLE_DIM), pos_row, invf, attn_pre_norm[i], attn_post_norm[i],
                   w_in[i], q_norm[i], kv_norm[i], w_uq[i], w_ukv[i], conv_w[i], conv_b[i],
                   gate_bias_i[i], gate_bias_f[i], mlstm_norm[i], w_out[i], ffn_pre_norm[i],
                   ffn_post_norm[i], w_gate[i], w_up[i], w_down[i], w_ple_proj[i], w_ple_gate[i],
                   batch, seq)
    return h.reshape(batch, seq, D_MODEL)
```

```python
import functools
import math

import jax
import jax.numpy as jnp
from jax import lax
from jax.experimental import pallas as pl
from jax.experimental.pallas import tpu as pltpu

F32 = jnp.float32
BF16 = jnp.bfloat16

D_MODEL = 1024
PLE_DIM = 256
MLA_HEADS = 4
QK_NOPE_DIM = 128
QK_ROPE_DIM = 64
QK_HEAD_DIM = QK_NOPE_DIM + QK_ROPE_DIM
V_HEAD_DIM = 128
Q_LORA_RANK = 256
KV_LORA_RANK = 128
ROPE_THETA = 10000.0
MLA_WIDTH = MLA_HEADS * V_HEAD_DIM
MLSTM_HEADS = 4
MLSTM_HEAD_DIM = 128
MLSTM_WIDTH = MLSTM_HEADS * MLSTM_HEAD_DIM
CONV_WIDTH = 4
D_FF = 2816
EPS = 1e-6

LANES = 128
SUBLANES = 8
BF16_ROWS = 16
QK_PAD = 2 * LANES
V_ROWS = V_HEAD_DIM + BF16_ROWS
ROPE_HALF = QK_ROPE_DIM // 2
VMEM_LIMIT = 56 * 1024 * 1024

C_Q = 0
C_KV = C_Q + Q_LORA_RANK
C_MQK = C_KV + KV_LORA_RANK
C_MO = C_MQK + 2 * MLSTM_WIDTH
D_IN_PACKED = C_MO + MLSTM_WIDTH
GATE_I = QK_ROPE_DIM
GATE_F = GATE_I + SUBLANES
ROW_KEY, ROW_KMAX, ROW_B, GATE_ROWS = 0, SUBLANES, 2 * SUBLANES, 3 * SUBLANES

TILE_T = 512
TM_IN = 512
SUB_IN = 256
TQ = TILE_T
HEADS_PER_STEP = 2
CHUNK = 256
TM_OUT = 512
SUB_OUT = 256
FF_CHUNKS = (1024, 1024, 768)


def _rms(x, g):
    return x * lax.rsqrt(jnp.mean(x * x, axis=-1, keepdims=True) + EPS) * g


def _dot(a, b):
    return jnp.dot(a, b, preferred_element_type=F32)


def _dot_nt(a, b):
    return lax.dot_general(a, b, (((1,), (1,)), ((), ())), preferred_element_type=F32)


def _const_spec(shape):
    return pl.BlockSpec(shape, lambda *_: (0,) * len(shape), pipeline_mode=pl.Buffered(1))


def _in_proj_body(x_ref, pos_ref, invf_ref, g_pre_ref, w_in_ref, w_tail_ref, qn_ref, kvn_ref,
                  w_uq_ref, w_uk_ref, w_uv_ref, w_mv_ref, vones_ref, conv_w_ref, conv_b_ref, gbias_ref,
                  qt_out, k_out, vt_out, mqt_out, mk_out, mvt_out, mo_out, grow_out, keycol_out,
                  ext_ref, *, tm, sub, tiles_per_seq, chunk):
    assert tm % sub == 0 and sub % chunk == 0
    i = pl.program_id(0)
    halo = SUBLANES

    @pl.when(lax.rem(i, tiles_per_seq) == 0)
    def _():
        ext_ref[0:halo, :] = jnp.zeros((halo, 2 * MLSTM_WIDTH), F32)

    for r0 in range(0, tm, sub):
        _in_proj_rows(r0, sub, halo, chunk, x_ref, pos_ref, invf_ref, g_pre_ref, w_in_ref, w_tail_ref,
                      qn_ref, kvn_ref, w_uq_ref, w_uk_ref, w_uv_ref, w_mv_ref, vones_ref, conv_w_ref,
                      conv_b_ref, gbias_ref, qt_out, k_out, vt_out, mqt_out, mk_out, mvt_out, mo_out,
                      grow_out, keycol_out, ext_ref)
    ext_ref[0:halo, :] = ext_ref[tm:tm + halo, :]


def _in_proj_rows(r0, ts, halo, chunk, x_ref, pos_ref, invf_ref, g_pre_ref, w_in_ref, w_tail_ref,
                  qn_ref, kvn_ref, w_uq_ref, w_uk_ref, w_uv_ref, w_mv_ref, vones_ref, conv_w_ref,
                  conv_b_ref, gbias_ref, qt_out, k_out, vt_out, mqt_out, mk_out, mvt_out, mo_out,
                  grow_out, keycol_out, ext_ref):
    rows = slice(r0, r0 + ts)
    tile, lanes = r0 // TILE_T, slice(r0 % TILE_T, r0 % TILE_T + ts)
    u = _rms(x_ref[rows, :], g_pre_ref[...]).astype(BF16)

    cq_raw = _dot(u, w_in_ref[:, C_Q:C_Q + Q_LORA_RANK])
    ckv_raw = _dot(u, w_in_ref[:, C_KV:C_KV + KV_LORA_RANK])
    tail_t = _dot_nt(w_tail_ref[...], u)
    ext_ref[halo + r0:halo + r0 + ts, :] = _dot(u, w_in_ref[:, C_MQK:C_MQK + 2 * MLSTM_WIDTH])
    mvt_out[tile, :, lanes] = (_dot_nt(w_mv_ref[...], u) + vones_ref[...]).astype(BF16)
    mo_out[rows, :] = jax.nn.sigmoid(_dot(u, w_in_ref[:, C_MO:C_MO + MLSTM_WIDTH])).astype(BF16)
    cq = _rms(cq_raw, qn_ref[...]).astype(BF16)
    ckv = _rms(ckv_raw, kvn_ref[...]).astype(BF16)
    qt = _dot_nt(w_uq_ref[...], cq)
    kn = _dot(ckv, w_uk_ref[...])
    vt_out[tile, :, lanes] = (_dot_nt(w_uv_ref[...], ckv) + vones_ref[...]).astype(BF16)

    ang = invf_ref[...] * pos_ref[:, rows]
    cos = jnp.cos(ang)
    sin = jnp.sin(ang)

    def rope_t(t1, t2):
        return t1 * cos - t2 * sin, t2 * cos + t1 * sin

    scale = QK_HEAD_DIM ** -0.5 * math.log2(math.e)
    for h in range(MLA_HEADS):
        o = h * QK_PAD
        pe = o + QK_NOPE_DIM
        r1, r2 = rope_t(qt[pe:pe + ROPE_HALF], qt[pe + ROPE_HALF:pe + QK_ROPE_DIM])
        qt_out[tile, o:pe, lanes] = (qt[o:pe] * scale).astype(BF16)
        qt_out[tile, pe:pe + ROPE_HALF, lanes] = (r1 * scale).astype(BF16)
        qt_out[tile, pe + ROPE_HALF:pe + QK_ROPE_DIM, lanes] = (r2 * scale).astype(BF16)
        qt_out[tile, pe + QK_ROPE_DIM:o + QK_PAD, lanes] = jnp.zeros((QK_PAD - QK_HEAD_DIM, ts), BF16)

    r1, r2 = rope_t(tail_t[0:ROPE_HALF], tail_t[ROPE_HALF:QK_ROPE_DIM])
    kpe_t = jnp.concatenate([r1, r2, jnp.zeros((LANES - QK_ROPE_DIM, ts), F32)], axis=0)
    kpe = kpe_t.T.astype(BF16)
    for h in range(MLA_HEADS):
        o = h * QK_PAD
        k_out[rows, o:o + LANES] = kn[:, h * LANES:(h + 1) * LANES].astype(BF16)
        k_out[rows, o + LANES:o + QK_PAD] = kpe

    gates = tail_t[GATE_I:GATE_I + 2 * SUBLANES] + gbias_ref[...]
    pos_in_chunk = lax.rem(r0 + lax.broadcasted_iota(jnp.int32, (SUBLANES, ts), 1), chunk)

    def chunk_scan(v, op, identity):
        shift = 1
        while shift < chunk:
            v = op(v, jnp.where(pos_in_chunk >= shift, pltpu.roll(v, shift, 1), identity))
            shift *= 2
        return v

    b8 = chunk_scan(jax.nn.log_sigmoid(gates[SUBLANES:]), jnp.add, 0.0)
    key8 = gates[:SUBLANES] - b8
    grow_out[:, rows] = jnp.concatenate([key8, chunk_scan(key8, jnp.maximum, -jnp.inf), b8], axis=0)
    keycol_out[rows, :] = jnp.concatenate([key8, jnp.zeros((LANES - SUBLANES, ts), F32)], axis=0).T

    acc = conv_b_ref[...]
    for j in range(CONV_WIDTH):
        start = halo - (CONV_WIDTH - 1) + j + r0
        acc = acc + conv_w_ref[j:j + 1, :] * ext_ref[start:start + ts, :]
    qk = acc * jax.nn.sigmoid(acc)
    mqt_out[tile, :, lanes] = qk[:, :MLSTM_WIDTH].T.astype(BF16)
    mk_out[rows, :] = (qk[:, MLSTM_WIDTH:] * (MLSTM_HEAD_DIM ** -0.5)).astype(BF16)


def _in_proj(x2, pos_row, invf, g_pre, w_in_p, w_tail, qn, kvn, w_uq_t, w_uk, w_uv_t, w_mv_t, vones,
             conv_w, conv_b, gbias, seq):
    t = x2.shape[0]
    tm = TM_IN
    nt = t // TILE_T
    row = lambda width: pl.BlockSpec((tm, width), lambda i: (i, 0))
    tile_t = lambda rows: pl.BlockSpec((tm // TILE_T, rows, TILE_T), lambda i: (i, 0, 0))
    bf = lambda *shape: jax.ShapeDtypeStruct(shape, BF16)
    out_shape = [
        bf(nt, MLA_HEADS * QK_PAD, TILE_T), bf(t, MLA_HEADS * QK_PAD), bf(nt, MLA_HEADS * V_ROWS, TILE_T),
        bf(nt, MLSTM_WIDTH, TILE_T), bf(t, MLSTM_WIDTH), bf(nt, MLSTM_HEADS * V_ROWS, TILE_T),
        bf(t, MLSTM_WIDTH), jax.ShapeDtypeStruct((GATE_ROWS, t), F32),
        jax.ShapeDtypeStruct((t, LANES), F32),
    ]
    out_specs = [
        tile_t(MLA_HEADS * QK_PAD), row(MLA_HEADS * QK_PAD), tile_t(MLA_HEADS * V_ROWS),
        tile_t(MLSTM_WIDTH), row(MLSTM_WIDTH), tile_t(MLSTM_HEADS * V_ROWS), row(MLSTM_WIDTH),
        pl.BlockSpec((GATE_ROWS, tm), lambda i: (0, i)), row(LANES),
    ]
    consts = (invf, g_pre, w_in_p, w_tail, qn, kvn, w_uq_t, w_uk, w_uv_t, w_mv_t, vones, conv_w,
              conv_b, gbias)
    return pl.pallas_call(
        functools.partial(_in_proj_body, tm=tm, sub=SUB_IN, tiles_per_seq=seq // tm, chunk=CHUNK),
        out_shape=out_shape,
        grid=(t // tm,),
        in_specs=[row(D_MODEL), pl.BlockSpec((1, tm), lambda i: (0, i))]
        + [_const_spec(c.shape) for c in consts],
        out_specs=out_specs,
        scratch_shapes=[pltpu.VMEM((tm + 2 * SUBLANES, 2 * MLSTM_WIDTH), F32)],
        compiler_params=pltpu.CompilerParams(dimension_semantics=("arbitrary",),
                                             vmem_limit_bytes=VMEM_LIMIT),
        name="in_proj",
    )(x2, pos_row, *consts)


def _attn_body(qt_ref, k_ref, vt_ref, o_ref, m_ref, acc_ref, sa_ref, sb_ref, cma_ref, cmb_ref,
               *, tq, heads):
    qi = pl.program_id(2)
    m_ref[...] = jnp.full(m_ref.shape, -jnp.inf, F32)
    acc_ref[...] = jnp.zeros(acc_ref.shape, F32)

    def scores(ki, s_ref, cm_ref):
        start = pl.multiple_of(ki * tq, tq)
        for h in range(heads):
            k = k_ref[pl.ds(start, tq), h * QK_PAD:(h + 1) * QK_PAD]
            s = _dot(k, qt_ref[0, h * QK_PAD:(h + 1) * QK_PAD, :])
            s_ref[h] = s
            cm_ref[h] = jnp.broadcast_to(jnp.max(s, axis=0, keepdims=True), cm_ref.shape[1:])

    def softmax_pv(ki, s_ref, cm_ref, diagonal):
        for h in range(heads):
            s = s_ref[h]
            if diagonal:
                r = lax.broadcasted_iota(jnp.int32, s.shape, 0)
                c = lax.broadcasted_iota(jnp.int32, s.shape, 1)
                s = jnp.where(r <= c, s, -jnp.inf)
                cmax = jnp.max(s, axis=0, keepdims=True)
            else:
                cmax = cm_ref[h, 0:1, :]
            m_old = m_ref[h, 0:1, :]
            m_new = jnp.maximum(m_old, cmax)
            alpha = jnp.exp2(m_old - m_new)
            p = jnp.exp2(s - m_new).astype(BF16)
            pv = _dot(vt_ref[ki, h * V_ROWS:(h + 1) * V_ROWS, :], p)
            acc_ref[h] = alpha * acc_ref[h] + pv
            m_ref[h] = jnp.broadcast_to(m_new, m_ref.shape[1:])

    scores(0, sa_ref, cma_ref)

    def pair(j, carry):
        ki = 2 * j
        scores(ki + 1, sb_ref, cmb_ref)
        softmax_pv(ki, sa_ref, cma_ref, False)
        scores(ki + 2, sa_ref, cma_ref)
        softmax_pv(ki + 1, sb_ref, cmb_ref, False)
        return carry

    lax.fori_loop(0, qi // 2, pair, 0)

    @pl.when(lax.rem(qi, 2) == 1)
    def _():
        scores(qi, sb_ref, cmb_ref)
        softmax_pv(qi - 1, sa_ref, cma_ref, False)
        softmax_pv(qi, sb_ref, cmb_ref, True)

    @pl.when(lax.rem(qi, 2) == 0)
    def _():
        softmax_pv(qi, sa_ref, cma_ref, True)

    for h in range(heads):
        acc = acc_ref[h]
        out_t = acc[0:V_HEAD_DIM] / acc[V_HEAD_DIM:V_HEAD_DIM + 1]
        o_ref[:, h * V_HEAD_DIM:(h + 1) * V_HEAD_DIM] = out_t.T.astype(o_ref.dtype)


def _attention(qt, k, vt, batch, seq):
    t = k.shape[0]
    tq = TQ
    nq = seq // tq
    heads = HEADS_PER_STEP
    return pl.pallas_call(
        functools.partial(_attn_body, tq=tq, heads=heads),
        out_shape=jax.ShapeDtypeStruct((t, MLA_WIDTH), BF16),
        grid=(batch, MLA_HEADS // heads, nq),
        in_specs=[
            pl.BlockSpec((1, heads * QK_PAD, tq), lambda b, g, i: (b * nq + i, g, 0)),
            pl.BlockSpec((seq, heads * QK_PAD), lambda b, g, i: (b, g)),
            pl.BlockSpec((nq, heads * V_ROWS, tq), lambda b, g, i: (b, g, 0)),
        ],
        out_specs=pl.BlockSpec((tq, heads * V_HEAD_DIM), lambda b, g, i: (b * nq + i, g)),
        scratch_shapes=[pltpu.VMEM((heads, SUBLANES, tq), F32),
                        pltpu.VMEM((heads, V_ROWS, tq), F32),
                        pltpu.VMEM((heads, tq, tq), F32),
                        pltpu.VMEM((heads, tq, tq), F32),
                        pltpu.VMEM((heads, SUBLANES, tq), F32),
                        pltpu.VMEM((heads, SUBLANES, tq), F32)],
        compiler_params=pltpu.CompilerParams(
            dimension_semantics=("arbitrary", "arbitrary", "arbitrary"),
            vmem_limit_bytes=VMEM_LIMIT),
        name="mla_attention",
    )(qt, k, vt)


def _mlstm_body(qt_ref, k_ref, vt_ref, og_ref, grow_ref, keycol_ref, norm_ref, out_ref, ct_ref,
                m_ref, *, chunk):
    @pl.when(pl.program_id(1) == 0)
    def _():
        ct_ref[...] = jnp.zeros(ct_ref.shape, F32)
        m_ref[...] = jnp.zeros(m_ref.shape, F32)

    heads = range(MLSTM_HEADS)
    hs = [slice(h * MLSTM_HEAD_DIM, (h + 1) * MLSTM_HEAD_DIM) for h in heads]
    qt = [qt_ref[0, hs[h], :] for h in heads]
    k = [k_ref[:, hs[h]] for h in heads]
    vt = [vt_ref[0, h * V_ROWS:(h + 1) * V_ROWS, :] for h in heads]
    ct = [ct_ref[h] for h in heads]
    m_prev = [m_ref[h, 0:1, 0:1] for h in heads]

    s_raw = [_dot(k[h], qt[h]) for h in heads]
    cq = [_dot(ct[h].astype(BF16), qt[h]) for h in heads]

    grow = grow_ref[...]
    mm = [jnp.maximum(m_prev[h], grow[ROW_KMAX + h:ROW_KMAX + h + 1]) for h in heads]
    b_row = [grow[ROW_B + h:ROW_B + h + 1] for h in heads]
    for h in heads:
        mm_last = mm[h][:, chunk - 1:chunk]
        w_row = jnp.exp(grow[ROW_KEY + h:ROW_KEY + h + 1] - mm_last)
        vtw = (vt[h].astype(F32) * w_row).astype(BF16)
        ct_ref[h] = jnp.exp(m_prev[h] - mm_last) * ct[h] + _dot(vtw, k[h])
        m_ref[h] = jnp.broadcast_to(b_row[h][:, chunk - 1:chunk] + mm_last, m_ref.shape[1:])

    r = lax.broadcasted_iota(jnp.int32, (chunk, chunk), 0)
    c = lax.broadcasted_iota(jnp.int32, (chunk, chunk), 1)
    upper = r <= c
    keycol = keycol_ref[...]
    for h in heads:
        p = jnp.exp(jnp.where(upper, keycol[:, h:h + 1] - mm[h], -jnp.inf))
        st = (s_raw[h] * p).astype(BF16)
        tot = _dot(vt[h], st) + jnp.exp(m_prev[h] - mm[h]) * cq[h]
        den = tot[MLSTM_HEAD_DIM:MLSTM_HEAD_DIM + 1]
        ht = tot[0:MLSTM_HEAD_DIM] / jnp.maximum(jnp.abs(den), jnp.exp(-(b_row[h] + mm[h])))
        hn_t = ht * lax.rsqrt(jnp.mean(ht * ht, axis=0, keepdims=True) + EPS)
        out_ref[:, hs[h]] = (hn_t.T * norm_ref[0:1, hs[h]]
                             * og_ref[:, hs[h]].astype(F32)).astype(out_ref.dtype)


def _mlstm(mqt, mk, mvt, mo, grow, keycol, norm, batch, seq):
    t = mk.shape[0]
    chunk = CHUNK
    nc = seq // chunk
    per_tile = TILE_T // chunk
    row = lambda width: pl.BlockSpec((chunk, width), lambda b, j: (b * nc + j, 0))
    tile_t = lambda rows: pl.BlockSpec(
        (1, rows, chunk), lambda b, j: ((b * nc + j) // per_tile, 0, (b * nc + j) % per_tile))
    return pl.pallas_call(
        functools.partial(_mlstm_body, chunk=chunk),
        out_shape=jax.ShapeDtypeStruct((t, MLSTM_WIDTH), BF16),
        grid=(batch, nc),
        in_specs=[tile_t(MLSTM_WIDTH), row(MLSTM_WIDTH), tile_t(MLSTM_HEADS * V_ROWS),
                  row(MLSTM_WIDTH), pl.BlockSpec((GATE_ROWS, chunk), lambda b, j: (0, b * nc + j)),
                  row(LANES), _const_spec(norm.shape)],
        out_specs=row(MLSTM_WIDTH),
        scratch_shapes=[
            pltpu.VMEM((MLSTM_HEADS, V_ROWS, MLSTM_HEAD_DIM), F32),
            pltpu.VMEM((MLSTM_HEADS, SUBLANES, LANES), F32),
        ],
        compiler_params=pltpu.CompilerParams(dimension_semantics=("arbitrary", "arbitrary"),
                                             vmem_limit_bytes=VMEM_LIMIT),
        name="mlstm",
    )(mqt, mk, mvt, mo, grow, keycol, norm)


def _out_ffn_body(x_ref, mla_ref, mlstm_ref, p_ref, w_out_ref, g_post_ref, g_fpre_ref, g_fpost_ref,
                  w_gate_ref, w_up_ref, w_down_ref, w_pg_ref, w_pp_ref, o_ref):
    tm = x_ref.shape[0]
    subs = [slice(r0, r0 + SUB_OUT) for r0 in range(0, tm, SUB_OUT)]
    mix = [_dot(mla_ref[s, :], w_out_ref[0:MLA_WIDTH, :])
           + _dot(mlstm_ref[s, :], w_out_ref[MLA_WIDTH:MLA_WIDTH + MLSTM_WIDTH, :]) for s in subs]
    h1 = [x_ref[s, :] + _rms(m, g_post_ref[...]) for s, m in zip(subs, mix)]
    f = [_rms(h, g_fpre_ref[...]).astype(BF16) for h in h1]
    ffn = [None] * len(subs)
    off = 0
    for width in FF_CHUNKS:
        cols = slice(off, off + width)
        gate = [_dot(v, w_gate_ref[:, cols]) for v in f]
        up = [_dot(v, w_up_ref[:, cols]) for v in f]
        act = [(g * jax.nn.sigmoid(g) * u).astype(BF16) for g, u in zip(gate, up)]
        part = [_dot(a, w_down_ref[cols, :]) for a in act]
        ffn = [p if acc is None else acc + p for acc, p in zip(ffn, part)]
        off += width
    h2 = [h + _rms(v, g_fpost_ref[...]) for h, v in zip(h1, ffn)]
    pgate = [jax.nn.sigmoid(_dot(h.astype(BF16), w_pg_ref[...])) for h in h2]
    for s, h, g in zip(subs, h2, pgate):
        o_ref[s, :] = h + g * _dot(p_ref[s, :].astype(BF16), w_pp_ref[...])


def _out_ffn(x2, mla, mlstm, p2, w_out, g_post, g_fpre, g_fpost, w_gate, w_up, w_down, w_pg, w_pp):
    t = x2.shape[0]
    tm = TM_OUT
    row = lambda width: pl.BlockSpec((tm, width), lambda i: (i, 0))
    consts = (w_out, g_post, g_fpre, g_fpost, w_gate, w_up, w_down, w_pg, w_pp)
    return pl.pallas_call(
        _out_ffn_body,
        out_shape=jax.ShapeDtypeStruct((t, D_MODEL), F32),
        grid=(t // tm,),
        in_specs=[row(D_MODEL), row(MLA_WIDTH), row(MLSTM_WIDTH), row(PLE_DIM)]
        + [_const_spec(c.shape) for c in consts],
        out_specs=row(D_MODEL),
        compiler_params=pltpu.CompilerParams(dimension_semantics=("arbitrary",),
                                             vmem_limit_bytes=VMEM_LIMIT),
        name="out_ffn",
    )(x2, mla, mlstm, p2, *consts)


def _pack_w_in(w_in):
    o_krope = Q_LORA_RANK + KV_LORA_RANK
    o_mq = o_krope + QK_ROPE_DIM
    o_mv = o_mq + 2 * MLSTM_WIDTH
    o_mo = o_mv + MLSTM_WIDTH
    o_gates = o_mo + MLSTM_WIDTH
    main = jnp.concatenate([w_in[:, :o_krope], w_in[:, o_mq:o_mv], w_in[:, o_mo:o_gates]], axis=1)
    gap = jnp.zeros((D_MODEL, SUBLANES - MLSTM_HEADS), w_in.dtype)
    tail = jnp.concatenate([w_in[:, o_krope:o_mq], w_in[:, o_gates:o_gates + MLSTM_HEADS], gap,
                            w_in[:, o_gates + MLSTM_HEADS:], gap], axis=1).T
    tail = jnp.pad(tail, ((0, LANES - tail.shape[0]), (0, 0)))
    mv_t = w_in[:, o_mv:o_mo].T.reshape(MLSTM_HEADS, MLSTM_HEAD_DIM, D_MODEL)
    mv_t = jnp.pad(mv_t, ((0, 0), (0, V_ROWS - MLSTM_HEAD_DIM), (0, 0)))
    return main.astype(BF16), tail.astype(BF16), mv_t.reshape(MLSTM_HEADS * V_ROWS, D_MODEL).astype(BF16)


def _pack_w_uq_t(w_uq):
    w = w_uq.T.reshape(MLA_HEADS, QK_HEAD_DIM, Q_LORA_RANK)
    w = jnp.pad(w, ((0, 0), (0, QK_PAD - QK_HEAD_DIM), (0, 0)))
    return w.reshape(MLA_HEADS * QK_PAD, Q_LORA_RANK).astype(BF16)


def _pack_w_ukv(w_ukv):
    w = w_ukv.reshape(KV_LORA_RANK, MLA_HEADS, QK_NOPE_DIM + V_HEAD_DIM)
    w_uk = w[:, :, :QK_NOPE_DIM].reshape(KV_LORA_RANK, MLA_HEADS * QK_NOPE_DIM)
    w_uv_t = jnp.transpose(w[:, :, QK_NOPE_DIM:], (1, 2, 0))
    w_uv_t = jnp.pad(w_uv_t, ((0, 0), (0, V_ROWS - V_HEAD_DIM), (0, 0)))
    return w_uk.astype(BF16), w_uv_t.reshape(MLA_HEADS * V_ROWS, KV_LORA_RANK).astype(BF16)


def _layer(h, p_i, pos_row, invf, attn_pre_norm, attn_post_norm, w_in, q_norm, kv_norm, w_uq, w_ukv,
           conv_w, conv_b, gate_bias_i, gate_bias_f, mlstm_norm, w_out, ffn_pre_norm, ffn_post_norm,
           w_gate, w_up, w_down, w_ple_proj, w_ple_gate, batch, seq):
    row = lambda a: a.reshape(1, -1)
    w_in_p, w_tail, w_mv_t = _pack_w_in(w_in)
    w_uk, w_uv_t = _pack_w_ukv(w_ukv)
    gap = jnp.zeros((SUBLANES - MLSTM_HEADS,), F32)
    gbias = jnp.concatenate([gate_bias_i, gap, gate_bias_f, gap]).astype(F32).reshape(2 * SUBLANES, 1)
    vones = jnp.zeros((MLA_HEADS, V_ROWS, 1), F32).at[:, V_HEAD_DIM, 0].set(1.0)
    qt, k, vt, mqt, mk, mvt, mo, grow, keycol = _in_proj(
        h, pos_row, invf, row(attn_pre_norm), w_in_p, w_tail, row(q_norm), row(kv_norm),
        _pack_w_uq_t(w_uq), w_uk, w_uv_t, w_mv_t, vones.reshape(MLA_HEADS * V_ROWS, 1), conv_w,
        row(conv_b), gbias, seq)
    mla = _attention(qt, k, vt, batch, seq)
    mlstm = _mlstm(mqt, mk, mvt, mo, grow, keycol, row(mlstm_norm), batch, seq)
    return _out_ffn(h, mla, mlstm, p_i, w_out.astype(BF16), row(attn_post_norm), row(ffn_pre_norm),
                    row(ffn_post_norm), w_gate.astype(BF16), w_up.astype(BF16),
                    w_down.astype(BF16), w_ple_gate.astype(BF16), w_ple_proj.astype(BF16))


@jax.jit
def kernel(x, p, positions, attn_pre_norm, attn_post_norm, w_in, q_norm, kv_norm, w_uq, w_ukv,
           conv_w, conv_b, gate_bias_i, gate_bias_f, mlstm_norm, w_out, ffn_pre_norm, ffn_post_norm,
           w_gate, w_up, w_down, w_ple_proj, w_ple_gate):
    batch, seq, _ = x.shape
    t = batch * seq
    depth = p.shape[0]
    inv_freq = ROPE_THETA ** (-jnp.arange(0, QK_ROPE_DIM, 2, dtype=F32) / QK_ROPE_DIM)
    invf = inv_freq.reshape(ROPE_HALF, 1)
    pos_row = positions.astype(F32).reshape(1, t)
    h = x.reshape(t, D_MODEL)
    for i in range(depth):
        h = _layer(h, p[i].reshape(t, PLE_DIM), pos_row, invf, attn_pre_norm[i], attn_post_norm[i],
                   w_in[i], q_norm[i], kv_norm[i], w_uq[i], w_ukv[i], conv_w[i], conv_b[i],
                   gate_bias_i[i], gate_bias_f[i], mlstm_norm[i], w_out[i], ffn_pre_norm[i],
                   ffn_post_norm[i], w_gate[i], w_up[i], w_down[i], w_ple_proj[i], w_ple_gate[i],
                   batch, seq)
    return h.reshape(batch, seq, D_MODEL)
```

```python
import functools
import math

import jax
import jax.numpy as jnp
from jax import lax
from jax.experimental import pallas as pl
from jax.experimental.pallas import tpu as pltpu

F32 = jnp.float32
BF16 = jnp.bfloat16

D_MODEL = 1024
PLE_DIM = 256
MLA_HEADS = 4
QK_NOPE_DIM = 128
QK_ROPE_DIM = 64
QK_HEAD_DIM = QK_NOPE_DIM + QK_ROPE_DIM
V_HEAD_DIM = 128
Q_LORA_RANK = 256
KV_LORA_RANK = 128
ROPE_THETA = 10000.0
MLA_WIDTH = MLA_HEADS * V_HEAD_DIM
MLSTM_HEADS = 4
MLSTM_HEAD_DIM = 128
MLSTM_WIDTH = MLSTM_HEADS * MLSTM_HEAD_DIM
CONV_WIDTH = 4
D_FF = 2816
EPS = 1e-6

LANES = 128
SUBLANES = 8
BF16_ROWS = 16
QK_PAD = 2 * LANES
V_ROWS = V_HEAD_DIM + BF16_ROWS
ROPE_HALF = QK_ROPE_DIM // 2
VMEM_LIMIT = 56 * 1024 * 1024

C_Q = 0
C_KV = C_Q + Q_LORA_RANK
C_MQK = C_KV + KV_LORA_RANK
C_MO = C_MQK + 2 * MLSTM_WIDTH
D_IN_PACKED = C_MO + MLSTM_WIDTH
GATE_I = QK_ROPE_DIM
GATE_F = GATE_I + SUBLANES
ROW_KEY, ROW_KMAX, ROW_B, GATE_ROWS = 0, SUBLANES, 2 * SUBLANES, 3 * SUBLANES

TILE_T = 512
TM_IN = 512
SUB_IN = 256
TQ = TILE_T
HEADS_PER_STEP = 2
CHUNK = 256
TM_OUT = 512
SUB_OUT = 256
FF_CHUNKS = (1024, 1024, 768)


def _rms(x, g):
    return x * lax.rsqrt(jnp.mean(x * x, axis=-1, keepdims=True) + EPS) * g


def _dot(a, b):
    return jnp.dot(a, b, preferred_element_type=F32)


def _dot_nt(a, b):
    return lax.dot_general(a, b, (((1,), (1,)), ((), ())), preferred_element_type=F32)


def _const_spec(shape):
    return pl.BlockSpec(shape, lambda *_: (0,) * len(shape), pipeline_mode=pl.Buffered(1))


def _in_proj_body(x_ref, pos_ref, invf_ref, g_pre_ref, w_in_ref, w_tail_ref, qn_ref, kvn_ref,
                  w_uq_ref, w_uk_ref, w_uv_ref, w_mv_ref, vones_ref, conv_w_ref, conv_b_ref, gbias_ref,
                  qt_out, k_out, vt_out, mqt_out, mk_out, mvt_out, mo_out, grow_out, keycol_out,
                  ext_ref, *, tm, sub, tiles_per_seq, chunk):
    assert tm % sub == 0 and sub % chunk == 0
    i = pl.program_id(0)
    halo = SUBLANES

    @pl.when(lax.rem(i, tiles_per_seq) == 0)
    def _():
        ext_ref[0:halo, :] = jnp.zeros((halo, 2 * MLSTM_WIDTH), F32)

    for r0 in range(0, tm, sub):
        _in_proj_rows(r0, sub, halo, chunk, x_ref, pos_ref, invf_ref, g_pre_ref, w_in_ref, w_tail_ref,
                      qn_ref, kvn_ref, w_uq_ref, w_uk_ref, w_uv_ref, w_mv_ref, vones_ref, conv_w_ref,
                      conv_b_ref, gbias_ref, qt_out, k_out, vt_out, mqt_out, mk_out, mvt_out, mo_out,
                      grow_out, keycol_out, ext_ref)
    ext_ref[0:halo, :] = ext_ref[tm:tm + halo, :]


def _in_proj_rows(r0, ts, halo, chunk, x_ref, pos_ref, invf_ref, g_pre_ref, w_in_ref, w_tail_ref,
                  qn_ref, kvn_ref, w_uq_ref, w_uk_ref, w_uv_ref, w_mv_ref, vones_ref, conv_w_ref,
                  conv_b_ref, gbias_ref, qt_out, k_out, vt_out, mqt_out, mk_out, mvt_out, mo_out,
                  grow_out, keycol_out, ext_ref):
    rows = slice(r0, r0 + ts)
    tile, lanes = r0 // TILE_T, slice(r0 % TILE_T, r0 % TILE_T + ts)
    u = _rms(x_ref[rows, :], g_pre_ref[...]).astype(BF16)

    cq_raw = _dot(u, w_in_ref[:, C_Q:C_Q + Q_LORA_RANK])
    ckv_raw = _dot(u, w_in_ref[:, C_KV:C_KV + KV_LORA_RANK])
    tail_t = _dot_nt(w_tail_ref[...], u)
    ext_ref[halo + r0:halo + r0 + ts, :] = _dot(u, w_in_ref[:, C_MQK:C_MQK + 2 * MLSTM_WIDTH])
    mvt_out[tile, :, lanes] = (_dot_nt(w_mv_ref[...], u) + vones_ref[...]).astype(BF16)
    mo_out[rows, :] = jax.nn.sigmoid(_dot(u, w_in_ref[:, C_MO:C_MO + MLSTM_WIDTH])).astype(BF16)
    cq = _rms(cq_raw, qn_ref[...]).astype(BF16)
    ckv = _rms(ckv_raw, kvn_ref[...]).astype(BF16)
    qt = _dot_nt(w_uq_ref[...], cq)
    kn = _dot(ckv, w_uk_ref[...])
    vt_out[tile, :, lanes] = (_dot_nt(w_uv_ref[...], ckv) + vones_ref[...]).astype(BF16)

    ang = invf_ref[...] * pos_ref[:, rows]
    cos = jnp.cos(ang)
    sin = jnp.sin(ang)

    def rope_t(t1, t2):
        return t1 * cos - t2 * sin, t2 * cos + t1 * sin

    scale = QK_HEAD_DIM ** -0.5 * math.log2(math.e)
    for h in range(MLA_HEADS):
        o = h * QK_PAD
        pe = o + QK_NOPE_DIM
        r1, r2 = rope_t(qt[pe:pe + ROPE_HALF], qt[pe + ROPE_HALF:pe + QK_ROPE_DIM])
        qt_out[tile, o:pe, lanes] = (qt[o:pe] * scale).astype(BF16)
        qt_out[tile, pe:pe + ROPE_HALF, lanes] = (r1 * scale).astype(BF16)
        qt_out[tile, pe + ROPE_HALF:pe + QK_ROPE_DIM, lanes] = (r2 * scale).astype(BF16)
        qt_out[tile, pe + QK_ROPE_DIM:o + QK_PAD, lanes] = jnp.zeros((QK_PAD - QK_HEAD_DIM, ts), BF16)

    r1, r2 = rope_t(tail_t[0:ROPE_HALF], tail_t[ROPE_HALF:QK_ROPE_DIM])
    kpe_t = jnp.concatenate([r1, r2, jnp.zeros((LANES - QK_ROPE_DIM, ts), F32)], axis=0)
    kpe = kpe_t.T.astype(BF16)
    for h in range(MLA_HEADS):
        o = h * QK_PAD
        k_out[rows, o:o + LANES] = kn[:, h * LANES:(h + 1) * LANES].astype(BF16)
        k_out[rows, o + LANES:o + QK_PAD] = kpe

    gates = tail_t[GATE_I:GATE_I + 2 * SUBLANES] + gbias_ref[...]
    pos_in_chunk = lax.rem(r0 + lax.broadcasted_iota(jnp.int32, (SUBLANES, ts), 1), chunk)

    def chunk_scan(v, op, identity):
        shift = 1
        while shift < chunk:
            v = op(v, jnp.where(pos_in_chunk >= shift, pltpu.roll(v, shift, 1), identity))
            shift *= 2
        return v

    b8 = chunk_scan(jax.nn.log_sigmoid(gates[SUBLANES:]), jnp.add, 0.0)
    key8 = gates[:SUBLANES] - b8
    grow_out[:, rows] = jnp.concatenate([key8, chunk_scan(key8, jnp.maximum, -jnp.inf), b8], axis=0)
    keycol_out[rows, :] = jnp.concatenate([key8, jnp.zeros((LANES - SUBLANES, ts), F32)], axis=0).T

    acc = conv_b_ref[...]
    for j in range(CONV_WIDTH):
        start = halo - (CONV_WIDTH - 1) + j + r0
        acc = acc + conv_w_ref[j:j + 1, :] * ext_ref[start:start + ts, :]
    qk = acc * jax.nn.sigmoid(acc)
    mqt_out[tile, :, lanes] = qk[:, :MLSTM_WIDTH].T.astype(BF16)
    mk_out[rows, :] = (qk[:, MLSTM_WIDTH:] * (MLSTM_HEAD_DIM ** -0.5)).astype(BF16)


def _in_proj(x2, pos_row, invf, g_pre, w_in_p, w_tail, qn, kvn, w_uq_t, w_uk, w_uv_t, w_mv_t, vones,
             conv_w, conv_b, gbias, seq):
    t = x2.shape[0]
    tm = TM_IN
    nt = t // TILE_T
    row = lambda width: pl.BlockSpec((tm, width), lambda i: (i, 0))
    tile_t = lambda rows: pl.BlockSpec((tm // TILE_T, rows, TILE_T), lambda i: (i, 0, 0))
    bf = lambda *shape: jax.ShapeDtypeStruct(shape, BF16)
    out_shape = [
        bf(nt, MLA_HEADS * QK_PAD, TILE_T), bf(t, MLA_HEADS * QK_PAD), bf(nt, MLA_HEADS * V_ROWS, TILE_T),
        bf(nt, MLSTM_WIDTH, TILE_T), bf(t, MLSTM_WIDTH), bf(nt, MLSTM_HEADS * V_ROWS, TILE_T),
        bf(t, MLSTM_WIDTH), jax.ShapeDtypeStruct((GATE_ROWS, t), F32),
        jax.ShapeDtypeStruct((t, LANES), F32),
    ]
    out_specs = [
        tile_t(MLA_HEADS * QK_PAD), row(MLA_HEADS * QK_PAD), tile_t(MLA_HEADS * V_ROWS),
        tile_t(MLSTM_WIDTH), row(MLSTM_WIDTH), tile_t(MLSTM_HEADS * V_ROWS), row(MLSTM_WIDTH),
        pl.BlockSpec((GATE_ROWS, tm), lambda i: (0, i)), row(LANES),
    ]
    consts = (invf, g_pre, w_in_p, w_tail, qn, kvn, w_uq_t, w_uk, w_uv_t, w_mv_t, vones, conv_w,
              conv_b, gbias)
    return pl.pallas_call(
        functools.partial(_in_proj_body, tm=tm, sub=SUB_IN, tiles_per_seq=seq // tm, chunk=CHUNK),
        out_shape=out_shape,
        grid=(t // tm,),
        in_specs=[row(D_MODEL), pl.BlockSpec((1, tm), lambda i: (0, i))]
        + [_const_spec(c.shape) for c in consts],
        out_specs=out_specs,
        scratch_shapes=[pltpu.VMEM((tm + 2 * SUBLANES, 2 * MLSTM_WIDTH), F32)],
        compiler_params=pltpu.CompilerParams(dimension_semantics=("arbitrary",),
                                             vmem_limit_bytes=VMEM_LIMIT),
        name="in_proj",
    )(x2, pos_row, *consts)


def _attn_body(qt_ref, k_ref, vt_ref, o_ref, m_ref, acc_ref, sa_ref, sb_ref, cma_ref, cmb_ref,
               *, tq, heads):
    qi = pl.program_id(2)
    m_ref[...] = jnp.full(m_ref.shape, -jnp.inf, F32)
    acc_ref[...] = jnp.zeros(acc_ref.shape, F32)

    def scores(ki, s_ref, cm_ref):
        start = pl.multiple_of(ki * tq, tq)
        for h in range(heads):
            k = k_ref[pl.ds(start, tq), h * QK_PAD:(h + 1) * QK_PAD]
            s = _dot(k, qt_ref[0, h * QK_PAD:(h + 1) * QK_PAD, :])
            s_ref[h] = s
            cm_ref[h] = jnp.broadcast_to(jnp.max(s, axis=0, keepdims=True), cm_ref.shape[1:])

    def softmax_pv(ki, s_ref, cm_ref, diagonal):
        for h in range(heads):
            s = s_ref[h]
            if diagonal:
                r = lax.broadcasted_iota(jnp.int32, s.shape, 0)
                c = lax.broadcasted_iota(jnp.int32, s.shape, 1)
                s = jnp.where(r <= c, s, -jnp.inf)
                cmax = jnp.max(s, axis=0, keepdims=True)
            else:
                cmax = cm_ref[h, 0:1, :]
            m_old = m_ref[h, 0:1, :]
            m_new = jnp.maximum(m_old, cmax)
            alpha = jnp.exp2(m_old - m_new)
            p = jnp.exp2(s - m_new).astype(BF16)
            pv = _dot(vt_ref[ki, h * V_ROWS:(h + 1) * V_ROWS, :], p)
            acc_ref[h] = alpha * acc_ref[h] + pv
            m_ref[h] = jnp.broadcast_to(m_new, m_ref.shape[1:])

    scores(0, sa_ref, cma_ref)

    def pair(j, carry):
        ki = 2 * j
        scores(ki + 1, sb_ref, cmb_ref)
        softmax_pv(ki, sa_ref, cma_ref, False)
        scores(ki + 2, sa_ref, cma_ref)
        softmax_pv(ki + 1, sb_ref, cmb_ref, False)
        return carry

    lax.fori_loop(0, qi // 2, pair, 0)

    @pl.when(lax.rem(qi, 2) == 1)
    def _():
        scores(qi, sb_ref, cmb_ref)
        softmax_pv(qi - 1, sa_ref, cma_ref, False)
        softmax_pv(qi, sb_ref, cmb_ref, True)

    @pl.when(lax.rem(qi, 2) == 0)
    def _():
        softmax_pv(qi, sa_ref, cma_ref, True)

    for h in range(heads):
        acc = acc_ref[h]
        out_t = acc[0:V_HEAD_DIM] / acc[V_HEAD_DIM:V_HEAD_DIM + 1]
        o_ref[:, h * V_HEAD_DIM:(h + 1) * V_HEAD_DIM] = out_t.T.astype(o_ref.dtype)


def _attention(qt, k, vt, batch, seq):
    t = k.shape[0]
    tq = TQ
    nq = seq // tq
    heads = HEADS_PER_STEP
    return pl.pallas_call(
        functools.partial(_attn_body, tq=tq, heads=heads),
        out_shape=jax.ShapeDtypeStruct((t, MLA_WIDTH), BF16),
        grid=(batch, MLA_HEADS // heads, nq),
        in_specs=[
            pl.BlockSpec((1, heads * QK_PAD, tq), lambda b, g, i: (b * nq + i, g, 0)),
            pl.BlockSpec((seq, heads * QK_PAD), lambda b, g, i: (b, g)),
            pl.BlockSpec((nq, heads * V_ROWS, tq), lambda b, g, i: (b, g, 0)),
        ],
        out_specs=pl.BlockSpec((tq, heads * V_HEAD_DIM), lambda b, g, i: (b * nq + i, g)),
        scratch_shapes=[pltpu.VMEM((heads, SUBLANES, tq), F32),
                        pltpu.VMEM((heads, V_ROWS, tq), F32),
                        pltpu.VMEM((heads, tq, tq), F32),
                        pltpu.VMEM((heads, tq, tq), F32),
                        pltpu.VMEM((heads, SUBLANES, tq), F32),
                        pltpu.VMEM((heads, SUBLANES, tq), F32)],
        compiler_params=pltpu.CompilerParams(
            dimension_semantics=("arbitrary", "arbitrary", "arbitrary"),
            vmem_limit_bytes=VMEM_LIMIT),
        name="mla_attention",
    )(qt, k, vt)


def _mlstm_body(qt_ref, k_ref, vt_ref, og_ref, grow_ref, keycol_ref, norm_ref, out_ref, ct_ref,
                m_ref, *, chunk, chunks_per_step):
    @pl.when(pl.program_id(1) == 0)
    def _():
        ct_ref[...] = jnp.zeros(ct_ref.shape, F32)
        m_ref[...] = jnp.zeros(m_ref.shape, F32)

    heads = range(MLSTM_HEADS)
    hs = [slice(h * MLSTM_HEAD_DIM, (h + 1) * MLSTM_HEAD_DIM) for h in heads]
    vs = [slice(h * V_ROWS, (h + 1) * V_ROWS) for h in heads]
    tok = [slice(c * chunk, (c + 1) * chunk) for c in range(chunks_per_step)]
    r = lax.broadcasted_iota(jnp.int32, (chunk, chunk), 0)
    c = lax.broadcasted_iota(jnp.int32, (chunk, chunk), 1)
    upper = r <= c

    s_raw = [[_dot(k_ref[t, hs[h]], qt_ref[0, hs[h], t]) for h in heads] for t in tok]
    ct = [ct_ref[h] for h in heads]
    m_prev = [m_ref[h, 0:1, 0:1] for h in heads]
    for ci, t in enumerate(tok):
        qt = [qt_ref[0, hs[h], t] for h in heads]
        vt = [vt_ref[0, vs[h], t] for h in heads]
        cq = [_dot(ct[h].astype(BF16), qt[h]) for h in heads]
        grow = grow_ref[:, t]
        mm = [jnp.maximum(m_prev[h], grow[ROW_KMAX + h:ROW_KMAX + h + 1]) for h in heads]
        b_row = [grow[ROW_B + h:ROW_B + h + 1] for h in heads]
        inter = [jnp.exp(m_prev[h] - mm[h]) for h in heads]
        for h in heads:
            mm_last = mm[h][:, chunk - 1:chunk]
            w_row = jnp.exp(grow[ROW_KEY + h:ROW_KEY + h + 1] - mm_last)
            vtw = (vt[h].astype(F32) * w_row).astype(BF16)
            ct[h] = jnp.exp(m_prev[h] - mm_last) * ct[h] + _dot(vtw, k_ref[t, hs[h]])
            m_prev[h] = b_row[h][:, chunk - 1:chunk] + mm_last
        keycol = keycol_ref[t, :]
        for h in heads:
            p = jnp.exp(jnp.where(upper, keycol[:, h:h + 1] - mm[h], -jnp.inf))
            st = (s_raw[ci][h] * p).astype(BF16)
            tot = _dot(vt[h], st) + inter[h] * cq[h]
            den = tot[MLSTM_HEAD_DIM:MLSTM_HEAD_DIM + 1]
            ht = tot[0:MLSTM_HEAD_DIM] / jnp.maximum(jnp.abs(den), jnp.exp(-(b_row[h] + mm[h])))
            hn_t = ht * lax.rsqrt(jnp.mean(ht * ht, axis=0, keepdims=True) + EPS)
            out_ref[t, hs[h]] = (hn_t.T * norm_ref[0:1, hs[h]]
                                 * og_ref[t, hs[h]].astype(F32)).astype(out_ref.dtype)
    for h in heads:
        ct_ref[h] = ct[h]
        m_ref[h] = jnp.broadcast_to(m_prev[h], m_ref.shape[1:])


def _mlstm(mqt, mk, mvt, mo, grow, keycol, norm, batch, seq):
    t = mk.shape[0]
    tile = TILE_T
    ns = seq // tile
    row = lambda width: pl.BlockSpec((tile, width), lambda b, j: (b * ns + j, 0))
    tile_t = lambda rows: pl.BlockSpec((1, rows, tile), lambda b, j: (b * ns + j, 0, 0))
    return pl.pallas_call(
        functools.partial(_mlstm_body, chunk=CHUNK, chunks_per_step=tile // CHUNK),
        out_shape=jax.ShapeDtypeStruct((t, MLSTM_WIDTH), BF16),
        grid=(batch, ns),
        in_specs=[tile_t(MLSTM_WIDTH), row(MLSTM_WIDTH), tile_t(MLSTM_HEADS * V_ROWS),
                  row(MLSTM_WIDTH), pl.BlockSpec((GATE_ROWS, tile), lambda b, j: (0, b * ns + j)),
                  row(LANES), _const_spec(norm.shape)],
        out_specs=row(MLSTM_WIDTH),
        scratch_shapes=[
            pltpu.VMEM((MLSTM_HEADS, V_ROWS, MLSTM_HEAD_DIM), F32),
            pltpu.VMEM((MLSTM_HEADS, SUBLANES, LANES), F32),
        ],
        compiler_params=pltpu.CompilerParams(dimension_semantics=("arbitrary", "arbitrary"),
                                             vmem_limit_bytes=VMEM_LIMIT),
        name="mlstm",
    )(mqt, mk, mvt, mo, grow, keycol, norm)


def _out_ffn_body(x_ref, mla_ref, mlstm_ref, p_ref, w_out_ref, g_post_ref, g_fpre_ref, g_fpost_ref,
                  w_gate_ref, w_up_ref, w_down_ref, w_pg_ref, w_pp_ref, o_ref):
    tm = x_ref.shape[0]
    subs = [slice(r0, r0 + SUB_OUT) for r0 in range(0, tm, SUB_OUT)]
    mix = [_dot(mla_ref[s, :], w_out_ref[0:MLA_WIDTH, :])
           + _dot(mlstm_ref[s, :], w_out_ref[MLA_WIDTH:MLA_WIDTH + MLSTM_WIDTH, :]) for s in subs]
    h1 = [x_ref[s, :] + _rms(m, g_post_ref[...]) for s, m in zip(subs, mix)]
    f = [_rms(h, g_fpre_ref[...]).astype(BF16) for h in h1]
    ffn = [None] * len(subs)
    off = 0
    for width in FF_CHUNKS:
        cols = slice(off, off + width)
        gate = [_dot(v, w_gate_ref[:, cols]) for v in f]
        up = [_dot(v, w_up_ref[:, cols]) for v in f]
        act = [(g * jax.nn.sigmoid(g) * u).astype(BF16) for g, u in zip(gate, up)]
        part = [_dot(a, w_down_ref[cols, :]) for a in act]
        ffn = [p if acc is None else acc + p for acc, p in zip(ffn, part)]
        off += width
    h2 = [h + _rms(v, g_fpost_ref[...]) for h, v in zip(h1, ffn)]
    pgate = [jax.nn.sigmoid(_dot(h.astype(BF16), w_pg_ref[...])) for h in h2]
    for s, h, g in zip(subs, h2, pgate):
        o_ref[s, :] = h + g * _dot(p_ref[s, :].astype(BF16), w_pp_ref[...])


def _out_ffn(x2, mla, mlstm, p2, w_out, g_post, g_fpre, g_fpost, w_gate, w_up, w_down, w_pg, w_pp):
    t = x2.shape[0]
    tm = TM_OUT
    row = lambda width: pl.BlockSpec((tm, width), lambda i: (i, 0))
    consts = (w_out, g_post, g_fpre, g_fpost, w_gate, w_up, w_down, w_pg, w_pp)
    return pl.pallas_call(
        _out_ffn_body,
        out_shape=jax.ShapeDtypeStruct((t, D_MODEL), F32),
        grid=(t // tm,),
        in_specs=[row(D_MODEL), row(MLA_WIDTH), row(MLSTM_WIDTH), row(PLE_DIM)]
        + [_const_spec(c.shape) for c in consts],
        out_specs=row(D_MODEL),
        compiler_params=pltpu.CompilerParams(dimension_semantics=("arbitrary",),
                                             vmem_limit_bytes=VMEM_LIMIT),
        name="out_ffn",
    )(x2, mla, mlstm, p2, *consts)


def _pack_w_in(w_in):
    o_krope = Q_LORA_RANK + KV_LORA_RANK
    o_mq = o_krope + QK_ROPE_DIM
    o_mv = o_mq + 2 * MLSTM_WIDTH
    o_mo = o_mv + MLSTM_WIDTH
    o_gates = o_mo + MLSTM_WIDTH
    main = jnp.concatenate([w_in[:, :o_krope], w_in[:, o_mq:o_mv], w_in[:, o_mo:o_gates]], axis=1)
    gap = jnp.zeros((D_MODEL, SUBLANES - MLSTM_HEADS), w_in.dtype)
    tail = jnp.concatenate([w_in[:, o_krope:o_mq], w_in[:, o_gates:o_gates + MLSTM_HEADS], gap,
                            w_in[:, o_gates + MLSTM_HEADS:], gap], axis=1).T
    tail = jnp.pad(tail, ((0, LANES - tail.shape[0]), (0, 0)))
    mv_t = w_in[:, o_mv:o_mo].T.reshape(MLSTM_HEADS, MLSTM_HEAD_DIM, D_MODEL)
    mv_t = jnp.pad(mv_t, ((0, 0), (0, V_ROWS - MLSTM_HEAD_DIM), (0, 0)))
    return main.astype(BF16), tail.astype(BF16), mv_t.reshape(MLSTM_HEADS * V_ROWS, D_MODEL).astype(BF16)


def _pack_w_uq_t(w_uq):
    w = w_uq.T.reshape(MLA_HEADS, QK_HEAD_DIM, Q_LORA_RANK)
    w = jnp.pad(w, ((0, 0), (0, QK_PAD - QK_HEAD_DIM), (0, 0)))
    return w.reshape(MLA_HEADS * QK_PAD, Q_LORA_RANK).astype(BF16)


def _pack_w_ukv(w_ukv):
    w = w_ukv.reshape(KV_LORA_RANK, MLA_HEADS, QK_NOPE_DIM + V_HEAD_DIM)
    w_uk = w[:, :, :QK_NOPE_DIM].reshape(KV_LORA_RANK, MLA_HEADS * QK_NOPE_DIM)
    w_uv_t = jnp.transpose(w[:, :, QK_NOPE_DIM:], (1, 2, 0))
    w_uv_t = jnp.pad(w_uv_t, ((0, 0), (0, V_ROWS - V_HEAD_DIM), (0, 0)))
    return w_uk.astype(BF16), w_uv_t.reshape(MLA_HEADS * V_ROWS, KV_LORA_RANK).astype(BF16)


def _layer(h, p_i, pos_row, invf, attn_pre_norm, attn_post_norm, w_in, q_norm, kv_norm, w_uq, w_ukv,
           conv_w, conv_b, gate_bias_i, gate_bias_f, mlstm_norm, w_out, ffn_pre_norm, ffn_post_norm,
           w_gate, w_up, w_down, w_ple_proj, w_ple_gate, batch, seq):
    row = lambda a: a.reshape(1, -1)
    w_in_p, w_tail, w_mv_t = _pack_w_in(w_in)
    w_uk, w_uv_t = _pack_w_ukv(w_ukv)
    gap = jnp.zeros((SUBLANES - MLSTM_HEADS,), F32)
    gbias = jnp.concatenate([gate_bias_i, gap, gate_bias_f, gap]).astype(F32).reshape(2 * SUBLANES, 1)
    vones = jnp.zeros((MLA_HEADS, V_ROWS, 1), F32).at[:, V_HEAD_DIM, 0].set(1.0)
    qt, k, vt, mqt, mk, mvt, mo, grow, keycol = _in_proj(
        h, pos_row, invf, row(attn_pre_norm), w_in_p, w_tail, row(q_norm), row(kv_norm),
        _pack_w_uq_t(w_uq), w_uk, w_uv_t, w_mv_t, vones.reshape(MLA_HEADS * V_ROWS, 1), conv_w,
        row(conv_b), gbias, seq)
    mla = _attention(qt, k, vt, batch, seq)
    mlstm = _mlstm(mqt, mk, mvt, mo, grow, keycol, row(mlstm_norm), batch, seq)
    return _out_ffn(h, mla, mlstm, p_i, w_out.astype(BF16), row(attn_post_norm), row(ffn_pre_norm),
                    row(ffn_post_norm), w_gate.astype(BF16), w_up.astype(BF16),
                    w_down.astype(BF16), w_ple_gate.astype(BF16), w_ple_proj.astype(BF16))


@jax.jit
def kernel(x, p, positions, attn_pre_norm, attn_post_norm, w_in, q_norm, kv_norm, w_uq, w_ukv,
           conv_w, conv_b, gate_bias_i, gate_bias_f, mlstm_norm, w_out, ffn_pre_norm, ffn_post_norm,
           w_gate, w_up, w_down, w_ple_proj, w_ple_gate):
    batch, seq, _ = x.shape
    t = batch * seq
    depth = p.shape[0]
    inv_freq = ROPE_THETA ** (-jnp.arange(0, QK_ROPE_DIM, 2, dtype=F32) / QK_ROPE_DIM)
    invf = inv_freq.reshape(ROPE_HALF, 1)
    pos_row = positions.astype(F32).reshape(1, t)
    h = x.reshape(t, D_MODEL)
    for i in range(depth):
        h = _layer(h, p[i].reshape(t, PLE_DIM), pos_row, invf, attn_pre_norm[i], attn_post_norm[i],
                   w_in[i], q_norm[i], kv_norm[i], w_uq[i], w_ukv[i], conv_w[i], conv_b[i],
                   gate_bias_i[i], gate_bias_f[i], mlstm_norm[i], w_out[i], ffn_pre_norm[i],
                   ffn_post_norm[i], w_gate[i], w_up[i], w_down[i], w_ple_proj[i], w_ple_gate[i],
                   batch, seq)
    return h.reshape(batch, seq, D_MODEL)
```

```python
import functools
import math

import jax
import jax.numpy as jnp
from jax import lax
from jax.experimental import pallas as pl
from jax.experimental.pallas import tpu as pltpu

F32 = jnp.float32
BF16 = jnp.bfloat16

D_MODEL = 1024
PLE_DIM = 256
MLA_HEADS = 4
QK_NOPE_DIM = 128
QK_ROPE_DIM = 64
QK_HEAD_DIM = QK_NOPE_DIM + QK_ROPE_DIM
V_HEAD_DIM = 128
Q_LORA_RANK = 256
KV_LORA_RANK = 128
ROPE_THETA = 10000.0
MLA_WIDTH = MLA_HEADS * V_HEAD_DIM
MLSTM_HEADS = 4
MLSTM_HEAD_DIM = 128
MLSTM_WIDTH = MLSTM_HEADS * MLSTM_HEAD_DIM
CONV_WIDTH = 4
D_FF = 2816
EPS = 1e-6

LANES = 128
SUBLANES = 8
BF16_ROWS = 16
QK_PAD = 2 * LANES
V_ROWS = V_HEAD_DIM + BF16_ROWS
ROPE_HALF = QK_ROPE_DIM // 2
VMEM_LIMIT = 56 * 1024 * 1024

C_Q = 0
C_KV = C_Q + Q_LORA_RANK
C_MQK = C_KV + KV_LORA_RANK
C_MO = C_MQK + 2 * MLSTM_WIDTH
D_IN_PACKED = C_MO + MLSTM_WIDTH
GATE_I = QK_ROPE_DIM
GATE_F = GATE_I + SUBLANES
ROW_KEY, ROW_KMAX, ROW_B, GATE_ROWS = 0, SUBLANES, 2 * SUBLANES, 3 * SUBLANES

TILE_T = 512
TM_IN = 512
SUB_IN = 256
TQ = TILE_T
HEADS_PER_STEP = 2
MAX_STALE_EXCESS = 64.0
CHUNK = 256
TM_OUT = 512
SUB_OUT = 256
FF_CHUNKS = (1024, 1024, 768)


def _rms(x, g):
    return x * lax.rsqrt(jnp.mean(x * x, axis=-1, keepdims=True) + EPS) * g


def _dot(a, b):
    return jnp.dot(a, b, preferred_element_type=F32)


def _dot_nt(a, b):
    return lax.dot_general(a, b, (((1,), (1,)), ((), ())), preferred_element_type=F32)


def _const_spec(shape):
    return pl.BlockSpec(shape, lambda *_: (0,) * len(shape), pipeline_mode=pl.Buffered(1))


def _in_proj_body(x_ref, pos_ref, invf_ref, g_pre_ref, w_in_ref, w_tail_ref, qn_ref, kvn_ref,
                  w_uq_ref, w_uk_ref, w_uv_ref, w_mv_ref, vones_ref, conv_w_ref, conv_b_ref, gbias_ref,
                  qt_out, k_out, vt_out, mqt_out, mk_out, mvt_out, mo_out, grow_out, keycol_out,
                  ext_ref, *, tm, sub, tiles_per_seq, chunk):
    assert tm % sub == 0 and sub % chunk == 0
    i = pl.program_id(0)
    halo = SUBLANES

    @pl.when(lax.rem(i, tiles_per_seq) == 0)
    def _():
        ext_ref[0:halo, :] = jnp.zeros((halo, 2 * MLSTM_WIDTH), F32)

    for r0 in range(0, tm, sub):
        _in_proj_rows(r0, sub, halo, chunk, x_ref, pos_ref, invf_ref, g_pre_ref, w_in_ref, w_tail_ref,
                      qn_ref, kvn_ref, w_uq_ref, w_uk_ref, w_uv_ref, w_mv_ref, vones_ref, conv_w_ref,
                      conv_b_ref, gbias_ref, qt_out, k_out, vt_out, mqt_out, mk_out, mvt_out, mo_out,
                      grow_out, keycol_out, ext_ref)
    ext_ref[0:halo, :] = ext_ref[tm:tm + halo, :]


def _in_proj_rows(r0, ts, halo, chunk, x_ref, pos_ref, invf_ref, g_pre_ref, w_in_ref, w_tail_ref,
                  qn_ref, kvn_ref, w_uq_ref, w_uk_ref, w_uv_ref, w_mv_ref, vones_ref, conv_w_ref,
                  conv_b_ref, gbias_ref, qt_out, k_out, vt_out, mqt_out, mk_out, mvt_out, mo_out,
                  grow_out, keycol_out, ext_ref):
    rows = slice(r0, r0 + ts)
    tile, lanes = r0 // TILE_T, slice(r0 % TILE_T, r0 % TILE_T + ts)
    u = _rms(x_ref[rows, :], g_pre_ref[...]).astype(BF16)

    cq_raw = _dot(u, w_in_ref[:, C_Q:C_Q + Q_LORA_RANK])
    ckv_raw = _dot(u, w_in_ref[:, C_KV:C_KV + KV_LORA_RANK])
    tail_t = _dot_nt(w_tail_ref[...], u)
    ext_ref[halo + r0:halo + r0 + ts, :] = _dot(u, w_in_ref[:, C_MQK:C_MQK + 2 * MLSTM_WIDTH])
    mvt_out[tile, :, lanes] = (_dot_nt(w_mv_ref[...], u) + vones_ref[...]).astype(BF16)
    mo_out[rows, :] = jax.nn.sigmoid(_dot(u, w_in_ref[:, C_MO:C_MO + MLSTM_WIDTH])).astype(BF16)
    cq = _rms(cq_raw, qn_ref[...]).astype(BF16)
    ckv = _rms(ckv_raw, kvn_ref[...]).astype(BF16)
    qt = _dot_nt(w_uq_ref[...], cq)
    kn = _dot(ckv, w_uk_ref[...])
    vt_out[tile, :, lanes] = (_dot_nt(w_uv_ref[...], ckv) + vones_ref[...]).astype(BF16)

    ang = invf_ref[...] * pos_ref[:, rows]
    cos = jnp.cos(ang)
    sin = jnp.sin(ang)

    def rope_t(t1, t2):
        return t1 * cos - t2 * sin, t2 * cos + t1 * sin

    scale = QK_HEAD_DIM ** -0.5 * math.log2(math.e)
    for h in range(MLA_HEADS):
        o = h * QK_PAD
        pe = o + QK_NOPE_DIM
        r1, r2 = rope_t(qt[pe:pe + ROPE_HALF], qt[pe + ROPE_HALF:pe + QK_ROPE_DIM])
        qt_out[tile, o:pe, lanes] = (qt[o:pe] * scale).astype(BF16)
        qt_out[tile, pe:pe + ROPE_HALF, lanes] = (r1 * scale).astype(BF16)
        qt_out[tile, pe + ROPE_HALF:pe + QK_ROPE_DIM, lanes] = (r2 * scale).astype(BF16)
        qt_out[tile, pe + QK_ROPE_DIM:o + QK_PAD, lanes] = jnp.zeros((QK_PAD - QK_HEAD_DIM, ts), BF16)

    r1, r2 = rope_t(tail_t[0:ROPE_HALF], tail_t[ROPE_HALF:QK_ROPE_DIM])
    kpe_t = jnp.concatenate([r1, r2, jnp.zeros((LANES - QK_ROPE_DIM, ts), F32)], axis=0)
    kpe = kpe_t.T.astype(BF16)
    for h in range(MLA_HEADS):
        o = h * QK_PAD
        k_out[rows, o:o + LANES] = kn[:, h * LANES:(h + 1) * LANES].astype(BF16)
        k_out[rows, o + LANES:o + QK_PAD] = kpe

    gates = tail_t[GATE_I:GATE_I + 2 * SUBLANES] + gbias_ref[...]
    pos_in_chunk = lax.rem(r0 + lax.broadcasted_iota(jnp.int32, (SUBLANES, ts), 1), chunk)

    def chunk_scan(v, op, identity):
        shift = 1
        while shift < chunk:
            v = op(v, jnp.where(pos_in_chunk >= shift, pltpu.roll(v, shift, 1), identity))
            shift *= 2
        return v

    b8 = chunk_scan(jax.nn.log_sigmoid(gates[SUBLANES:]), jnp.add, 0.0)
    key8 = gates[:SUBLANES] - b8
    grow_out[:, rows] = jnp.concatenate([key8, chunk_scan(key8, jnp.maximum, -jnp.inf), b8], axis=0)
    keycol_out[rows, :] = jnp.concatenate([key8, jnp.zeros((LANES - SUBLANES, ts), F32)], axis=0).T

    acc = conv_b_ref[...]
    for j in range(CONV_WIDTH):
        start = halo - (CONV_WIDTH - 1) + j + r0
        acc = acc + conv_w_ref[j:j + 1, :] * ext_ref[start:start + ts, :]
    qk = acc * jax.nn.sigmoid(acc)
    mqt_out[tile, :, lanes] = qk[:, :MLSTM_WIDTH].T.astype(BF16)
    mk_out[rows, :] = (qk[:, MLSTM_WIDTH:] * (MLSTM_HEAD_DIM ** -0.5)).astype(BF16)


def _in_proj(x2, pos_row, invf, g_pre, w_in_p, w_tail, qn, kvn, w_uq_t, w_uk, w_uv_t, w_mv_t, vones,
             conv_w, conv_b, gbias, seq):
    t = x2.shape[0]
    tm = TM_IN
    nt = t // TILE_T
    row = lambda width: pl.BlockSpec((tm, width), lambda i: (i, 0))
    tile_t = lambda rows: pl.BlockSpec((tm // TILE_T, rows, TILE_T), lambda i: (i, 0, 0))
    bf = lambda *shape: jax.ShapeDtypeStruct(shape, BF16)
    out_shape = [
        bf(nt, MLA_HEADS * QK_PAD, TILE_T), bf(t, MLA_HEADS * QK_PAD), bf(nt, MLA_HEADS * V_ROWS, TILE_T),
        bf(nt, MLSTM_WIDTH, TILE_T), bf(t, MLSTM_WIDTH), bf(nt, MLSTM_HEADS * V_ROWS, TILE_T),
        bf(t, MLSTM_WIDTH), jax.ShapeDtypeStruct((GATE_ROWS, t), F32),
        jax.ShapeDtypeStruct((t, LANES), F32),
    ]
    out_specs = [
        tile_t(MLA_HEADS * QK_PAD), row(MLA_HEADS * QK_PAD), tile_t(MLA_HEADS * V_ROWS),
        tile_t(MLSTM_WIDTH), row(MLSTM_WIDTH), tile_t(MLSTM_HEADS * V_ROWS), row(MLSTM_WIDTH),
        pl.BlockSpec((GATE_ROWS, tm), lambda i: (0, i)), row(LANES),
    ]
    consts = (invf, g_pre, w_in_p, w_tail, qn, kvn, w_uq_t, w_uk, w_uv_t, w_mv_t, vones, conv_w,
              conv_b, gbias)
    return pl.pallas_call(
        functools.partial(_in_proj_body, tm=tm, sub=SUB_IN, tiles_per_seq=seq // tm, chunk=CHUNK),
        out_shape=out_shape,
        grid=(t // tm,),
        in_specs=[row(D_MODEL), pl.BlockSpec((1, tm), lambda i: (0, i))]
        + [_const_spec(c.shape) for c in consts],
        out_specs=out_specs,
        scratch_shapes=[pltpu.VMEM((tm + 2 * SUBLANES, 2 * MLSTM_WIDTH), F32)],
        compiler_params=pltpu.CompilerParams(dimension_semantics=("arbitrary",),
                                             vmem_limit_bytes=VMEM_LIMIT),
        name="in_proj",
    )(x2, pos_row, *consts)


def _attn_body(qt_ref, k_ref, vt_ref, o_ref, m_ref, use_ref, viol_ref, acc_ref, pa_ref, pb_ref,
               ua_ref, ub_ref, *, tq, heads):
    qi = pl.program_id(2)
    hq = [slice(h * QK_PAD, (h + 1) * QK_PAD) for h in range(heads)]
    hv = [slice(h * V_ROWS, (h + 1) * V_ROWS) for h in range(heads)]
    stat = m_ref.shape[1:]

    def qk(ki, h):
        start = pl.multiple_of(ki * tq, tq)
        return _dot(k_ref[pl.ds(start, tq), hq[h]], qt_ref[0, hq[h], :])

    def causal(s):
        r = lax.broadcasted_iota(jnp.int32, s.shape, 0)
        c = lax.broadcasted_iota(jnp.int32, s.shape, 1)
        return jnp.where(r <= c, s, -jnp.inf)

    def fast_scores(ki, p_ref, u_ref):
        for h in range(heads):
            s = qk(ki, h)
            m_use = m_ref[h, 0:1, :]
            p_ref[h] = jnp.exp2(s - m_use).astype(BF16)
            cmax = jnp.max(s, axis=0, keepdims=True)
            excess = jnp.where(qi > 0, cmax - m_use, 0.0)
            u_ref[h] = jnp.broadcast_to(m_use, stat)
            viol_ref[h] = jnp.broadcast_to(jnp.maximum(viol_ref[h, 0:1, :], excess), stat)
            m_ref[h] = jnp.broadcast_to(jnp.maximum(m_use, cmax), stat)

    def fast_pv(ki, p_ref, u_ref):
        for h in range(heads):
            m_use = u_ref[h, 0:1, :]
            alpha = jnp.exp2(use_ref[h, 0:1, :] - m_use)
            acc_ref[h] = alpha * acc_ref[h] + _dot(vt_ref[ki, hv[h], :], p_ref[h])
            use_ref[h] = jnp.broadcast_to(m_use, stat)

    def exact_block(ki, diagonal):
        for h in range(heads):
            s = causal(qk(ki, h)) if diagonal else qk(ki, h)
            m_old = m_ref[h, 0:1, :]
            m_new = jnp.maximum(m_old, jnp.max(s, axis=0, keepdims=True))
            p = jnp.exp2(s - m_new).astype(BF16)
            acc_ref[h] = jnp.exp2(m_old - m_new) * acc_ref[h] + _dot(vt_ref[ki, hv[h], :], p)
            m_ref[h] = jnp.broadcast_to(m_new, stat)

    viol_ref[...] = jnp.zeros(viol_ref.shape, F32)
    s_diag = [causal(qk(qi, h)) for h in range(heads)]
    for h in range(heads):
        cmax = jnp.broadcast_to(jnp.max(s_diag[h], axis=0, keepdims=True), stat)
        m_ref[h] = cmax
        use_ref[h] = cmax
    fast_scores(0, pa_ref, ua_ref)
    for h in range(heads):
        p = jnp.exp2(s_diag[h] - use_ref[h, 0:1, :]).astype(BF16)
        acc_ref[h] = _dot(vt_ref[qi, hv[h], :], p)

    def pair(j, carry):
        ki = 2 * j
        fast_scores(ki + 1, pb_ref, ub_ref)
        fast_pv(ki, pa_ref, ua_ref)
        fast_scores(ki + 2, pa_ref, ua_ref)
        fast_pv(ki + 1, pb_ref, ub_ref)
        return carry

    lax.fori_loop(0, jnp.maximum(qi - 1, 0) // 2, pair, 0)

    @pl.when(lax.rem(qi, 2) == 1)
    def _():
        fast_pv(qi - 1, pa_ref, ua_ref)

    @pl.when(jnp.logical_and(lax.rem(qi, 2) == 0, qi >= 2))
    def _():
        fast_scores(qi - 1, pb_ref, ub_ref)
        fast_pv(qi - 2, pa_ref, ua_ref)
        fast_pv(qi - 1, pb_ref, ub_ref)

    @pl.when(jnp.max(viol_ref[...]) > MAX_STALE_EXCESS)
    def _():
        m_ref[...] = jnp.full(m_ref.shape, -jnp.inf, F32)
        acc_ref[...] = jnp.zeros(acc_ref.shape, F32)

        def blk(ki, carry):
            exact_block(ki, False)
            return carry

        lax.fori_loop(0, qi, blk, 0)
        exact_block(qi, True)

    for h in range(heads):
        acc = acc_ref[h]
        out_t = acc[0:V_HEAD_DIM] / acc[V_HEAD_DIM:V_HEAD_DIM + 1]
        o_ref[:, h * V_HEAD_DIM:(h + 1) * V_HEAD_DIM] = out_t.T.astype(o_ref.dtype)


def _attention(qt, k, vt, batch, seq):
    t = k.shape[0]
    tq = TQ
    nq = seq // tq
    heads = HEADS_PER_STEP
    return pl.pallas_call(
        functools.partial(_attn_body, tq=tq, heads=heads),
        out_shape=jax.ShapeDtypeStruct((t, MLA_WIDTH), BF16),
        grid=(batch, MLA_HEADS // heads, nq),
        in_specs=[
            pl.BlockSpec((1, heads * QK_PAD, tq), lambda b, g, i: (b * nq + i, g, 0)),
            pl.BlockSpec((seq, heads * QK_PAD), lambda b, g, i: (b, g)),
            pl.BlockSpec((nq, heads * V_ROWS, tq), lambda b, g, i: (b, g, 0)),
        ],
        out_specs=pl.BlockSpec((tq, heads * V_HEAD_DIM), lambda b, g, i: (b * nq + i, g)),
        scratch_shapes=[pltpu.VMEM((heads, SUBLANES, tq), F32),
                        pltpu.VMEM((heads, SUBLANES, tq), F32),
                        pltpu.VMEM((heads, SUBLANES, tq), F32),
                        pltpu.VMEM((heads, V_ROWS, tq), F32),
                        pltpu.VMEM((heads, tq, tq), BF16),
                        pltpu.VMEM((heads, tq, tq), BF16),
                        pltpu.VMEM((heads, SUBLANES, tq), F32),
                        pltpu.VMEM((heads, SUBLANES, tq), F32)],
        compiler_params=pltpu.CompilerParams(
            dimension_semantics=("arbitrary", "arbitrary", "arbitrary"),
            vmem_limit_bytes=VMEM_LIMIT),
        name="mla_attention",
    )(qt, k, vt)


def _mlstm_body(qt_ref, k_ref, vt_ref, og_ref, grow_ref, keycol_ref, norm_ref, out_ref, ct_ref,
                m_ref, *, chunk, chunks_per_step):
    @pl.when(pl.program_id(1) == 0)
    def _():
        ct_ref[...] = jnp.zeros(ct_ref.shape, F32)
        m_ref[...] = jnp.zeros(m_ref.shape, F32)

    heads = range(MLSTM_HEADS)
    hs = [slice(h * MLSTM_HEAD_DIM, (h + 1) * MLSTM_HEAD_DIM) for h in heads]
    vs = [slice(h * V_ROWS, (h + 1) * V_ROWS) for h in heads]
    tok = [slice(c * chunk, (c + 1) * chunk) for c in range(chunks_per_step)]
    r = lax.broadcasted_iota(jnp.int32, (chunk, chunk), 0)
    c = lax.broadcasted_iota(jnp.int32, (chunk, chunk), 1)
    upper = r <= c

    s_raw = [[_dot(k_ref[t, hs[h]], qt_ref[0, hs[h], t]) for h in heads] for t in tok]
    ct = [ct_ref[h] for h in heads]
    m_prev = [m_ref[h, 0:1, 0:1] for h in heads]
    for ci, t in enumerate(tok):
        qt = [qt_ref[0, hs[h], t] for h in heads]
        vt = [vt_ref[0, vs[h], t] for h in heads]
        cq = [_dot(ct[h].astype(BF16), qt[h]) for h in heads]
        grow = grow_ref[:, t]
        mm = [jnp.maximum(m_prev[h], grow[ROW_KMAX + h:ROW_KMAX + h + 1]) for h in heads]
        b_row = [grow[ROW_B + h:ROW_B + h + 1] for h in heads]
        inter = [jnp.exp(m_prev[h] - mm[h]) for h in heads]
        for h in heads:
            mm_last = mm[h][:, chunk - 1:chunk]
            w_row = jnp.exp(grow[ROW_KEY + h:ROW_KEY + h + 1] - mm_last)
            vtw = (vt[h].astype(F32) * w_row).astype(BF16)
            ct[h] = jnp.exp(m_prev[h] - mm_last) * ct[h] + _dot(vtw, k_ref[t, hs[h]])
            m_prev[h] = b_row[h][:, chunk - 1:chunk] + mm_last
        keycol = keycol_ref[t, :]
        for h in heads:
            p = jnp.exp(jnp.where(upper, keycol[:, h:h + 1] - mm[h], -jnp.inf))
            st = (s_raw[ci][h] * p).astype(BF16)
            tot = _dot(vt[h], st) + inter[h] * cq[h]
            den = tot[MLSTM_HEAD_DIM:MLSTM_HEAD_DIM + 1]
            ht = tot[0:MLSTM_HEAD_DIM] / jnp.maximum(jnp.abs(den), jnp.exp(-(b_row[h] + mm[h])))
            hn_t = ht * lax.rsqrt(jnp.mean(ht * ht, axis=0, keepdims=True) + EPS)
            out_ref[t, hs[h]] = (hn_t.T * norm_ref[0:1, hs[h]]
                                 * og_ref[t, hs[h]].astype(F32)).astype(out_ref.dtype)
    for h in heads:
        ct_ref[h] = ct[h]
        m_ref[h] = jnp.broadcast_to(m_prev[h], m_ref.shape[1:])


def _mlstm(mqt, mk, mvt, mo, grow, keycol, norm, batch, seq):
    t = mk.shape[0]
    tile = TILE_T
    ns = seq // tile
    row = lambda width: pl.BlockSpec((tile, width), lambda b, j: (b * ns + j, 0))
    tile_t = lambda rows: pl.BlockSpec((1, rows, tile), lambda b, j: (b * ns + j, 0, 0))
    return pl.pallas_call(
        functools.partial(_mlstm_body, chunk=CHUNK, chunks_per_step=tile // CHUNK),
        out_shape=jax.ShapeDtypeStruct((t, MLSTM_WIDTH), BF16),
        grid=(batch, ns),
        in_specs=[tile_t(MLSTM_WIDTH), row(MLSTM_WIDTH), tile_t(MLSTM_HEADS * V_ROWS),
                  row(MLSTM_WIDTH), pl.BlockSpec((GATE_ROWS, tile), lambda b, j: (0, b * ns + j)),
                  row(LANES), _const_spec(norm.shape)],
        out_specs=row(MLSTM_WIDTH),
        scratch_shapes=[
            pltpu.VMEM((MLSTM_HEADS, V_ROWS, MLSTM_HEAD_DIM), F32),
            pltpu.VMEM((MLSTM_HEADS, SUBLANES, LANES), F32),
        ],
        compiler_params=pltpu.CompilerParams(dimension_semantics=("arbitrary", "arbitrary"),
                                             vmem_limit_bytes=VMEM_LIMIT),
        name="mlstm",
    )(mqt, mk, mvt, mo, grow, keycol, norm)


def _out_ffn_body(x_ref, mla_ref, mlstm_ref, p_ref, w_out_ref, g_post_ref, g_fpre_ref, g_fpost_ref,
                  w_gate_ref, w_up_ref, w_down_ref, w_pg_ref, w_pp_ref, o_ref):
    tm = x_ref.shape[0]
    subs = [slice(r0, r0 + SUB_OUT) for r0 in range(0, tm, SUB_OUT)]
    mix = [_dot(mla_ref[s, :], w_out_ref[0:MLA_WIDTH, :])
           + _dot(mlstm_ref[s, :], w_out_ref[MLA_WIDTH:MLA_WIDTH + MLSTM_WIDTH, :]) for s in subs]
    h1 = [x_ref[s, :] + _rms(m, g_post_ref[...]) for s, m in zip(subs, mix)]
    f = [_rms(h, g_fpre_ref[...]).astype(BF16) for h in h1]
    ffn = [None] * len(subs)
    off = 0
    for width in FF_CHUNKS:
        cols = slice(off, off + width)
        gate = [_dot(v, w_gate_ref[:, cols]) for v in f]
        up = [_dot(v, w_up_ref[:, cols]) for v in f]
        act = [(g * jax.nn.sigmoid(g) * u).astype(BF16) for g, u in zip(gate, up)]
        part = [_dot(a, w_down_ref[cols, :]) for a in act]
        ffn = [p if acc is None else acc + p for acc, p in zip(ffn, part)]
        off += width
    h2 = [h + _rms(v, g_fpost_ref[...]) for h, v in zip(h1, ffn)]
    pgate = [jax.nn.sigmoid(_dot(h.astype(BF16), w_pg_ref[...])) for h in h2]
    for s, h, g in zip(subs, h2, pgate):
        o_ref[s, :] = h + g * _dot(p_ref[s, :].astype(BF16), w_pp_ref[...])


def _out_ffn(x2, mla, mlstm, p2, w_out, g_post, g_fpre, g_fpost, w_gate, w_up, w_down, w_pg, w_pp):
    t = x2.shape[0]
    tm = TM_OUT
    row = lambda width: pl.BlockSpec((tm, width), lambda i: (i, 0))
    consts = (w_out, g_post, g_fpre, g_fpost, w_gate, w_up, w_down, w_pg, w_pp)
    return pl.pallas_call(
        _out_ffn_body,
        out_shape=jax.ShapeDtypeStruct((t, D_MODEL), F32),
        grid=(t // tm,),
        in_specs=[row(D_MODEL), row(MLA_WIDTH), row(MLSTM_WIDTH), row(PLE_DIM)]
        + [_const_spec(c.shape) for c in consts],
        out_specs=row(D_MODEL),
        compiler_params=pltpu.CompilerParams(dimension_semantics=("arbitrary",),
                                             vmem_limit_bytes=VMEM_LIMIT),
        name="out_ffn",
    )(x2, mla, mlstm, p2, *consts)


def _pack_w_in(w_in):
    o_krope = Q_LORA_RANK + KV_LORA_RANK
    o_mq = o_krope + QK_ROPE_DIM
    o_mv = o_mq + 2 * MLSTM_WIDTH
    o_mo = o_mv + MLSTM_WIDTH
    o_gates = o_mo + MLSTM_WIDTH
    main = jnp.concatenate([w_in[:, :o_krope], w_in[:, o_mq:o_mv], w_in[:, o_mo:o_gates]], axis=1)
    gap = jnp.zeros((D_MODEL, SUBLANES - MLSTM_HEADS), w_in.dtype)
    tail = jnp.concatenate([w_in[:, o_krope:o_mq], w_in[:, o_gates:o_gates + MLSTM_HEADS], gap,
                            w_in[:, o_gates + MLSTM_HEADS:], gap], axis=1).T
    tail = jnp.pad(tail, ((0, LANES - tail.shape[0]), (0, 0)))
    mv_t = w_in[:, o_mv:o_mo].T.reshape(MLSTM_HEADS, MLSTM_HEAD_DIM, D_MODEL)
    mv_t = jnp.pad(mv_t, ((0, 0), (0, V_ROWS - MLSTM_HEAD_DIM), (0, 0)))
    return main.astype(BF16), tail.astype(BF16), mv_t.reshape(MLSTM_HEADS * V_ROWS, D_MODEL).astype(BF16)


def _pack_w_uq_t(w_uq):
    w = w_uq.T.reshape(MLA_HEADS, QK_HEAD_DIM, Q_LORA_RANK)
    w = jnp.pad(w, ((0, 0), (0, QK_PAD - QK_HEAD_DIM), (0, 0)))
    return w.reshape(MLA_HEADS * QK_PAD, Q_LORA_RANK).astype(BF16)


def _pack_w_ukv(w_ukv):
    w = w_ukv.reshape(KV_LORA_RANK, MLA_HEADS, QK_NOPE_DIM + V_HEAD_DIM)
    w_uk = w[:, :, :QK_NOPE_DIM].reshape(KV_LORA_RANK, MLA_HEADS * QK_NOPE_DIM)
    w_uv_t = jnp.transpose(w[:, :, QK_NOPE_DIM:], (1, 2, 0))
    w_uv_t = jnp.pad(w_uv_t, ((0, 0), (0, V_ROWS - V_HEAD_DIM), (0, 0)))
    return w_uk.astype(BF16), w_uv_t.reshape(MLA_HEADS * V_ROWS, KV_LORA_RANK).astype(BF16)


def _layer(h, p_i, pos_row, invf, attn_pre_norm, attn_post_norm, w_in, q_norm, kv_norm, w_uq, w_ukv,
           conv_w, conv_b, gate_bias_i, gate_bias_f, mlstm_norm, w_out, ffn_pre_norm, ffn_post_norm,
           w_gate, w_up, w_down, w_ple_proj, w_ple_gate, batch, seq):
    row = lambda a: a.reshape(1, -1)
    w_in_p, w_tail, w_mv_t = _pack_w_in(w_in)
    w_uk, w_uv_t = _pack_w_ukv(w_ukv)
    gap = jnp.zeros((SUBLANES - MLSTM_HEADS,), F32)
    gbias = jnp.concatenate([gate_bias_i, gap, gate_bias_f, gap]).astype(F32).reshape(2 * SUBLANES, 1)
    vones = jnp.zeros((MLA_HEADS, V_ROWS, 1), F32).at[:, V_HEAD_DIM, 0].set(1.0)
    qt, k, vt, mqt, mk, mvt, mo, grow, keycol = _in_proj(
        h, pos_row, invf, row(attn_pre_norm), w_in_p, w_tail, row(q_norm), row(kv_norm),
        _pack_w_uq_t(w_uq), w_uk, w_uv_t, w_mv_t, vones.reshape(MLA_HEADS * V_ROWS, 1), conv_w,
        row(conv_b), gbias, seq)
    mla = _attention(qt, k, vt, batch, seq)
    mlstm = _mlstm(mqt, mk, mvt, mo, grow, keycol, row(mlstm_norm), batch, seq)
    return _out_ffn(h, mla, mlstm, p_i, w_out.astype(BF16), row(attn_post_norm), row(ffn_pre_norm),
                    row(ffn_post_norm), w_gate.astype(BF16), w_up.astype(BF16),
                    w_down.astype(BF16), w_ple_gate.astype(BF16), w_ple_proj.astype(BF16))


@jax.jit
def kernel(x, p, positions, attn_pre_norm, attn_post_norm, w_in, q_norm, kv_norm, w_uq, w_ukv,
           conv_w, conv_b, gate_bias_i, gate_bias_f, mlstm_norm, w_out, ffn_pre_norm, ffn_post_norm,
           w_gate, w_up, w_down, w_ple_proj, w_ple_gate):
    batch, seq, _ = x.shape
    t = batch * seq
    depth = p.shape[0]
    inv_freq = ROPE_THETA ** (-jnp.arange(0, QK_ROPE_DIM, 2, dtype=F32) / QK_ROPE_DIM)
    invf = inv_freq.reshape(ROPE_HALF, 1)
    pos_row = positions.astype(F32).reshape(1, t)
    h = x.reshape(t, D_MODEL)
    for i in range(depth):
        h = _layer(h, p[i].reshape(t, PLE_DIM), pos_row, invf, attn_pre_norm[i], attn_post_norm[i],
                   w_in[i], q_norm[i], kv_norm[i], w_uq[i], w_ukv[i], conv_w[i], conv_b[i],
                   gate_bias_i[i], gate_bias_f[i], mlstm_norm[i], w_out[i], ffn_pre_norm[i],
                   ffn_post_norm[i], w_gate[i], w_up[i], w_down[i], w_ple_proj[i], w_ple_gate[i],
                   batch, seq)
    return h.reshape(batch, seq, D_MODEL)
```

```python
import functools
import math

import jax
import jax.numpy as jnp
from jax import lax
from jax.experimental import pallas as pl
from jax.experimental.pallas import tpu as pltpu

F32 = jnp.float32
BF16 = jnp.bfloat16

D_MODEL = 1024
PLE_DIM = 256
MLA_HEADS = 4
QK_NOPE_DIM = 128
QK_ROPE_DIM = 64
QK_HEAD_DIM = QK_NOPE_DIM + QK_ROPE_DIM
V_HEAD_DIM = 128
Q_LORA_RANK = 256
KV_LORA_RANK = 128
ROPE_THETA = 10000.0
MLA_WIDTH = MLA_HEADS * V_HEAD_DIM
MLSTM_HEADS = 4
MLSTM_HEAD_DIM = 128
MLSTM_WIDTH = MLSTM_HEADS * MLSTM_HEAD_DIM
CONV_WIDTH = 4
CONV_STRIDE = 4
D_FF = 2816
EPS = 1e-6

LANES = 128
SUBLANES = 8
BF16_ROWS = 16
QK_PAD = 2 * LANES
V_ROWS = V_HEAD_DIM + BF16_ROWS
ROPE_HALF = QK_ROPE_DIM // 2
VMEM_LIMIT = 56 * 1024 * 1024

C_Q = 0
C_KV = C_Q + Q_LORA_RANK
C_MQK = C_KV + KV_LORA_RANK
C_MO = C_MQK + 2 * MLSTM_WIDTH
D_IN_PACKED = C_MO + MLSTM_WIDTH
GATE_I = QK_ROPE_DIM
GATE_F = GATE_I + SUBLANES
ROW_KEY, ROW_KMAX, ROW_B, GATE_ROWS = 0, SUBLANES, 2 * SUBLANES, 3 * SUBLANES

TILE_T = 512
TM_IN = 512
SUB_IN = 256
TQ = TILE_T
HEADS_PER_STEP = 2
MAX_STALE_EXCESS = 64.0
CHUNK = 256
TM_OUT = 512
SUB_OUT = 256
FF_CHUNKS = (1024, 1024, 768)


def _rms(x, g):
    return x * lax.rsqrt(jnp.mean(x * x, axis=-1, keepdims=True) + EPS) * g


def _dot(a, b):
    return jnp.dot(a, b, preferred_element_type=F32)


def _dot_nt(a, b):
    return lax.dot_general(a, b, (((1,), (1,)), ((), ())), preferred_element_type=F32)


def _const_spec(shape):
    return pl.BlockSpec(shape, lambda *_: (0,) * len(shape), pipeline_mode=pl.Buffered(1))


def _in_proj_body(x_ref, pos_ref, invf_ref, g_pre_ref, w_in_ref, w_tail_ref, qn_ref, kvn_ref,
                  w_uq_ref, w_uk_ref, w_uv_ref, w_mv_ref, vones_ref, conv_w_ref, conv_b_ref, gbias_ref,
                  qt_out, k_out, vt_out, mqt_out, mk_out, mvt_out, mo_out, grow_out, keycol_out,
                  ext_ref, stage_ref, *, tm, sub, tiles_per_seq, chunk):
    assert tm % sub == 0 and sub % chunk == 0
    i = pl.program_id(0)
    halo = SUBLANES

    @pl.when(lax.rem(i, tiles_per_seq) == 0)
    def _():
        ext_ref[:, 0:halo, :] = jnp.zeros((ext_ref.shape[0], halo, LANES), F32)

    for r0 in range(0, tm, sub):
        _in_proj_rows(r0, sub, halo, chunk, x_ref, pos_ref, invf_ref, g_pre_ref, w_in_ref, w_tail_ref,
                      qn_ref, kvn_ref, w_uq_ref, w_uk_ref, w_uv_ref, w_mv_ref, vones_ref, conv_w_ref,
                      conv_b_ref, gbias_ref, qt_out, k_out, vt_out, mqt_out, mk_out, mvt_out, mo_out,
                      grow_out, keycol_out, ext_ref, stage_ref)
    ext_ref[:, 0:halo, :] = ext_ref[:, tm:tm + halo, :]


def _in_proj_rows(r0, ts, halo, chunk, x_ref, pos_ref, invf_ref, g_pre_ref, w_in_ref, w_tail_ref,
                  qn_ref, kvn_ref, w_uq_ref, w_uk_ref, w_uv_ref, w_mv_ref, vones_ref, conv_w_ref,
                  conv_b_ref, gbias_ref, qt_out, k_out, vt_out, mqt_out, mk_out, mvt_out, mo_out,
                  grow_out, keycol_out, ext_ref, stage_ref):
    rows = slice(r0, r0 + ts)
    tile, lanes = r0 // TILE_T, slice(r0 % TILE_T, r0 % TILE_T + ts)
    u = _rms(x_ref[rows, :], g_pre_ref[...]).astype(BF16)

    cq_raw = _dot(u, w_in_ref[:, C_Q:C_Q + Q_LORA_RANK])
    ckv_raw = _dot(u, w_in_ref[:, C_KV:C_KV + KV_LORA_RANK])
    tail_t = _dot_nt(w_tail_ref[...], u)
    z = _dot(u, w_in_ref[:, C_MQK:C_MQK + 2 * MLSTM_WIDTH])
    for c in range(2 * MLSTM_WIDTH // LANES):
        ext_ref[c, halo + r0:halo + r0 + ts, :] = z[:, c * LANES:(c + 1) * LANES]
    mvt_out[tile, :, lanes] = (_dot_nt(w_mv_ref[...], u) + vones_ref[...]).astype(BF16)
    mo_out[rows, :] = jax.nn.sigmoid(_dot(u, w_in_ref[:, C_MO:C_MO + MLSTM_WIDTH])).astype(BF16)
    cq = _rms(cq_raw, qn_ref[...]).astype(BF16)
    ckv = _rms(ckv_raw, kvn_ref[...]).astype(BF16)
    qt = _dot_nt(w_uq_ref[...], cq)
    kn = _dot(ckv, w_uk_ref[...])
    vt_out[tile, :, lanes] = (_dot_nt(w_uv_ref[...], ckv) + vones_ref[...]).astype(BF16)

    ang = invf_ref[...] * pos_ref[:, rows]
    cos = jnp.cos(ang)
    sin = jnp.sin(ang)

    def rope_t(t1, t2):
        return t1 * cos - t2 * sin, t2 * cos + t1 * sin

    scale = QK_HEAD_DIM ** -0.5 * math.log2(math.e)
    for h in range(MLA_HEADS):
        o = h * QK_PAD
        pe = o + QK_NOPE_DIM
        r1, r2 = rope_t(qt[pe:pe + ROPE_HALF], qt[pe + ROPE_HALF:pe + QK_ROPE_DIM])
        qt_out[tile, o:pe, lanes] = (qt[o:pe] * scale).astype(BF16)
        qt_out[tile, pe:pe + ROPE_HALF, lanes] = (r1 * scale).astype(BF16)
        qt_out[tile, pe + ROPE_HALF:pe + QK_ROPE_DIM, lanes] = (r2 * scale).astype(BF16)
        qt_out[tile, pe + QK_ROPE_DIM:o + QK_PAD, lanes] = jnp.zeros((QK_PAD - QK_HEAD_DIM, ts), BF16)

    r1, r2 = rope_t(tail_t[0:ROPE_HALF], tail_t[ROPE_HALF:QK_ROPE_DIM])
    kpe_t = jnp.concatenate([r1, r2, jnp.zeros((LANES - QK_ROPE_DIM, ts), F32)], axis=0)
    kpe = kpe_t.T.astype(BF16)
    for h in range(MLA_HEADS):
        o = h * QK_PAD
        k_out[rows, o:o + LANES] = kn[:, h * LANES:(h + 1) * LANES].astype(BF16)
        k_out[rows, o + LANES:o + QK_PAD] = kpe

    gates = tail_t[GATE_I:GATE_I + 2 * SUBLANES] + gbias_ref[...]
    pos_in_chunk = lax.rem(r0 + lax.broadcasted_iota(jnp.int32, (SUBLANES, ts), 1), chunk)

    def chunk_scan(v, op, identity):
        shift = 1
        while shift < chunk:
            v = op(v, jnp.where(pos_in_chunk >= shift, pltpu.roll(v, shift, 1), identity))
            shift *= 2
        return v

    b8 = chunk_scan(jax.nn.log_sigmoid(gates[SUBLANES:]), jnp.add, 0.0)
    key8 = gates[:SUBLANES] - b8
    grow_out[:, rows] = jnp.concatenate([key8, chunk_scan(key8, jnp.maximum, -jnp.inf), b8], axis=0)
    keycol_out[rows, :] = jnp.concatenate([key8, jnp.zeros((LANES - SUBLANES, ts), F32)], axis=0).T

    n = ts // CONV_STRIDE
    for c in range(2 * MLSTM_WIDTH // LANES):
        cols = slice(c * LANES, (c + 1) * LANES)
        taps = {q: ext_ref[c, pl.ds(halo + r0 + q, n, stride=CONV_STRIDE), :]
                for q in range(1 - CONV_WIDTH, CONV_STRIDE)}
        for r in range(CONV_STRIDE):
            acc = conv_b_ref[:, cols]
            for j in range(CONV_WIDTH):
                acc = acc + conv_w_ref[j:j + 1, cols] * taps[r - (CONV_WIDTH - 1) + j]
            stage_ref[c, pl.ds(r0 + r, n, stride=CONV_STRIDE), :] = acc * jax.nn.sigmoid(acc)
    for c in range(MLSTM_WIDTH // LANES):
        cols = slice(c * LANES, (c + 1) * LANES)
        mqt_out[tile, cols, lanes] = stage_ref[c, rows, :].T.astype(BF16)
        k_slab = stage_ref[MLSTM_WIDTH // LANES + c, rows, :]
        mk_out[rows, cols] = (k_slab * (MLSTM_HEAD_DIM ** -0.5)).astype(BF16)


def _in_proj(x2, pos_row, invf, g_pre, w_in_p, w_tail, qn, kvn, w_uq_t, w_uk, w_uv_t, w_mv_t, vones,
             conv_w, conv_b, gbias, seq):
    t = x2.shape[0]
    tm = TM_IN
    nt = t // TILE_T
    row = lambda width: pl.BlockSpec((tm, width), lambda i: (i, 0))
    tile_t = lambda rows: pl.BlockSpec((tm // TILE_T, rows, TILE_T), lambda i: (i, 0, 0))
    bf = lambda *shape: jax.ShapeDtypeStruct(shape, BF16)
    out_shape = [
        bf(nt, MLA_HEADS * QK_PAD, TILE_T), bf(t, MLA_HEADS * QK_PAD), bf(nt, MLA_HEADS * V_ROWS, TILE_T),
        bf(nt, MLSTM_WIDTH, TILE_T), bf(t, MLSTM_WIDTH), bf(nt, MLSTM_HEADS * V_ROWS, TILE_T),
        bf(t, MLSTM_WIDTH), jax.ShapeDtypeStruct((GATE_ROWS, t), F32),
        jax.ShapeDtypeStruct((t, LANES), F32),
    ]
    out_specs = [
        tile_t(MLA_HEADS * QK_PAD), row(MLA_HEADS * QK_PAD), tile_t(MLA_HEADS * V_ROWS),
        tile_t(MLSTM_WIDTH), row(MLSTM_WIDTH), tile_t(MLSTM_HEADS * V_ROWS), row(MLSTM_WIDTH),
        pl.BlockSpec((GATE_ROWS, tm), lambda i: (0, i)), row(LANES),
    ]
    consts = (invf, g_pre, w_in_p, w_tail, qn, kvn, w_uq_t, w_uk, w_uv_t, w_mv_t, vones, conv_w,
              conv_b, gbias)
    return pl.pallas_call(
        functools.partial(_in_proj_body, tm=tm, sub=SUB_IN, tiles_per_seq=seq // tm, chunk=CHUNK),
        out_shape=out_shape,
        grid=(t // tm,),
        in_specs=[row(D_MODEL), pl.BlockSpec((1, tm), lambda i: (0, i))]
        + [_const_spec(c.shape) for c in consts],
        out_specs=out_specs,
        scratch_shapes=[pltpu.VMEM((2 * MLSTM_WIDTH // LANES, tm + 2 * SUBLANES, LANES), F32),
                        pltpu.VMEM((2 * MLSTM_WIDTH // LANES, tm, LANES), F32)],
        compiler_params=pltpu.CompilerParams(dimension_semantics=("arbitrary",),
                                             vmem_limit_bytes=VMEM_LIMIT),
        name="in_proj",
    )(x2, pos_row, *consts)


def _attn_body(qt_ref, k_ref, vt_ref, o_ref, m_ref, use_ref, viol_ref, acc_ref, pa_ref, pb_ref,
               ua_ref, ub_ref, *, tq, heads):
    qi = pl.program_id(2)
    hq = [slice(h * QK_PAD, (h + 1) * QK_PAD) for h in range(heads)]
    hv = [slice(h * V_ROWS, (h + 1) * V_ROWS) for h in range(heads)]
    stat = m_ref.shape[1:]

    def qk(ki, h):
        start = pl.multiple_of(ki * tq, tq)
        return _dot(k_ref[pl.ds(start, tq), hq[h]], qt_ref[0, hq[h], :])

    def causal(s):
        r = lax.broadcasted_iota(jnp.int32, s.shape, 0)
        c = lax.broadcasted_iota(jnp.int32, s.shape, 1)
        return jnp.where(r <= c, s, -jnp.inf)

    def fast_scores(ki, p_ref, u_ref):
        for h in range(heads):
            s = qk(ki, h)
            m_use = m_ref[h, 0:1, :]
            p_ref[h] = jnp.exp2(s - m_use).astype(BF16)
            cmax = jnp.max(s, axis=0, keepdims=True)
            excess = jnp.where(qi > 0, cmax - m_use, 0.0)
            u_ref[h] = jnp.broadcast_to(m_use, stat)
            viol_ref[h] = jnp.broadcast_to(jnp.maximum(viol_ref[h, 0:1, :], excess), stat)
            m_ref[h] = jnp.broadcast_to(jnp.maximum(m_use, cmax), stat)

    def fast_pv(ki, p_ref, u_ref):
        for h in range(heads):
            m_use = u_ref[h, 0:1, :]
            alpha = jnp.exp2(use_ref[h, 0:1, :] - m_use)
            acc_ref[h] = alpha * acc_ref[h] + _dot(vt_ref[ki, hv[h], :], p_ref[h])
            use_ref[h] = jnp.broadcast_to(m_use, stat)

    def exact_block(ki, diagonal):
        for h in range(heads):
            s = causal(qk(ki, h)) if diagonal else qk(ki, h)
            m_old = m_ref[h, 0:1, :]
            m_new = jnp.maximum(m_old, jnp.max(s, axis=0, keepdims=True))
            p = jnp.exp2(s - m_new).astype(BF16)
            acc_ref[h] = jnp.exp2(m_old - m_new) * acc_ref[h] + _dot(vt_ref[ki, hv[h], :], p)
            m_ref[h] = jnp.broadcast_to(m_new, stat)

    viol_ref[...] = jnp.zeros(viol_ref.shape, F32)
    s_diag = [causal(qk(qi, h)) for h in range(heads)]
    for h in range(heads):
        cmax = jnp.broadcast_to(jnp.max(s_diag[h], axis=0, keepdims=True), stat)
        m_ref[h] = cmax
        use_ref[h] = cmax
    fast_scores(0, pa_ref, ua_ref)
    for h in range(heads):
        p = jnp.exp2(s_diag[h] - use_ref[h, 0:1, :]).astype(BF16)
        acc_ref[h] = _dot(vt_ref[qi, hv[h], :], p)

    def pair(j, carry):
        ki = 2 * j
        fast_scores(ki + 1, pb_ref, ub_ref)
        fast_pv(ki, pa_ref, ua_ref)
        fast_scores(ki + 2, pa_ref, ua_ref)
        fast_pv(ki + 1, pb_ref, ub_ref)
        return carry

    lax.fori_loop(0, jnp.maximum(qi - 1, 0) // 2, pair, 0)

    @pl.when(lax.rem(qi, 2) == 1)
    def _():
        fast_pv(qi - 1, pa_ref, ua_ref)

    @pl.when(jnp.logical_and(lax.rem(qi, 2) == 0, qi >= 2))
    def _():
        fast_scores(qi - 1, pb_ref, ub_ref)
        fast_pv(qi - 2, pa_ref, ua_ref)
        fast_pv(qi - 1, pb_ref, ub_ref)

    @pl.when(jnp.max(viol_ref[...]) > MAX_STALE_EXCESS)
    def _():
        m_ref[...] = jnp.full(m_ref.shape, -jnp.inf, F32)
        acc_ref[...] = jnp.zeros(acc_ref.shape, F32)

        def blk(ki, carry):
            exact_block(ki, False)
            return carry

        lax.fori_loop(0, qi, blk, 0)
        exact_block(qi, True)

    for h in range(heads):
        acc = acc_ref[h]
        out_t = acc[0:V_HEAD_DIM] / acc[V_HEAD_DIM:V_HEAD_DIM + 1]
        o_ref[:, h * V_HEAD_DIM:(h + 1) * V_HEAD_DIM] = out_t.T.astype(o_ref.dtype)


def _attention(qt, k, vt, batch, seq):
    t = k.shape[0]
    tq = TQ
    nq = seq // tq
    heads = HEADS_PER_STEP
    return pl.pallas_call(
        functools.partial(_attn_body, tq=tq, heads=heads),
        out_shape=jax.ShapeDtypeStruct((t, MLA_WIDTH), BF16),
        grid=(batch, MLA_HEADS // heads, nq),
        in_specs=[
            pl.BlockSpec((1, heads * QK_PAD, tq), lambda b, g, i: (b * nq + i, g, 0)),
            pl.BlockSpec((seq, heads * QK_PAD), lambda b, g, i: (b, g)),
            pl.BlockSpec((nq, heads * V_ROWS, tq), lambda b, g, i: (b, g, 0)),
        ],
        out_specs=pl.BlockSpec((tq, heads * V_HEAD_DIM), lambda b, g, i: (b * nq + i, g)),
        scratch_shapes=[pltpu.VMEM((heads, SUBLANES, tq), F32),
                        pltpu.VMEM((heads, SUBLANES, tq), F32),
                        pltpu.VMEM((heads, SUBLANES, tq), F32),
                        pltpu.VMEM((heads, V_ROWS, tq), F32),
                        pltpu.VMEM((heads, tq, tq), BF16),
                        pltpu.VMEM((heads, tq, tq), BF16),
                        pltpu.VMEM((heads, SUBLANES, tq), F32),
                        pltpu.VMEM((heads, SUBLANES, tq), F32)],
        compiler_params=pltpu.CompilerParams(
            dimension_semantics=("arbitrary", "arbitrary", "arbitrary"),
            vmem_limit_bytes=VMEM_LIMIT),
        name="mla_attention",
    )(qt, k, vt)


def _mlstm_body(qt_ref, k_ref, vt_ref, og_ref, grow_ref, keycol_ref, norm_ref, out_ref, ct_ref,
                m_ref, *, chunk, chunks_per_step):
    @pl.when(pl.program_id(1) == 0)
    def _():
        ct_ref[...] = jnp.zeros(ct_ref.shape, F32)
        m_ref[...] = jnp.zeros(m_ref.shape, F32)

    heads = range(MLSTM_HEADS)
    hs = [slice(h * MLSTM_HEAD_DIM, (h + 1) * MLSTM_HEAD_DIM) for h in heads]
    vs = [slice(h * V_ROWS, (h + 1) * V_ROWS) for h in heads]
    tok = [slice(c * chunk, (c + 1) * chunk) for c in range(chunks_per_step)]
    r = lax.broadcasted_iota(jnp.int32, (chunk, chunk), 0)
    c = lax.broadcasted_iota(jnp.int32, (chunk, chunk), 1)
    upper = r <= c

    s_raw = [[_dot(k_ref[t, hs[h]], qt_ref[0, hs[h], t]) for h in heads] for t in tok]
    ct = [ct_ref[h] for h in heads]
    m_prev = [m_ref[h, 0:1, 0:1] for h in heads]
    for ci, t in enumerate(tok):
        qt = [qt_ref[0, hs[h], t] for h in heads]
        vt = [vt_ref[0, vs[h], t] for h in heads]
        cq = [_dot(ct[h].astype(BF16), qt[h]) for h in heads]
        grow = grow_ref[:, t]
        mm = [jnp.maximum(m_prev[h], grow[ROW_KMAX + h:ROW_KMAX + h + 1]) for h in heads]
        b_row = [grow[ROW_B + h:ROW_B + h + 1] for h in heads]
        inter = [jnp.exp(m_prev[h] - mm[h]) for h in heads]
        for h in heads:
            mm_last = mm[h][:, chunk - 1:chunk]
            w_row = jnp.exp(grow[ROW_KEY + h:ROW_KEY + h + 1] - mm_last)
            vtw = (vt[h].astype(F32) * w_row).astype(BF16)
            ct[h] = jnp.exp(m_prev[h] - mm_last) * ct[h] + _dot(vtw, k_ref[t, hs[h]])
            m_prev[h] = b_row[h][:, chunk - 1:chunk] + mm_last
        keycol = keycol_ref[t, :]
        for h in heads:
            p = jnp.exp(jnp.where(upper, keycol[:, h:h + 1] - mm[h], -jnp.inf))
            st = (s_raw[ci][h] * p).astype(BF16)
            tot = _dot(vt[h], st) + inter[h] * cq[h]
            den = tot[MLSTM_HEAD_DIM:MLSTM_HEAD_DIM + 1]
            ht = tot[0:MLSTM_HEAD_DIM] / jnp.maximum(jnp.abs(den), jnp.exp(-(b_row[h] + mm[h])))
            hn_t = ht * lax.rsqrt(jnp.mean(ht * ht, axis=0, keepdims=True) + EPS)
            out_ref[t, hs[h]] = (hn_t.T * norm_ref[0:1, hs[h]]
                                 * og_ref[t, hs[h]].astype(F32)).astype(out_ref.dtype)
    for h in heads:
        ct_ref[h] = ct[h]
        m_ref[h] = jnp.broadcast_to(m_prev[h], m_ref.shape[1:])


def _mlstm(mqt, mk, mvt, mo, grow, keycol, norm, batch, seq):
    t = mk.shape[0]
    tile = TILE_T
    ns = seq // tile
    row = lambda width: pl.BlockSpec((tile, width), lambda b, j: (b * ns + j, 0))
    tile_t = lambda rows: pl.BlockSpec((1, rows, tile), lambda b, j: (b * ns + j, 0, 0))
    return pl.pallas_call(
        functools.partial(_mlstm_body, chunk=CHUNK, chunks_per_step=tile // CHUNK),
        out_shape=jax.ShapeDtypeStruct((t, MLSTM_WIDTH), BF16),
        grid=(batch, ns),
        in_specs=[tile_t(MLSTM_WIDTH), row(MLSTM_WIDTH), tile_t(MLSTM_HEADS * V_ROWS),
                  row(MLSTM_WIDTH), pl.BlockSpec((GATE_ROWS, tile), lambda b, j: (0, b * ns + j)),
                  row(LANES), _const_spec(norm.shape)],
        out_specs=row(MLSTM_WIDTH),
        scratch_shapes=[
            pltpu.VMEM((MLSTM_HEADS, V_ROWS, MLSTM_HEAD_DIM), F32),
            pltpu.VMEM((MLSTM_HEADS, SUBLANES, LANES), F32),
        ],
        compiler_params=pltpu.CompilerParams(dimension_semantics=("arbitrary", "arbitrary"),
                                             vmem_limit_bytes=VMEM_LIMIT),
        name="mlstm",
    )(mqt, mk, mvt, mo, grow, keycol, norm)


def _out_ffn_body(x_ref, mla_ref, mlstm_ref, p_ref, w_out_ref, g_post_ref, g_fpre_ref, g_fpost_ref,
                  w_gate_ref, w_up_ref, w_down_ref, w_pg_ref, w_pp_ref, o_ref):
    tm = x_ref.shape[0]
    subs = [slice(r0, r0 + SUB_OUT) for r0 in range(0, tm, SUB_OUT)]
    mix = [_dot(mla_ref[s, :], w_out_ref[0:MLA_WIDTH, :])
           + _dot(mlstm_ref[s, :], w_out_ref[MLA_WIDTH:MLA_WIDTH + MLSTM_WIDTH, :]) for s in subs]
    h1 = [x_ref[s, :] + _rms(m, g_post_ref[...]) for s, m in zip(subs, mix)]
    f = [_rms(h, g_fpre_ref[...]).astype(BF16) for h in h1]
    ffn = [None] * len(subs)
    off = 0
    for width in FF_CHUNKS:
        cols = slice(off, off + width)
        gate = [_dot(v, w_gate_ref[:, cols]) for v in f]
        up = [_dot(v, w_up_ref[:, cols]) for v in f]
        act = [(g * jax.nn.sigmoid(g) * u).astype(BF16) for g, u in zip(gate, up)]
        part = [_dot(a, w_down_ref[cols, :]) for a in act]
        ffn = [p if acc is None else acc + p for acc, p in zip(ffn, part)]
        off += width
    h2 = [h + _rms(v, g_fpost_ref[...]) for h, v in zip(h1, ffn)]
    pgate = [jax.nn.sigmoid(_dot(h.astype(BF16), w_pg_ref[...])) for h in h2]
    for s, h, g in zip(subs, h2, pgate):
        o_ref[s, :] = h + g * _dot(p_ref[s, :].astype(BF16), w_pp_ref[...])


def _out_ffn(x2, mla, mlstm, p2, w_out, g_post, g_fpre, g_fpost, w_gate, w_up, w_down, w_pg, w_pp):
    t = x2.shape[0]
    tm = TM_OUT
    row = lambda width: pl.BlockSpec((tm, width), lambda i: (i, 0))
    consts = (w_out, g_post, g_fpre, g_fpost, w_gate, w_up, w_down, w_pg, w_pp)
    return pl.pallas_call(
        _out_ffn_body,
        out_shape=jax.ShapeDtypeStruct((t, D_MODEL), F32),
        grid=(t // tm,),
        in_specs=[row(D_MODEL), row(MLA_WIDTH), row(MLSTM_WIDTH), row(PLE_DIM)]
        + [_const_spec(c.shape) for c in consts],
        out_specs=row(D_MODEL),
        compiler_params=pltpu.CompilerParams(dimension_semantics=("arbitrary",),
                                             vmem_limit_bytes=VMEM_LIMIT),
        name="out_ffn",
    )(x2, mla, mlstm, p2, *consts)


def _pack_w_in(w_in):
    o_krope = Q_LORA_RANK + KV_LORA_RANK
    o_mq = o_krope + QK_ROPE_DIM
    o_mv = o_mq + 2 * MLSTM_WIDTH
    o_mo = o_mv + MLSTM_WIDTH
    o_gates = o_mo + MLSTM_WIDTH
    main = jnp.concatenate([w_in[:, :o_krope], w_in[:, o_mq:o_mv], w_in[:, o_mo:o_gates]], axis=1)
    gap = jnp.zeros((D_MODEL, SUBLANES - MLSTM_HEADS), w_in.dtype)
    tail = jnp.concatenate([w_in[:, o_krope:o_mq], w_in[:, o_gates:o_gates + MLSTM_HEADS], gap,
                            w_in[:, o_gates + MLSTM_HEADS:], gap], axis=1).T
    tail = jnp.pad(tail, ((0, LANES - tail.shape[0]), (0, 0)))
    mv_t = w_in[:, o_mv:o_mo].T.reshape(MLSTM_HEADS, MLSTM_HEAD_DIM, D_MODEL)
    mv_t = jnp.pad(mv_t, ((0, 0), (0, V_ROWS - MLSTM_HEAD_DIM), (0, 0)))
    return main.astype(BF16), tail.astype(BF16), mv_t.reshape(MLSTM_HEADS * V_ROWS, D_MODEL).astype(BF16)


def _pack_w_uq_t(w_uq):
    w = w_uq.T.reshape(MLA_HEADS, QK_HEAD_DIM, Q_LORA_RANK)
    w = jnp.pad(w, ((0, 0), (0, QK_PAD - QK_HEAD_DIM), (0, 0)))
    return w.reshape(MLA_HEADS * QK_PAD, Q_LORA_RANK).astype(BF16)


def _pack_w_ukv(w_ukv):
    w = w_ukv.reshape(KV_LORA_RANK, MLA_HEADS, QK_NOPE_DIM + V_HEAD_DIM)
    w_uk = w[:, :, :QK_NOPE_DIM].reshape(KV_LORA_RANK, MLA_HEADS * QK_NOPE_DIM)
    w_uv_t = jnp.transpose(w[:, :, QK_NOPE_DIM:], (1, 2, 0))
    w_uv_t = jnp.pad(w_uv_t, ((0, 0), (0, V_ROWS - V_HEAD_DIM), (0, 0)))
    return w_uk.astype(BF16), w_uv_t.reshape(MLA_HEADS * V_ROWS, KV_LORA_RANK).astype(BF16)


def _layer(h, p_i, pos_row, invf, attn_pre_norm, attn_post_norm, w_in, q_norm, kv_norm, w_uq, w_ukv,
           conv_w, conv_b, gate_bias_i, gate_bias_f, mlstm_norm, w_out, ffn_pre_norm, ffn_post_norm,
           w_gate, w_up, w_down, w_ple_proj, w_ple_gate, batch, seq):
    row = lambda a: a.reshape(1, -1)
    w_in_p, w_tail, w_mv_t = _pack_w_in(w_in)
    w_uk, w_uv_t = _pack_w_ukv(w_ukv)
    gap = jnp.zeros((SUBLANES - MLSTM_HEADS,), F32)
    gbias = jnp.concatenate([gate_bias_i, gap, gate_bias_f, gap]).astype(F32).reshape(2 * SUBLANES, 1)
    vones = jnp.zeros((MLA_HEADS, V_ROWS, 1), F32).at[:, V_HEAD_DIM, 0].set(1.0)
    qt, k, vt, mqt, mk, mvt, mo, grow, keycol = _in_proj(
        h, pos_row, invf, row(attn_pre_norm), w_in_p, w_tail, row(q_norm), row(kv_norm),
        _pack_w_uq_t(w_uq), w_uk, w_uv_t, w_mv_t, vones.reshape(MLA_HEADS * V_ROWS, 1), conv_w,
        row(conv_b), gbias, seq)
    mla = _attention(qt, k, vt, batch, seq)
    mlstm = _mlstm(mqt, mk, mvt, mo, grow, keycol, row(mlstm_norm), batch, seq)
    return _out_ffn(h, mla, mlstm, p_i, w_out.astype(BF16), row(attn_post_norm), row(ffn_pre_norm),
                    row(ffn_post_norm), w_gate.astype(BF16), w_up.astype(BF16),
                    w_down.astype(BF16), w_ple_gate.astype(BF16), w_ple_proj.astype(BF16))


@jax.jit
def kernel(x, p, positions, attn_pre_norm, attn_post_norm, w_in, q_norm, kv_norm, w_uq, w_ukv,
           conv_w, conv_b, gate_bias_i, gate_bias_f, mlstm_norm, w_out, ffn_pre_norm, ffn_post_norm,
           w_gate, w_up, w_down, w_ple_proj, w_ple_gate):
    batch, seq, _ = x.shape
    t = batch * seq
    depth = p.shape[0]
    inv_freq = ROPE_THETA ** (-jnp.arange(0, QK_ROPE_DIM, 2, dtype=F32) / QK_ROPE_DIM)
    invf = inv_freq.reshape(ROPE_HALF, 1)
    pos_row = positions.astype(F32).reshape(1, t)
    h = x.reshape(t, D_MODEL)
    for i in range(depth):
        h = _layer(h, p[i].reshape(t, PLE_DIM), pos_row, invf, attn_pre_norm[i], attn_post_norm[i],
                   w_in[i], q_norm[i], kv_norm[i], w_uq[i], w_ukv[i], conv_w[i], conv_b[i],
                   gate_bias_i[i], gate_bias_f[i], mlstm_norm[i], w_out[i], ffn_pre_norm[i],
                   ffn_post_norm[i], w_gate[i], w_up[i], w_down[i], w_ple_proj[i], w_ple_gate[i],
                   batch, seq)
    return h.reshape(batch, seq, D_MODEL)
```

```python
import functools
import math

import jax
import jax.numpy as jnp
from jax import lax
from jax.experimental import pallas as pl
from jax.experimental.pallas import tpu as pltpu

F32 = jnp.float32
BF16 = jnp.bfloat16

D_MODEL = 1024
PLE_DIM = 256
MLA_HEADS = 4
QK_NOPE_DIM = 128
QK_ROPE_DIM = 64
QK_HEAD_DIM = QK_NOPE_DIM + QK_ROPE_DIM
V_HEAD_DIM = 128
Q_LORA_RANK = 256
KV_LORA_RANK = 128
ROPE_THETA = 10000.0
MLA_WIDTH = MLA_HEADS * V_HEAD_DIM
MLSTM_HEADS = 4
MLSTM_HEAD_DIM = 128
MLSTM_WIDTH = MLSTM_HEADS * MLSTM_HEAD_DIM
CONV_WIDTH = 4
CONV_STRIDE = 4
D_FF = 2816
EPS = 1e-6

LANES = 128
SUBLANES = 8
BF16_ROWS = 16
QK_PAD = 2 * LANES
V_ROWS = V_HEAD_DIM + BF16_ROWS
ROPE_HALF = QK_ROPE_DIM // 2
VMEM_LIMIT = 56 * 1024 * 1024

C_Q = 0
C_KV = C_Q + Q_LORA_RANK
C_MQK = C_KV + KV_LORA_RANK
C_MO = C_MQK + 2 * MLSTM_WIDTH
D_IN_PACKED = C_MO + MLSTM_WIDTH
GATE_I = QK_ROPE_DIM
GATE_F = GATE_I + SUBLANES
ROW_KEY, ROW_KMAX, ROW_B, GATE_ROWS = 0, SUBLANES, 2 * SUBLANES, 3 * SUBLANES

TILE_T = 512
TM_IN = 512
SUB_IN = 256
TQ = TILE_T
HEADS_PER_STEP = 2
MAX_STALE_EXCESS = 64.0
CHUNK = 256
TM_OUT = 512
SUB_OUT = 256
FF_CHUNKS = (1024, 1024, 768)


def _rms(x, g):
    return x * lax.rsqrt(jnp.mean(x * x, axis=-1, keepdims=True) + EPS) * g


def _dot(a, b):
    return jnp.dot(a, b, preferred_element_type=F32)


def _dot_nt(a, b):
    return lax.dot_general(a, b, (((1,), (1,)), ((), ())), preferred_element_type=F32)


def _const_spec(shape):
    return pl.BlockSpec(shape, lambda *_: (0,) * len(shape), pipeline_mode=pl.Buffered(1))


def _in_proj_body(x_ref, pos_ref, invf_ref, g_pre_ref, w_in_ref, w_tail_ref, qn_ref, kvn_ref,
                  w_uq_ref, w_uk_ref, w_uv_ref, w_mv_ref, vones_ref, conv_w_ref, conv_b_ref, gbias_ref,
                  qt_out, k_out, vt_out, mqt_out, mk_out, mvt_out, mo_out, grow_out, keycol_out,
                  ext_ref, stage_ref, *, tm, sub, tiles_per_seq, chunk):
    assert tm % sub == 0 and sub % chunk == 0
    i = pl.program_id(0)
    halo = SUBLANES

    @pl.when(lax.rem(i, tiles_per_seq) == 0)
    def _():
        ext_ref[:, 0:halo, :] = jnp.zeros((ext_ref.shape[0], halo, LANES), F32)

    for r0 in range(0, tm, sub):
        _in_proj_rows(r0, sub, halo, chunk, x_ref, pos_ref, invf_ref, g_pre_ref, w_in_ref, w_tail_ref,
                      qn_ref, kvn_ref, w_uq_ref, w_uk_ref, w_uv_ref, w_mv_ref, vones_ref, conv_w_ref,
                      conv_b_ref, gbias_ref, qt_out, k_out, vt_out, mqt_out, mk_out, mvt_out, mo_out,
                      grow_out, keycol_out, ext_ref, stage_ref)
    ext_ref[:, 0:halo, :] = ext_ref[:, tm:tm + halo, :]


def _in_proj_rows(r0, ts, halo, chunk, x_ref, pos_ref, invf_ref, g_pre_ref, w_in_ref, w_tail_ref,
                  qn_ref, kvn_ref, w_uq_ref, w_uk_ref, w_uv_ref, w_mv_ref, vones_ref, conv_w_ref,
                  conv_b_ref, gbias_ref, qt_out, k_out, vt_out, mqt_out, mk_out, mvt_out, mo_out,
                  grow_out, keycol_out, ext_ref, stage_ref):
    rows = slice(r0, r0 + ts)
    tile, lanes = r0 // TILE_T, slice(r0 % TILE_T, r0 % TILE_T + ts)
    u = _rms(x_ref[rows, :], g_pre_ref[...]).astype(BF16)

    cq_raw = _dot(u, w_in_ref[:, C_Q:C_Q + Q_LORA_RANK])
    ckv_raw = _dot(u, w_in_ref[:, C_KV:C_KV + KV_LORA_RANK])
    tail_t = _dot_nt(w_tail_ref[...], u)
    z = _dot(u, w_in_ref[:, C_MQK:C_MQK + 2 * MLSTM_WIDTH])
    for c in range(2 * MLSTM_WIDTH // LANES):
        ext_ref[c, halo + r0:halo + r0 + ts, :] = z[:, c * LANES:(c + 1) * LANES]
    mvt_out[tile, :, lanes] = (_dot_nt(w_mv_ref[...], u) + vones_ref[...]).astype(BF16)
    mo_out[rows, :] = jax.nn.sigmoid(_dot(u, w_in_ref[:, C_MO:C_MO + MLSTM_WIDTH])).astype(BF16)
    cq = _rms(cq_raw, qn_ref[...]).astype(BF16)
    ckv = _rms(ckv_raw, kvn_ref[...]).astype(BF16)
    qt = _dot_nt(w_uq_ref[...], cq)
    kn = _dot(ckv, w_uk_ref[...])
    vt_out[tile, :, lanes] = (_dot_nt(w_uv_ref[...], ckv) + vones_ref[...]).astype(BF16)

    ang = invf_ref[...] * pos_ref[:, rows]
    cos = jnp.cos(ang)
    sin = jnp.sin(ang)

    def rope_t(t1, t2):
        return t1 * cos - t2 * sin, t2 * cos + t1 * sin

    scale = QK_HEAD_DIM ** -0.5 * math.log2(math.e)
    for h in range(MLA_HEADS):
        o = h * QK_PAD
        pe = o + QK_NOPE_DIM
        r1, r2 = rope_t(qt[pe:pe + ROPE_HALF], qt[pe + ROPE_HALF:pe + QK_ROPE_DIM])
        qt_out[tile, o:pe, lanes] = (qt[o:pe] * scale).astype(BF16)
        qt_out[tile, pe:pe + ROPE_HALF, lanes] = (r1 * scale).astype(BF16)
        qt_out[tile, pe + ROPE_HALF:pe + QK_ROPE_DIM, lanes] = (r2 * scale).astype(BF16)
        qt_out[tile, pe + QK_ROPE_DIM:o + QK_PAD, lanes] = jnp.zeros((QK_PAD - QK_HEAD_DIM, ts), BF16)

    r1, r2 = rope_t(tail_t[0:ROPE_HALF], tail_t[ROPE_HALF:QK_ROPE_DIM])
    kpe_t = jnp.concatenate([r1, r2, jnp.zeros((LANES - QK_ROPE_DIM, ts), F32)], axis=0)
    kpe = kpe_t.T.astype(BF16)
    for h in range(MLA_HEADS):
        o = h * QK_PAD
        k_out[rows, o:o + LANES] = kn[:, h * LANES:(h + 1) * LANES].astype(BF16)
        k_out[rows, o + LANES:o + QK_PAD] = kpe

    gates = tail_t[GATE_I:GATE_I + 2 * SUBLANES] + gbias_ref[...]
    pos_in_chunk = lax.rem(r0 + lax.broadcasted_iota(jnp.int32, (SUBLANES, ts), 1), chunk)

    def chunk_scan(v, op, identity):
        shift = 1
        while shift < chunk:
            v = op(v, jnp.where(pos_in_chunk >= shift, pltpu.roll(v, shift, 1), identity))
            shift *= 2
        return v

    b8 = chunk_scan(jax.nn.log_sigmoid(gates[SUBLANES:]), jnp.add, 0.0)
    key8 = gates[:SUBLANES] - b8
    grow_out[:, rows] = jnp.concatenate([key8, chunk_scan(key8, jnp.maximum, -jnp.inf), b8], axis=0)
    keycol_out[rows, :] = jnp.concatenate([key8, jnp.zeros((LANES - SUBLANES, ts), F32)], axis=0).T

    n = ts // CONV_STRIDE
    for c in range(2 * MLSTM_WIDTH // LANES):
        cols = slice(c * LANES, (c + 1) * LANES)
        taps = {q: ext_ref[c, pl.ds(halo + r0 + q, n, stride=CONV_STRIDE), :]
                for q in range(1 - CONV_WIDTH, CONV_STRIDE)}
        for r in range(CONV_STRIDE):
            acc = conv_b_ref[:, cols]
            for j in range(CONV_WIDTH):
                acc = acc + conv_w_ref[j:j + 1, cols] * taps[r - (CONV_WIDTH - 1) + j]
            stage_ref[c, pl.ds(r0 + r, n, stride=CONV_STRIDE), :] = acc * jax.nn.sigmoid(acc)
    for c in range(MLSTM_WIDTH // LANES):
        cols = slice(c * LANES, (c + 1) * LANES)
        mqt_out[tile, cols, lanes] = stage_ref[c, rows, :].T.astype(BF16)
        k_slab = stage_ref[MLSTM_WIDTH // LANES + c, rows, :]
        mk_out[rows, cols] = (k_slab * (MLSTM_HEAD_DIM ** -0.5)).astype(BF16)


def _in_proj(x2, pos_row, invf, g_pre, w_in_p, w_tail, qn, kvn, w_uq_t, w_uk, w_uv_t, w_mv_t, vones,
             conv_w, conv_b, gbias, seq):
    t = x2.shape[0]
    tm = TM_IN
    nt = t // TILE_T
    row = lambda width: pl.BlockSpec((tm, width), lambda i: (i, 0))
    tile_t = lambda rows: pl.BlockSpec((tm // TILE_T, rows, TILE_T), lambda i: (i, 0, 0))
    bf = lambda *shape: jax.ShapeDtypeStruct(shape, BF16)
    out_shape = [
        bf(nt, MLA_HEADS * QK_PAD, TILE_T), bf(t, MLA_HEADS * QK_PAD), bf(nt, MLA_HEADS * V_ROWS, TILE_T),
        bf(nt, MLSTM_WIDTH, TILE_T), bf(t, MLSTM_WIDTH), bf(nt, MLSTM_HEADS * V_ROWS, TILE_T),
        bf(t, MLSTM_WIDTH), jax.ShapeDtypeStruct((GATE_ROWS, t), F32),
        jax.ShapeDtypeStruct((t, LANES), F32),
    ]
    out_specs = [
        tile_t(MLA_HEADS * QK_PAD), row(MLA_HEADS * QK_PAD), tile_t(MLA_HEADS * V_ROWS),
        tile_t(MLSTM_WIDTH), row(MLSTM_WIDTH), tile_t(MLSTM_HEADS * V_ROWS), row(MLSTM_WIDTH),
        pl.BlockSpec((GATE_ROWS, tm), lambda i: (0, i)), row(LANES),
    ]
    consts = (invf, g_pre, w_in_p, w_tail, qn, kvn, w_uq_t, w_uk, w_uv_t, w_mv_t, vones, conv_w,
              conv_b, gbias)
    return pl.pallas_call(
        functools.partial(_in_proj_body, tm=tm, sub=SUB_IN, tiles_per_seq=seq // tm, chunk=CHUNK),
        out_shape=out_shape,
        grid=(t // tm,),
        in_specs=[row(D_MODEL), pl.BlockSpec((1, tm), lambda i: (0, i))]
        + [_const_spec(c.shape) for c in consts],
        out_specs=out_specs,
        scratch_shapes=[pltpu.VMEM((2 * MLSTM_WIDTH // LANES, tm + 2 * SUBLANES, LANES), F32),
                        pltpu.VMEM((2 * MLSTM_WIDTH // LANES, tm, LANES), F32)],
        compiler_params=pltpu.CompilerParams(dimension_semantics=("arbitrary",),
                                             vmem_limit_bytes=VMEM_LIMIT),
        name="in_proj",
    )(x2, pos_row, *consts)


def _attn_body(qt_ref, k_ref, vt_ref, o_ref, m_ref, use_ref, viol_ref, acc_ref, pa_ref, pb_ref,
               ua_ref, ub_ref, *, tq, heads):
    qi = pl.program_id(2)
    hq = [slice(h * QK_PAD, (h + 1) * QK_PAD) for h in range(heads)]
    hv = [slice(h * V_ROWS, (h + 1) * V_ROWS) for h in range(heads)]
    stat = m_ref.shape[1:]

    def qk(ki, h):
        start = pl.multiple_of(ki * tq, tq)
        return _dot(k_ref[pl.ds(start, tq), hq[h]], qt_ref[0, hq[h], :])

    def causal(s):
        r = lax.broadcasted_iota(jnp.int32, s.shape, 0)
        c = lax.broadcasted_iota(jnp.int32, s.shape, 1)
        return jnp.where(r <= c, s, -jnp.inf)

    def fast_scores(ki, p_ref, u_ref):
        for h in range(heads):
            s = qk(ki, h)
            m_use = m_ref[h, 0:1, :]
            p_ref[h] = jnp.exp2(s - m_use).astype(BF16)
            cmax = jnp.max(s, axis=0, keepdims=True)
            excess = jnp.where(qi > 0, cmax - m_use, 0.0)
            u_ref[h] = jnp.broadcast_to(m_use, stat)
            viol_ref[h] = jnp.broadcast_to(jnp.maximum(viol_ref[h, 0:1, :], excess), stat)
            m_ref[h] = jnp.broadcast_to(jnp.maximum(m_use, cmax), stat)

    def fast_pv(ki, p_ref, u_ref):
        for h in range(heads):
            m_use = u_ref[h, 0:1, :]
            alpha = jnp.exp2(use_ref[h, 0:1, :] - m_use)
            acc_ref[h] = alpha * acc_ref[h] + _dot(vt_ref[ki, hv[h], :], p_ref[h])
            use_ref[h] = jnp.broadcast_to(m_use, stat)

    def exact_block(ki, diagonal):
        for h in range(heads):
            s = causal(qk(ki, h)) if diagonal else qk(ki, h)
            m_old = m_ref[h, 0:1, :]
            m_new = jnp.maximum(m_old, jnp.max(s, axis=0, keepdims=True))
            p = jnp.exp2(s - m_new).astype(BF16)
            acc_ref[h] = jnp.exp2(m_old - m_new) * acc_ref[h] + _dot(vt_ref[ki, hv[h], :], p)
            m_ref[h] = jnp.broadcast_to(m_new, stat)

    viol_ref[...] = jnp.zeros(viol_ref.shape, F32)
    s_diag = [causal(qk(qi, h)) for h in range(heads)]
    for h in range(heads):
        cmax = jnp.broadcast_to(jnp.max(s_diag[h], axis=0, keepdims=True), stat)
        m_ref[h] = cmax
        use_ref[h] = cmax
    fast_scores(0, pa_ref, ua_ref)
    for h in range(heads):
        p = jnp.exp2(s_diag[h] - use_ref[h, 0:1, :]).astype(BF16)
        acc_ref[h] = _dot(vt_ref[qi, hv[h], :], p)

    bufs = ((pa_ref, ua_ref), (pb_ref, ub_ref))

    def run(first, count, score_next):
        for i in range(count):
            if i + 1 < count or score_next:
                fast_scores(first + i + 1, *bufs[(i + 1) % 2])
            fast_pv(first + i, *bufs[i % 2])

    def four(j, carry):
        run(4 * j, 4, True)
        return carry

    n_loop = jnp.maximum(qi - 1, 0) // 4
    lax.fori_loop(0, n_loop, four, 0)
    for left in range(1, 5):
        @pl.when(qi - 4 * n_loop == left)
        def _(left=left):
            run(4 * n_loop, left, False)

    @pl.when(jnp.max(viol_ref[...]) > MAX_STALE_EXCESS)
    def _():
        m_ref[...] = jnp.full(m_ref.shape, -jnp.inf, F32)
        acc_ref[...] = jnp.zeros(acc_ref.shape, F32)

        def blk(ki, carry):
            exact_block(ki, False)
            return carry

        lax.fori_loop(0, qi, blk, 0)
        exact_block(qi, True)

    for h in range(heads):
        acc = acc_ref[h]
        out_t = acc[0:V_HEAD_DIM] / acc[V_HEAD_DIM:V_HEAD_DIM + 1]
        o_ref[:, h * V_HEAD_DIM:(h + 1) * V_HEAD_DIM] = out_t.T.astype(o_ref.dtype)


def _attention(qt, k, vt, batch, seq):
    t = k.shape[0]
    tq = TQ
    nq = seq // tq
    heads = HEADS_PER_STEP
    return pl.pallas_call(
        functools.partial(_attn_body, tq=tq, heads=heads),
        out_shape=jax.ShapeDtypeStruct((t, MLA_WIDTH), BF16),
        grid=(batch, MLA_HEADS // heads, nq),
        in_specs=[
            pl.BlockSpec((1, heads * QK_PAD, tq), lambda b, g, i: (b * nq + i, g, 0)),
            pl.BlockSpec((seq, heads * QK_PAD), lambda b, g, i: (b, g)),
            pl.BlockSpec((nq, heads * V_ROWS, tq), lambda b, g, i: (b, g, 0)),
        ],
        out_specs=pl.BlockSpec((tq, heads * V_HEAD_DIM), lambda b, g, i: (b * nq + i, g)),
        scratch_shapes=[pltpu.VMEM((heads, SUBLANES, tq), F32),
                        pltpu.VMEM((heads, SUBLANES, tq), F32),
                        pltpu.VMEM((heads, SUBLANES, tq), F32),
                        pltpu.VMEM((heads, V_ROWS, tq), F32),
                        pltpu.VMEM((heads, tq, tq), BF16),
                        pltpu.VMEM((heads, tq, tq), BF16),
                        pltpu.VMEM((heads, SUBLANES, tq), F32),
                        pltpu.VMEM((heads, SUBLANES, tq), F32)],
        compiler_params=pltpu.CompilerParams(
            dimension_semantics=("arbitrary", "arbitrary", "arbitrary"),
            vmem_limit_bytes=VMEM_LIMIT),
        name="mla_attention",
    )(qt, k, vt)


def _mlstm_body(qt_ref, k_ref, vt_ref, og_ref, grow_ref, keycol_ref, norm_ref, out_ref, ct_ref,
                m_ref, *, chunk, chunks_per_step):
    @pl.when(pl.program_id(1) == 0)
    def _():
        ct_ref[...] = jnp.zeros(ct_ref.shape, F32)
        m_ref[...] = jnp.zeros(m_ref.shape, F32)

    heads = range(MLSTM_HEADS)
    hs = [slice(h * MLSTM_HEAD_DIM, (h + 1) * MLSTM_HEAD_DIM) for h in heads]
    vs = [slice(h * V_ROWS, (h + 1) * V_ROWS) for h in heads]
    tok = [slice(c * chunk, (c + 1) * chunk) for c in range(chunks_per_step)]
    r = lax.broadcasted_iota(jnp.int32, (chunk, chunk), 0)
    c = lax.broadcasted_iota(jnp.int32, (chunk, chunk), 1)
    upper = r <= c

    s_raw = [[_dot(k_ref[t, hs[h]], qt_ref[0, hs[h], t]) for h in heads] for t in tok]
    ct = [ct_ref[h] for h in heads]
    m_prev = [m_ref[h, 0:1, 0:1] for h in heads]
    for ci, t in enumerate(tok):
        qt = [qt_ref[0, hs[h], t] for h in heads]
        vt = [vt_ref[0, vs[h], t] for h in heads]
        cq = [_dot(ct[h].astype(BF16), qt[h]) for h in heads]
        grow = grow_ref[:, t]
        mm = [jnp.maximum(m_prev[h], grow[ROW_KMAX + h:ROW_KMAX + h + 1]) for h in heads]
        b_row = [grow[ROW_B + h:ROW_B + h + 1] for h in heads]
        inter = [jnp.exp(m_prev[h] - mm[h]) for h in heads]
        for h in heads:
            mm_last = mm[h][:, chunk - 1:chunk]
            w_row = jnp.exp(grow[ROW_KEY + h:ROW_KEY + h + 1] - mm_last)
            vtw = (vt[h].astype(F32) * w_row).astype(BF16)
            ct[h] = jnp.exp(m_prev[h] - mm_last) * ct[h] + _dot(vtw, k_ref[t, hs[h]])
            m_prev[h] = b_row[h][:, chunk - 1:chunk] + mm_last
        keycol = keycol_ref[t, :]
        for h in heads:
            p = jnp.exp(jnp.where(upper, keycol[:, h:h + 1] - mm[h], -jnp.inf))
            st = (s_raw[ci][h] * p).astype(BF16)
            tot = _dot(vt[h], st) + inter[h] * cq[h]
            den = tot[MLSTM_HEAD_DIM:MLSTM_HEAD_DIM + 1]
            ht = tot[0:MLSTM_HEAD_DIM] / jnp.maximum(jnp.abs(den), jnp.exp(-(b_row[h] + mm[h])))
            hn_t = ht * lax.rsqrt(jnp.mean(ht * ht, axis=0, keepdims=True) + EPS)
            out_ref[t, hs[h]] = (hn_t.T * norm_ref[0:1, hs[h]]
                                 * og_ref[t, hs[h]].astype(F32)).astype(out_ref.dtype)
    for h in heads:
        ct_ref[h] = ct[h]
        m_ref[h] = jnp.broadcast_to(m_prev[h], m_ref.shape[1:])


def _mlstm(mqt, mk, mvt, mo, grow, keycol, norm, batch, seq):
    t = mk.shape[0]
    tile = TILE_T
    ns = seq // tile
    row = lambda width: pl.BlockSpec((tile, width), lambda b, j: (b * ns + j, 0))
    tile_t = lambda rows: pl.BlockSpec((1, rows, tile), lambda b, j: (b * ns + j, 0, 0))
    return pl.pallas_call(
        functools.partial(_mlstm_body, chunk=CHUNK, chunks_per_step=tile // CHUNK),
        out_shape=jax.ShapeDtypeStruct((t, MLSTM_WIDTH), BF16),
        grid=(batch, ns),
        in_specs=[tile_t(MLSTM_WIDTH), row(MLSTM_WIDTH), tile_t(MLSTM_HEADS * V_ROWS),
                  row(MLSTM_WIDTH), pl.BlockSpec((GATE_ROWS, tile), lambda b, j: (0, b * ns + j)),
                  row(LANES), _const_spec(norm.shape)],
        out_specs=row(MLSTM_WIDTH),
        scratch_shapes=[
            pltpu.VMEM((MLSTM_HEADS, V_ROWS, MLSTM_HEAD_DIM), F32),
            pltpu.VMEM((MLSTM_HEADS, SUBLANES, LANES), F32),
        ],
        compiler_params=pltpu.CompilerParams(dimension_semantics=("arbitrary", "arbitrary"),
                                             vmem_limit_bytes=VMEM_LIMIT),
        name="mlstm",
    )(mqt, mk, mvt, mo, grow, keycol, norm)


def _out_ffn_body(x_ref, mla_ref, mlstm_ref, p_ref, w_out_ref, g_post_ref, g_fpre_ref, g_fpost_ref,
                  w_gate_ref, w_up_ref, w_down_ref, w_pg_ref, w_pp_ref, o_ref):
    tm = x_ref.shape[0]
    subs = [slice(r0, r0 + SUB_OUT) for r0 in range(0, tm, SUB_OUT)]
    mix = [_dot(mla_ref[s, :], w_out_ref[0:MLA_WIDTH, :])
           + _dot(mlstm_ref[s, :], w_out_ref[MLA_WIDTH:MLA_WIDTH + MLSTM_WIDTH, :]) for s in subs]
    h1 = [x_ref[s, :] + _rms(m, g_post_ref[...]) for s, m in zip(subs, mix)]
    f = [_rms(h, g_fpre_ref[...]).astype(BF16) for h in h1]
    ffn = [None] * len(subs)
    off = 0
    for width in FF_CHUNKS:
        cols = slice(off, off + width)
        gate = [_dot(v, w_gate_ref[:, cols]) for v in f]
        up = [_dot(v, w_up_ref[:, cols]) for v in f]
        act = [(g * jax.nn.sigmoid(g) * u).astype(BF16) for g, u in zip(gate, up)]
        part = [_dot(a, w_down_ref[cols, :]) for a in act]
        ffn = [p if acc is None else acc + p for acc, p in zip(ffn, part)]
        off += width
    h2 = [h + _rms(v, g_fpost_ref[...]) for h, v in zip(h1, ffn)]
    pgate = [jax.nn.sigmoid(_dot(h.astype(BF16), w_pg_ref[...])) for h in h2]
    for s, h, g in zip(subs, h2, pgate):
        o_ref[s, :] = h + g * _dot(p_ref[s, :].astype(BF16), w_pp_ref[...])


def _out_ffn(x2, mla, mlstm, p2, w_out, g_post, g_fpre, g_fpost, w_gate, w_up, w_down, w_pg, w_pp):
    t = x2.shape[0]
    tm = TM_OUT
    row = lambda width: pl.BlockSpec((tm, width), lambda i: (i, 0))
    consts = (w_out, g_post, g_fpre, g_fpost, w_gate, w_up, w_down, w_pg, w_pp)
    return pl.pallas_call(
        _out_ffn_body,
        out_shape=jax.ShapeDtypeStruct((t, D_MODEL), F32),
        grid=(t // tm,),
        in_specs=[row(D_MODEL), row(MLA_WIDTH), row(MLSTM_WIDTH), row(PLE_DIM)]
        + [_const_spec(c.shape) for c in consts],
        out_specs=row(D_MODEL),
        compiler_params=pltpu.CompilerParams(dimension_semantics=("arbitrary",),
                                             vmem_limit_bytes=VMEM_LIMIT),
        name="out_ffn",
    )(x2, mla, mlstm, p2, *consts)


def _pack_w_in(w_in):
    o_krope = Q_LORA_RANK + KV_LORA_RANK
    o_mq = o_krope + QK_ROPE_DIM
    o_mv = o_mq + 2 * MLSTM_WIDTH
    o_mo = o_mv + MLSTM_WIDTH
    o_gates = o_mo + MLSTM_WIDTH
    main = jnp.concatenate([w_in[:, :o_krope], w_in[:, o_mq:o_mv], w_in[:, o_mo:o_gates]], axis=1)
    gap = jnp.zeros((D_MODEL, SUBLANES - MLSTM_HEADS), w_in.dtype)
    tail = jnp.concatenate([w_in[:, o_krope:o_mq], w_in[:, o_gates:o_gates + MLSTM_HEADS], gap,
                            w_in[:, o_gates + MLSTM_HEADS:], gap], axis=1).T
    tail = jnp.pad(tail, ((0, LANES - tail.shape[0]), (0, 0)))
    mv_t = w_in[:, o_mv:o_mo].T.reshape(MLSTM_HEADS, MLSTM_HEAD_DIM, D_MODEL)
    mv_t = jnp.pad(mv_t, ((0, 0), (0, V_ROWS - MLSTM_HEAD_DIM), (0, 0)))
    return main.astype(BF16), tail.astype(BF16), mv_t.reshape(MLSTM_HEADS * V_ROWS, D_MODEL).astype(BF16)


def _pack_w_uq_t(w_uq):
    w = w_uq.T.reshape(MLA_HEADS, QK_HEAD_DIM, Q_LORA_RANK)
    w = jnp.pad(w, ((0, 0), (0, QK_PAD - QK_HEAD_DIM), (0, 0)))
    return w.reshape(MLA_HEADS * QK_PAD, Q_LORA_RANK).astype(BF16)


def _pack_w_ukv(w_ukv):
    w = w_ukv.reshape(KV_LORA_RANK, MLA_HEADS, QK_NOPE_DIM + V_HEAD_DIM)
    w_uk = w[:, :, :QK_NOPE_DIM].reshape(KV_LORA_RANK, MLA_HEADS * QK_NOPE_DIM)
    w_uv_t = jnp.transpose(w[:, :, QK_NOPE_DIM:], (1, 2, 0))
    w_uv_t = jnp.pad(w_uv_t, ((0, 0), (0, V_ROWS - V_HEAD_DIM), (0, 0)))
    return w_uk.astype(BF16), w_uv_t.reshape(MLA_HEADS * V_ROWS, KV_LORA_RANK).astype(BF16)


def _layer(h, p_i, pos_row, invf, attn_pre_norm, attn_post_norm, w_in, q_norm, kv_norm, w_uq, w_ukv,
           conv_w, conv_b, gate_bias_i, gate_bias_f, mlstm_norm, w_out, ffn_pre_norm, ffn_post_norm,
           w_gate, w_up, w_down, w_ple_proj, w_ple_gate, batch, seq):
    row = lambda a: a.reshape(1, -1)
    w_in_p, w_tail, w_mv_t = _pack_w_in(w_in)
    w_uk, w_uv_t = _pack_w_ukv(w_ukv)
    gap = jnp.zeros((SUBLANES - MLSTM_HEADS,), F32)
    gbias = jnp.concatenate([gate_bias_i, gap, gate_bias_f, gap]).astype(F32).reshape(2 * SUBLANES, 1)
    vones = jnp.zeros((MLA_HEADS, V_ROWS, 1), F32).at[:, V_HEAD_DIM, 0].set(1.0)
    qt, k, vt, mqt, mk, mvt, mo, grow, keycol = _in_proj(
        h, pos_row, invf, row(attn_pre_norm), w_in_p, w_tail, row(q_norm), row(kv_norm),
        _pack_w_uq_t(w_uq), w_uk, w_uv_t, w_mv_t, vones.reshape(MLA_HEADS * V_ROWS, 1), conv_w,
        row(conv_b), gbias, seq)
    mla = _attention(qt, k, vt, batch, seq)
    mlstm = _mlstm(mqt, mk, mvt, mo, grow, keycol, row(mlstm_norm), batch, seq)
    return _out_ffn(h, mla, mlstm, p_i, w_out.astype(BF16), row(attn_post_norm), row(ffn_pre_norm),
                    row(ffn_post_norm), w_gate.astype(BF16), w_up.astype(BF16),
                    w_down.astype(BF16), w_ple_gate.astype(BF16), w_ple_proj.astype(BF16))


@jax.jit
def kernel(x, p, positions, attn_pre_norm, attn_post_norm, w_in, q_norm, kv_norm, w_uq, w_ukv,
           conv_w, conv_b, gate_bias_i, gate_bias_f, mlstm_norm, w_out, ffn_pre_norm, ffn_post_norm,
           w_gate, w_up, w_down, w_ple_proj, w_ple_gate):
    batch, seq, _ = x.shape
    t = batch * seq
    depth = p.shape[0]
    inv_freq = ROPE_THETA ** (-jnp.arange(0, QK_ROPE_DIM, 2, dtype=F32) / QK_ROPE_DIM)
    invf = inv_freq.reshape(ROPE_HALF, 1)
    pos_row = positions.astype(F32).reshape(1, t)
    h = x.reshape(t, D_MODEL)
    for i in range(depth):
        h = _layer(h, p[i].reshape(t, PLE_DIM), pos_row, invf, attn_pre_norm[i], attn_post_norm[i],
                   w_in[i], q_norm[i], kv_norm[i], w_uq[i], w_ukv[i], conv_w[i], conv_b[i],
                   gate_bias_i[i], gate_bias_f[i], mlstm_norm[i], w_out[i], ffn_pre_norm[i],
                   ffn_post_norm[i], w_gate[i], w_up[i], w_down[i], w_ple_proj[i], w_ple_gate[i],
                   batch, seq)
    return h.reshape(batch, seq, D_MODEL)
```

```python
import functools
import math

import jax
import jax.numpy as jnp
from jax import lax
from jax.experimental import pallas as pl
from jax.experimental.pallas import tpu as pltpu

F32 = jnp.float32
BF16 = jnp.bfloat16

D_MODEL = 1024
PLE_DIM = 256
MLA_HEADS = 4
QK_NOPE_DIM = 128
QK_ROPE_DIM = 64
QK_HEAD_DIM = QK_NOPE_DIM + QK_ROPE_DIM
V_HEAD_DIM = 128
Q_LORA_RANK = 256
KV_LORA_RANK = 128
ROPE_THETA = 10000.0
MLA_WIDTH = MLA_HEADS * V_HEAD_DIM
MLSTM_HEADS = 4
MLSTM_HEAD_DIM = 128
MLSTM_WIDTH = MLSTM_HEADS * MLSTM_HEAD_DIM
CONV_WIDTH = 4
CONV_STRIDE = 4
D_FF = 2816
EPS = 1e-6

LANES = 128
SUBLANES = 8
BF16_ROWS = 16
QK_PAD = 2 * LANES
V_ROWS = V_HEAD_DIM + BF16_ROWS
ROPE_HALF = QK_ROPE_DIM // 2
VMEM_LIMIT = 56 * 1024 * 1024

C_Q = 0
C_KV = C_Q + Q_LORA_RANK
C_MQK = C_KV + KV_LORA_RANK
C_MO = C_MQK + 2 * MLSTM_WIDTH
D_IN_PACKED = C_MO + MLSTM_WIDTH
GATE_I = QK_ROPE_DIM
GATE_F = GATE_I + SUBLANES
ROW_KEY, ROW_KMAX, ROW_B, GATE_ROWS = 0, SUBLANES, 2 * SUBLANES, 3 * SUBLANES

TILE_T = 512
TM_IN = 512
SUB_IN = 256
TQ = TILE_T
HEADS_PER_STEP = 2
MAX_STALE_EXCESS = 64.0
CHUNK = 256
TM_OUT = 512
SUB_OUT = 256
FF_CHUNKS = (1024, 1024, 768)


def _rms(x, g):
    return x * lax.rsqrt(jnp.mean(x * x, axis=-1, keepdims=True) + EPS) * g


def _dot(a, b):
    return jnp.dot(a, b, preferred_element_type=F32)


def _dot_nt(a, b):
    return lax.dot_general(a, b, (((1,), (1,)), ((), ())), preferred_element_type=F32)


def _const_spec(shape):
    return pl.BlockSpec(shape, lambda *_: (0,) * len(shape), pipeline_mode=pl.Buffered(1))


def _in_proj_body(x_ref, pos_ref, invf_ref, g_pre_ref, w_in_ref, w_tail_ref, qn_ref, kvn_ref,
                  w_uq_ref, w_uk_ref, w_uv_ref, w_mv_ref, vones_ref, conv_w_ref, conv_b_ref, gbias_ref,
                  qt_out, k_out, vt_out, mqt_out, mk_out, mvt_out, mo_out, grow_out, keycol_out,
                  ext_ref, stage_ref, *, tm, sub, tiles_per_seq, chunk):
    assert tm % sub == 0 and sub % chunk == 0
    i = pl.program_id(0)
    halo = SUBLANES

    @pl.when(lax.rem(i, tiles_per_seq) == 0)
    def _():
        ext_ref[:, 0:halo, :] = jnp.zeros((ext_ref.shape[0], halo, LANES), F32)

    for r0 in range(0, tm, sub):
        _in_proj_rows(r0, sub, halo, chunk, x_ref, pos_ref, invf_ref, g_pre_ref, w_in_ref, w_tail_ref,
                      qn_ref, kvn_ref, w_uq_ref, w_uk_ref, w_uv_ref, w_mv_ref, vones_ref, conv_w_ref,
                      conv_b_ref, gbias_ref, qt_out, k_out, vt_out, mqt_out, mk_out, mvt_out, mo_out,
                      grow_out, keycol_out, ext_ref, stage_ref)
    ext_ref[:, 0:halo, :] = ext_ref[:, tm:tm + halo, :]


def _in_proj_rows(r0, ts, halo, chunk, x_ref, pos_ref, invf_ref, g_pre_ref, w_in_ref, w_tail_ref,
                  qn_ref, kvn_ref, w_uq_ref, w_uk_ref, w_uv_ref, w_mv_ref, vones_ref, conv_w_ref,
                  conv_b_ref, gbias_ref, qt_out, k_out, vt_out, mqt_out, mk_out, mvt_out, mo_out,
                  grow_out, keycol_out, ext_ref, stage_ref):
    rows = slice(r0, r0 + ts)
    tile, lanes = r0 // TILE_T, slice(r0 % TILE_T, r0 % TILE_T + ts)
    u = _rms(x_ref[rows, :], g_pre_ref[...]).astype(BF16)

    cq_raw = _dot(u, w_in_ref[:, C_Q:C_Q + Q_LORA_RANK])
    ckv_raw = _dot(u, w_in_ref[:, C_KV:C_KV + KV_LORA_RANK])
    tail_t = _dot_nt(w_tail_ref[...], u)
    z = _dot(u, w_in_ref[:, C_MQK:C_MQK + 2 * MLSTM_WIDTH])
    for c in range(2 * MLSTM_WIDTH // LANES):
        ext_ref[c, halo + r0:halo + r0 + ts, :] = z[:, c * LANES:(c + 1) * LANES]
    mvt_out[tile, :, lanes] = (_dot_nt(w_mv_ref[...], u) + vones_ref[...]).astype(BF16)
    mo_out[rows, :] = jax.nn.sigmoid(_dot(u, w_in_ref[:, C_MO:C_MO + MLSTM_WIDTH])).astype(BF16)
    cq = _rms(cq_raw, qn_ref[...]).astype(BF16)
    ckv = _rms(ckv_raw, kvn_ref[...]).astype(BF16)
    qt = _dot_nt(w_uq_ref[...], cq)
    kn = _dot(ckv, w_uk_ref[...])
    vt_out[tile, :, lanes] = (_dot_nt(w_uv_ref[...], ckv) + vones_ref[...]).astype(BF16)

    ang = invf_ref[...] * pos_ref[:, rows]
    cos = jnp.cos(ang)
    sin = jnp.sin(ang)

    def rope_t(t1, t2):
        return t1 * cos - t2 * sin, t2 * cos + t1 * sin

    scale = QK_HEAD_DIM ** -0.5 * math.log2(math.e)
    for h in range(MLA_HEADS):
        o = h * QK_PAD
        pe = o + QK_NOPE_DIM
        r1, r2 = rope_t(qt[pe:pe + ROPE_HALF], qt[pe + ROPE_HALF:pe + QK_ROPE_DIM])
        qt_out[tile, o:pe, lanes] = (qt[o:pe] * scale).astype(BF16)
        qt_out[tile, pe:pe + ROPE_HALF, lanes] = (r1 * scale).astype(BF16)
        qt_out[tile, pe + ROPE_HALF:pe + QK_ROPE_DIM, lanes] = (r2 * scale).astype(BF16)
        qt_out[tile, pe + QK_ROPE_DIM:o + QK_PAD, lanes] = jnp.zeros((QK_PAD - QK_HEAD_DIM, ts), BF16)

    r1, r2 = rope_t(tail_t[0:ROPE_HALF], tail_t[ROPE_HALF:QK_ROPE_DIM])
    kpe_t = jnp.concatenate([r1, r2, jnp.zeros((LANES - QK_ROPE_DIM, ts), F32)], axis=0)
    kpe = kpe_t.T.astype(BF16)
    for h in range(MLA_HEADS):
        o = h * QK_PAD
        k_out[rows, o:o + LANES] = kn[:, h * LANES:(h + 1) * LANES].astype(BF16)
        k_out[rows, o + LANES:o + QK_PAD] = kpe

    gates = tail_t[GATE_I:GATE_I + 2 * SUBLANES] + gbias_ref[...]
    pos_in_chunk = lax.rem(r0 + lax.broadcasted_iota(jnp.int32, (SUBLANES, ts), 1), chunk)

    def chunk_scan(v, op, identity):
        shift = 1
        while shift < chunk:
            v = op(v, jnp.where(pos_in_chunk >= shift, pltpu.roll(v, shift, 1), identity))
            shift *= 2
        return v

    b8 = chunk_scan(jax.nn.log_sigmoid(gates[SUBLANES:]), jnp.add, 0.0)
    key8 = gates[:SUBLANES] - b8
    grow_out[:, rows] = jnp.concatenate([key8, chunk_scan(key8, jnp.maximum, -jnp.inf), b8], axis=0)
    keycol_out[rows, :] = jnp.concatenate([key8, jnp.zeros((LANES - SUBLANES, ts), F32)], axis=0).T

    n = ts // CONV_STRIDE
    for c in range(2 * MLSTM_WIDTH // LANES):
        cols = slice(c * LANES, (c + 1) * LANES)
        taps = {q: ext_ref[c, pl.ds(halo + r0 + q, n, stride=CONV_STRIDE), :]
                for q in range(1 - CONV_WIDTH, CONV_STRIDE)}
        for r in range(CONV_STRIDE):
            acc = conv_b_ref[:, cols]
            for j in range(CONV_WIDTH):
                acc = acc + conv_w_ref[j:j + 1, cols] * taps[r - (CONV_WIDTH - 1) + j]
            stage_ref[c, pl.ds(r0 + r, n, stride=CONV_STRIDE), :] = acc * jax.nn.sigmoid(acc)
    for c in range(MLSTM_WIDTH // LANES):
        cols = slice(c * LANES, (c + 1) * LANES)
        mqt_out[tile, cols, lanes] = stage_ref[c, rows, :].T.astype(BF16)
        k_slab = stage_ref[MLSTM_WIDTH // LANES + c, rows, :]
        mk_out[rows, cols] = (k_slab * (MLSTM_HEAD_DIM ** -0.5)).astype(BF16)


def _in_proj(x2, pos_row, invf, g_pre, w_in_p, w_tail, qn, kvn, w_uq_t, w_uk, w_uv_t, w_mv_t, vones,
             conv_w, conv_b, gbias, seq):
    t = x2.shape[0]
    tm = TM_IN
    nt = t // TILE_T
    row = lambda width: pl.BlockSpec((tm, width), lambda i: (i, 0))
    tile_t = lambda rows: pl.BlockSpec((tm // TILE_T, rows, TILE_T), lambda i: (i, 0, 0))
    bf = lambda *shape: jax.ShapeDtypeStruct(shape, BF16)
    out_shape = [
        bf(nt, MLA_HEADS * QK_PAD, TILE_T), bf(t, MLA_HEADS * QK_PAD), bf(nt, MLA_HEADS * V_ROWS, TILE_T),
        bf(nt, MLSTM_WIDTH, TILE_T), bf(t, MLSTM_WIDTH), bf(nt, MLSTM_HEADS * V_ROWS, TILE_T),
        bf(t, MLSTM_WIDTH), jax.ShapeDtypeStruct((GATE_ROWS, t), F32),
        jax.ShapeDtypeStruct((t, LANES), F32),
    ]
    out_specs = [
        tile_t(MLA_HEADS * QK_PAD), row(MLA_HEADS * QK_PAD), tile_t(MLA_HEADS * V_ROWS),
        tile_t(MLSTM_WIDTH), row(MLSTM_WIDTH), tile_t(MLSTM_HEADS * V_ROWS), row(MLSTM_WIDTH),
        pl.BlockSpec((GATE_ROWS, tm), lambda i: (0, i)), row(LANES),
    ]
    consts = (invf, g_pre, w_in_p, w_tail, qn, kvn, w_uq_t, w_uk, w_uv_t, w_mv_t, vones, conv_w,
              conv_b, gbias)
    return pl.pallas_call(
        functools.partial(_in_proj_body, tm=tm, sub=SUB_IN, tiles_per_seq=seq // tm, chunk=CHUNK),
        out_shape=out_shape,
        grid=(t // tm,),
        in_specs=[row(D_MODEL), pl.BlockSpec((1, tm), lambda i: (0, i))]
        + [_const_spec(c.shape) for c in consts],
        out_specs=out_specs,
        scratch_shapes=[pltpu.VMEM((2 * MLSTM_WIDTH // LANES, tm + 2 * SUBLANES, LANES), F32),
                        pltpu.VMEM((2 * MLSTM_WIDTH // LANES, tm, LANES), F32)],
        compiler_params=pltpu.CompilerParams(dimension_semantics=("arbitrary",),
                                             vmem_limit_bytes=VMEM_LIMIT),
        name="in_proj",
    )(x2, pos_row, *consts)


def _attn_body(qt_ref, k_ref, vt_ref, o_ref, m_ref, use_ref, viol_ref, acc_ref, pa_ref, pb_ref,
               ua_ref, ub_ref, *, tq, heads):
    qi = pl.program_id(2)
    hq = [slice(h * QK_PAD, (h + 1) * QK_PAD) for h in range(heads)]
    hv = [slice(h * V_ROWS, (h + 1) * V_ROWS) for h in range(heads)]
    stat = m_ref.shape[1:]

    def qk(ki, h):
        start = pl.multiple_of(ki * tq, tq)
        return _dot(k_ref[pl.ds(start, tq), hq[h]], qt_ref[0, hq[h], :])

    def causal(s):
        r = lax.broadcasted_iota(jnp.int32, s.shape, 0)
        c = lax.broadcasted_iota(jnp.int32, s.shape, 1)
        return jnp.where(r <= c, s, -jnp.inf)

    def fast_scores(ki, p_ref, u_ref):
        for h in range(heads):
            s = qk(ki, h)
            m_use = m_ref[h, 0:1, :]
            p_ref[h] = jnp.exp2(s - m_use).astype(BF16)
            cmax = jnp.max(s, axis=0, keepdims=True)
            excess = jnp.where(qi > 0, cmax - m_use, 0.0)
            u_ref[h] = jnp.broadcast_to(m_use, stat)
            viol_ref[h] = jnp.broadcast_to(jnp.maximum(viol_ref[h, 0:1, :], excess), stat)
            m_ref[h] = jnp.broadcast_to(jnp.maximum(m_use, cmax), stat)

    def fast_pv(ki, p_ref, u_ref):
        for h in range(heads):
            m_use = u_ref[h, 0:1, :]
            alpha = jnp.exp2(use_ref[h, 0:1, :] - m_use)
            acc_ref[h] = alpha * acc_ref[h] + _dot(vt_ref[ki, hv[h], :], p_ref[h])
            use_ref[h] = jnp.broadcast_to(m_use, stat)

    def exact_block(ki, diagonal):
        for h in range(heads):
            s = causal(qk(ki, h)) if diagonal else qk(ki, h)
            m_old = m_ref[h, 0:1, :]
            m_new = jnp.maximum(m_old, jnp.max(s, axis=0, keepdims=True))
            p = jnp.exp2(s - m_new).astype(BF16)
            acc_ref[h] = jnp.exp2(m_old - m_new) * acc_ref[h] + _dot(vt_ref[ki, hv[h], :], p)
            m_ref[h] = jnp.broadcast_to(m_new, stat)

    def start():
        viol_ref[...] = jnp.zeros(viol_ref.shape, F32)
        s_diag = [causal(qk(qi, h)) for h in range(heads)]
        for h in range(heads):
            cmax = jnp.broadcast_to(jnp.max(s_diag[h], axis=0, keepdims=True), stat)
            m_ref[h] = cmax
            use_ref[h] = cmax
        fast_scores(0, pa_ref, ua_ref)
        for h in range(heads):
            p = jnp.exp2(s_diag[h] - use_ref[h, 0:1, :]).astype(BF16)
            acc_ref[h] = _dot(vt_ref[qi, hv[h], :], p)

    bufs = ((pa_ref, ua_ref), (pb_ref, ub_ref))

    def run(first, count, score_next):
        for i in range(count):
            if i + 1 < count or score_next:
                fast_scores(first + i + 1, *bufs[(i + 1) % 2])
            fast_pv(first + i, *bufs[i % 2])

    def four(j, carry):
        run(4 * j, 4, True)
        return carry

    for short in range(5):
        @pl.when(qi == short)
        def _(short=short):
            start()
            run(0, short, False)

    @pl.when(qi > 4)
    def _():
        start()
        run(0, 4, True)

    n_loop = jnp.maximum(qi - 1, 0) // 4
    lax.fori_loop(1, n_loop, four, 0)
    for left in range(1, 5):
        @pl.when(jnp.logical_and(qi > 4, qi - 4 * n_loop == left))
        def _(left=left):
            run(4 * n_loop, left, False)

    @pl.when(jnp.max(viol_ref[...]) > MAX_STALE_EXCESS)
    def _():
        m_ref[...] = jnp.full(m_ref.shape, -jnp.inf, F32)
        acc_ref[...] = jnp.zeros(acc_ref.shape, F32)

        def blk(ki, carry):
            exact_block(ki, False)
            return carry

        lax.fori_loop(0, qi, blk, 0)
        exact_block(qi, True)

    for h in range(heads):
        acc = acc_ref[h]
        out_t = acc[0:V_HEAD_DIM] / acc[V_HEAD_DIM:V_HEAD_DIM + 1]
        o_ref[:, h * V_HEAD_DIM:(h + 1) * V_HEAD_DIM] = out_t.T.astype(o_ref.dtype)


def _attention(qt, k, vt, batch, seq):
    t = k.shape[0]
    tq = TQ
    nq = seq // tq
    heads = HEADS_PER_STEP
    return pl.pallas_call(
        functools.partial(_attn_body, tq=tq, heads=heads),
        out_shape=jax.ShapeDtypeStruct((t, MLA_WIDTH), BF16),
        grid=(batch, MLA_HEADS // heads, nq),
        in_specs=[
            pl.BlockSpec((1, heads * QK_PAD, tq), lambda b, g, i: (b * nq + i, g, 0)),
            pl.BlockSpec((seq, heads * QK_PAD), lambda b, g, i: (b, g)),
            pl.BlockSpec((nq, heads * V_ROWS, tq), lambda b, g, i: (b, g, 0)),
        ],
        out_specs=pl.BlockSpec((tq, heads * V_HEAD_DIM), lambda b, g, i: (b * nq + i, g)),
        scratch_shapes=[pltpu.VMEM((heads, SUBLANES, tq), F32),
                        pltpu.VMEM((heads, SUBLANES, tq), F32),
                        pltpu.VMEM((heads, SUBLANES, tq), F32),
                        pltpu.VMEM((heads, V_ROWS, tq), F32),
                        pltpu.VMEM((heads, tq, tq), BF16),
                        pltpu.VMEM((heads, tq, tq), BF16),
                        pltpu.VMEM((heads, SUBLANES, tq), F32),
                        pltpu.VMEM((heads, SUBLANES, tq), F32)],
        compiler_params=pltpu.CompilerParams(
            dimension_semantics=("arbitrary", "arbitrary", "arbitrary"),
            vmem_limit_bytes=VMEM_LIMIT),
        name="mla_attention",
    )(qt, k, vt)


def _mlstm_body(qt_ref, k_ref, vt_ref, og_ref, grow_ref, keycol_ref, norm_ref, out_ref, ct_ref,
                m_ref, *, chunk, chunks_per_step):
    @pl.when(pl.program_id(1) == 0)
    def _():
        ct_ref[...] = jnp.zeros(ct_ref.shape, F32)
        m_ref[...] = jnp.zeros(m_ref.shape, F32)

    heads = range(MLSTM_HEADS)
    hs = [slice(h * MLSTM_HEAD_DIM, (h + 1) * MLSTM_HEAD_DIM) for h in heads]
    vs = [slice(h * V_ROWS, (h + 1) * V_ROWS) for h in heads]
    tok = [slice(c * chunk, (c + 1) * chunk) for c in range(chunks_per_step)]
    r = lax.broadcasted_iota(jnp.int32, (chunk, chunk), 0)
    c = lax.broadcasted_iota(jnp.int32, (chunk, chunk), 1)
    upper = r <= c

    s_raw = [[_dot(k_ref[t, hs[h]], qt_ref[0, hs[h], t]) for h in heads] for t in tok]
    ct = [ct_ref[h] for h in heads]
    m_prev = [m_ref[h, 0:1, 0:1] for h in heads]
    for ci, t in enumerate(tok):
        qt = [qt_ref[0, hs[h], t] for h in heads]
        vt = [vt_ref[0, vs[h], t] for h in heads]
        cq = [_dot(ct[h].astype(BF16), qt[h]) for h in heads]
        grow = grow_ref[:, t]
        mm = [jnp.maximum(m_prev[h], grow[ROW_KMAX + h:ROW_KMAX + h + 1]) for h in heads]
        b_row = [grow[ROW_B + h:ROW_B + h + 1] for h in heads]
        inter = [jnp.exp(m_prev[h] - mm[h]) for h in heads]
        for h in heads:
            mm_last = mm[h][:, chunk - 1:chunk]
            w_row = jnp.exp(grow[ROW_KEY + h:ROW_KEY + h + 1] - mm_last)
            vtw = (vt[h].astype(F32) * w_row).astype(BF16)
            ct[h] = jnp.exp(m_prev[h] - mm_last) * ct[h] + _dot(vtw, k_ref[t, hs[h]])
            m_prev[h] = b_row[h][:, chunk - 1:chunk] + mm_last
        keycol = keycol_ref[t, :]
        for h in heads:
            p = jnp.exp(jnp.where(upper, keycol[:, h:h + 1] - mm[h], -jnp.inf))
            st = (s_raw[ci][h] * p).astype(BF16)
            tot = _dot(vt[h], st) + inter[h] * cq[h]
            den = tot[MLSTM_HEAD_DIM:MLSTM_HEAD_DIM + 1]
            ht = tot[0:MLSTM_HEAD_DIM] / jnp.maximum(jnp.abs(den), jnp.exp(-(b_row[h] + mm[h])))
            hn_t = ht * lax.rsqrt(jnp.mean(ht * ht, axis=0, keepdims=True) + EPS)
            out_ref[t, hs[h]] = (hn_t.T * norm_ref[0:1, hs[h]]
                                 * og_ref[t, hs[h]].astype(F32)).astype(out_ref.dtype)
    for h in heads:
        ct_ref[h] = ct[h]
        m_ref[h] = jnp.broadcast_to(m_prev[h], m_ref.shape[1:])


def _mlstm(mqt, mk, mvt, mo, grow, keycol, norm, batch, seq):
    t = mk.shape[0]
    tile = TILE_T
    ns = seq // tile
    row = lambda width: pl.BlockSpec((tile, width), lambda b, j: (b * ns + j, 0))
    tile_t = lambda rows: pl.BlockSpec((1, rows, tile), lambda b, j: (b * ns + j, 0, 0))
    return pl.pallas_call(
        functools.partial(_mlstm_body, chunk=CHUNK, chunks_per_step=tile // CHUNK),
        out_shape=jax.ShapeDtypeStruct((t, MLSTM_WIDTH), BF16),
        grid=(batch, ns),
        in_specs=[tile_t(MLSTM_WIDTH), row(MLSTM_WIDTH), tile_t(MLSTM_HEADS * V_ROWS),
                  row(MLSTM_WIDTH), pl.BlockSpec((GATE_ROWS, tile), lambda b, j: (0, b * ns + j)),
                  row(LANES), _const_spec(norm.shape)],
        out_specs=row(MLSTM_WIDTH),
        scratch_shapes=[
            pltpu.VMEM((MLSTM_HEADS, V_ROWS, MLSTM_HEAD_DIM), F32),
            pltpu.VMEM((MLSTM_HEADS, SUBLANES, LANES), F32),
        ],
        compiler_params=pltpu.CompilerParams(dimension_semantics=("arbitrary", "arbitrary"),
                                             vmem_limit_bytes=VMEM_LIMIT),
        name="mlstm",
    )(mqt, mk, mvt, mo, grow, keycol, norm)


def _out_ffn_body(x_ref, mla_ref, mlstm_ref, p_ref, w_out_ref, g_post_ref, g_fpre_ref, g_fpost_ref,
                  w_gate_ref, w_up_ref, w_down_ref, w_pg_ref, w_pp_ref, o_ref):
    tm = x_ref.shape[0]
    subs = [slice(r0, r0 + SUB_OUT) for r0 in range(0, tm, SUB_OUT)]
    mix = [_dot(mla_ref[s, :], w_out_ref[0:MLA_WIDTH, :])
           + _dot(mlstm_ref[s, :], w_out_ref[MLA_WIDTH:MLA_WIDTH + MLSTM_WIDTH, :]) for s in subs]
    h1 = [x_ref[s, :] + _rms(m, g_post_ref[...]) for s, m in zip(subs, mix)]
    f = [_rms(h, g_fpre_ref[...]).astype(BF16) for h in h1]
    ffn = [None] * len(subs)
    off = 0
    for width in FF_CHUNKS:
        cols = slice(off, off + width)
        gate = [_dot(v, w_gate_ref[:, cols]) for v in f]
        up = [_dot(v, w_up_ref[:, cols]) for v in f]
        act = [(g * jax.nn.sigmoid(g) * u).astype(BF16) for g, u in zip(gate, up)]
        part = [_dot(a, w_down_ref[cols, :]) for a in act]
        ffn = [p if acc is None else acc + p for acc, p in zip(ffn, part)]
        off += width
    h2 = [h + _rms(v, g_fpost_ref[...]) for h, v in zip(h1, ffn)]
    pgate = [jax.nn.sigmoid(_dot(h.astype(BF16), w_pg_ref[...])) for h in h2]
    for s, h, g in zip(subs, h2, pgate):
        o_ref[s, :] = h + g * _dot(p_ref[s, :].astype(BF16), w_pp_ref[...])


def _out_ffn(x2, mla, mlstm, p2, w_out, g_post, g_fpre, g_fpost, w_gate, w_up, w_down, w_pg, w_pp):
    t = x2.shape[0]
    tm = TM_OUT
    row = lambda width: pl.BlockSpec((tm, width), lambda i: (i, 0))
    consts = (w_out, g_post, g_fpre, g_fpost, w_gate, w_up, w_down, w_pg, w_pp)
    return pl.pallas_call(
        _out_ffn_body,
        out_shape=jax.ShapeDtypeStruct((t, D_MODEL), F32),
        grid=(t // tm,),
        in_specs=[row(D_MODEL), row(MLA_WIDTH), row(MLSTM_WIDTH), row(PLE_DIM)]
        + [_const_spec(c.shape) for c in consts],
        out_specs=row(D_MODEL),
        compiler_params=pltpu.CompilerParams(dimension_semantics=("arbitrary",),
                                             vmem_limit_bytes=VMEM_LIMIT),
        name="out_ffn",
    )(x2, mla, mlstm, p2, *consts)


def _pack_w_in(w_in):
    o_krope = Q_LORA_RANK + KV_LORA_RANK
    o_mq = o_krope + QK_ROPE_DIM
    o_mv = o_mq + 2 * MLSTM_WIDTH
    o_mo = o_mv + MLSTM_WIDTH
    o_gates = o_mo + MLSTM_WIDTH
    main = jnp.concatenate([w_in[:, :o_krope], w_in[:, o_mq:o_mv], w_in[:, o_mo:o_gates]], axis=1)
    gap = jnp.zeros((D_MODEL, SUBLANES - MLSTM_HEADS), w_in.dtype)
    tail = jnp.concatenate([w_in[:, o_krope:o_mq], w_in[:, o_gates:o_gates + MLSTM_HEADS], gap,
                            w_in[:, o_gates + MLSTM_HEADS:], gap], axis=1).T
    tail = jnp.pad(tail, ((0, LANES - tail.shape[0]), (0, 0)))
    mv_t = w_in[:, o_mv:o_mo].T.reshape(MLSTM_HEADS, MLSTM_HEAD_DIM, D_MODEL)
    mv_t = jnp.pad(mv_t, ((0, 0), (0, V_ROWS - MLSTM_HEAD_DIM), (0, 0)))
    return main.astype(BF16), tail.astype(BF16), mv_t.reshape(MLSTM_HEADS * V_ROWS, D_MODEL).astype(BF16)


def _pack_w_uq_t(w_uq):
    w = w_uq.T.reshape(MLA_HEADS, QK_HEAD_DIM, Q_LORA_RANK)
    w = jnp.pad(w, ((0, 0), (0, QK_PAD - QK_HEAD_DIM), (0, 0)))
    return w.reshape(MLA_HEADS * QK_PAD, Q_LORA_RANK).astype(BF16)


def _pack_w_ukv(w_ukv):
    w = w_ukv.reshape(KV_LORA_RANK, MLA_HEADS, QK_NOPE_DIM + V_HEAD_DIM)
    w_uk = w[:, :, :QK_NOPE_DIM].reshape(KV_LORA_RANK, MLA_HEADS * QK_NOPE_DIM)
    w_uv_t = jnp.transpose(w[:, :, QK_NOPE_DIM:], (1, 2, 0))
    w_uv_t = jnp.pad(w_uv_t, ((0, 0), (0, V_ROWS - V_HEAD_DIM), (0, 0)))
    return w_uk.astype(BF16), w_uv_t.reshape(MLA_HEADS * V_ROWS, KV_LORA_RANK).astype(BF16)


def _layer(h, p_i, pos_row, invf, attn_pre_norm, attn_post_norm, w_in, q_norm, kv_norm, w_uq, w_ukv,
           conv_w, conv_b, gate_bias_i, gate_bias_f, mlstm_norm, w_out, ffn_pre_norm, ffn_post_norm,
           w_gate, w_up, w_down, w_ple_proj, w_ple_gate, batch, seq):
    row = lambda a: a.reshape(1, -1)
    w_in_p, w_tail, w_mv_t = _pack_w_in(w_in)
    w_uk, w_uv_t = _pack_w_ukv(w_ukv)
    gap = jnp.zeros((SUBLANES - MLSTM_HEADS,), F32)
    gbias = jnp.concatenate([gate_bias_i, gap, gate_bias_f, gap]).astype(F32).reshape(2 * SUBLANES, 1)
    vones = jnp.zeros((MLA_HEADS, V_ROWS, 1), F32).at[:, V_HEAD_DIM, 0].set(1.0)
    qt, k, vt, mqt, mk, mvt, mo, grow, keycol = _in_proj(
        h, pos_row, invf, row(attn_pre_norm), w_in_p, w_tail, row(q_norm), row(kv_norm),
        _pack_w_uq_t(w_uq), w_uk, w_uv_t, w_mv_t, vones.reshape(MLA_HEADS * V_ROWS, 1), conv_w,
        row(conv_b), gbias, seq)
    mla = _attention(qt, k, vt, batch, seq)
    mlstm = _mlstm(mqt, mk, mvt, mo, grow, keycol, row(mlstm_norm), batch, seq)
    return _out_ffn(h, mla, mlstm, p_i, w_out.astype(BF16), row(attn_post_norm), row(ffn_pre_norm),
                    row(ffn_post_norm), w_gate.astype(BF16), w_up.astype(BF16),
                    w_down.astype(BF16), w_ple_gate.astype(BF16), w_ple_proj.astype(BF16))


@jax.jit
def kernel(x, p, positions, attn_pre_norm, attn_post_norm, w_in, q_norm, kv_norm, w_uq, w_ukv,
           conv_w, conv_b, gate_bias_i, gate_bias_f, mlstm_norm, w_out, ffn_pre_norm, ffn_post_norm,
           w_gate, w_up, w_down, w_ple_proj, w_ple_gate):
    batch, seq, _ = x.shape
    t = batch * seq
    depth = p.shape[0]
    inv_freq = ROPE_THETA ** (-jnp.arange(0, QK_ROPE_DIM, 2, dtype=F32) / QK_ROPE_DIM)
    invf = inv_freq.reshape(ROPE_HALF, 1)
    pos_row = positions.astype(F32).reshape(1, t)
    h = x.reshape(t, D_MODEL)
    for i in range(depth):
        h = _layer(h, p[i].reshape(t, PLE_DIM), pos_row, invf, attn_pre_norm[i], attn_post_norm[i],
                   w_in[i], q_norm[i], kv_norm[i], w_uq[i], w_ukv[i], conv_w[i], conv_b[i],
                   gate_bias_i[i], gate_bias_f[i], mlstm_norm[i], w_out[i], ffn_pre_norm[i],
                   ffn_post_norm[i], w_gate[i], w_up[i], w_down[i], w_ple_proj[i], w_ple_gate[i],
                   batch, seq)
    return h.reshape(batch, seq, D_MODEL)
```

```python
import functools
import math

import jax
import jax.numpy as jnp
from jax import lax
from jax.experimental import pallas as pl
from jax.experimental.pallas import tpu as pltpu

F32 = jnp.float32
BF16 = jnp.bfloat16

D_MODEL = 1024
PLE_DIM = 256
MLA_HEADS = 4
QK_NOPE_DIM = 128
QK_ROPE_DIM = 64
QK_HEAD_DIM = QK_NOPE_DIM + QK_ROPE_DIM
V_HEAD_DIM = 128
Q_LORA_RANK = 256
KV_LORA_RANK = 128
ROPE_THETA = 10000.0
MLA_WIDTH = MLA_HEADS * V_HEAD_DIM
MLSTM_HEADS = 4
MLSTM_HEAD_DIM = 128
MLSTM_WIDTH = MLSTM_HEADS * MLSTM_HEAD_DIM
CONV_WIDTH = 4
CONV_STRIDE = 4
D_FF = 2816
EPS = 1e-6

LANES = 128
SUBLANES = 8
BF16_ROWS = 16
QK_PAD = 2 * LANES
V_ROWS = V_HEAD_DIM + BF16_ROWS
ROPE_HALF = QK_ROPE_DIM // 2
VMEM_LIMIT = 56 * 1024 * 1024

C_Q = 0
C_KV = C_Q + Q_LORA_RANK
C_MQK = C_KV + KV_LORA_RANK
C_MO = C_MQK + 2 * MLSTM_WIDTH
D_IN_PACKED = C_MO + MLSTM_WIDTH
GATE_I = QK_ROPE_DIM
GATE_F = GATE_I + SUBLANES
ROW_KEY, ROW_KMAX, ROW_B, GATE_ROWS = 0, SUBLANES, 2 * SUBLANES, 3 * SUBLANES

TILE_T = 512
TM_IN = 512
SUB_IN = 256
TQ = TILE_T
HEADS_PER_STEP = 2
MAX_STALE_EXCESS = 64.0
CHUNK = 256
MLSTM_TILES = 2
TM_OUT = 512
SUB_OUT = 256
FF_CHUNKS = (1024, 1024, 768)


def _rms(x, g):
    return x * lax.rsqrt(jnp.mean(x * x, axis=-1, keepdims=True) + EPS) * g


def _dot(a, b):
    return jnp.dot(a, b, preferred_element_type=F32)


def _dot_nt(a, b):
    return lax.dot_general(a, b, (((1,), (1,)), ((), ())), preferred_element_type=F32)


def _const_spec(shape):
    return pl.BlockSpec(shape, lambda *_: (0,) * len(shape), pipeline_mode=pl.Buffered(1))


def _in_proj_body(x_ref, pos_ref, invf_ref, g_pre_ref, w_in_ref, w_tail_ref, qn_ref, kvn_ref,
                  w_uq_ref, w_uk_ref, w_uv_ref, w_mv_ref, vones_ref, conv_w_ref, conv_b_ref, gbias_ref,
                  qt_out, k_out, vt_out, mqt_out, mk_out, mvt_out, mo_out, grow_out, keycol_out,
                  ext_ref, stage_ref, *, tm, sub, tiles_per_seq, chunk):
    assert tm % sub == 0 and sub % chunk == 0
    i = pl.program_id(0)
    halo = SUBLANES

    @pl.when(lax.rem(i, tiles_per_seq) == 0)
    def _():
        ext_ref[:, 0:halo, :] = jnp.zeros((ext_ref.shape[0], halo, LANES), F32)

    for r0 in range(0, tm, sub):
        _in_proj_rows(r0, sub, halo, chunk, x_ref, pos_ref, invf_ref, g_pre_ref, w_in_ref, w_tail_ref,
                      qn_ref, kvn_ref, w_uq_ref, w_uk_ref, w_uv_ref, w_mv_ref, vones_ref, conv_w_ref,
                      conv_b_ref, gbias_ref, qt_out, k_out, vt_out, mqt_out, mk_out, mvt_out, mo_out,
                      grow_out, keycol_out, ext_ref, stage_ref)
    ext_ref[:, 0:halo, :] = ext_ref[:, tm:tm + halo, :]


def _in_proj_rows(r0, ts, halo, chunk, x_ref, pos_ref, invf_ref, g_pre_ref, w_in_ref, w_tail_ref,
                  qn_ref, kvn_ref, w_uq_ref, w_uk_ref, w_uv_ref, w_mv_ref, vones_ref, conv_w_ref,
                  conv_b_ref, gbias_ref, qt_out, k_out, vt_out, mqt_out, mk_out, mvt_out, mo_out,
                  grow_out, keycol_out, ext_ref, stage_ref):
    rows = slice(r0, r0 + ts)
    tile, lanes = r0 // TILE_T, slice(r0 % TILE_T, r0 % TILE_T + ts)
    u = _rms(x_ref[rows, :], g_pre_ref[...]).astype(BF16)

    cq_raw = _dot(u, w_in_ref[:, C_Q:C_Q + Q_LORA_RANK])
    ckv_raw = _dot(u, w_in_ref[:, C_KV:C_KV + KV_LORA_RANK])
    tail_t = _dot_nt(w_tail_ref[...], u)
    z = _dot(u, w_in_ref[:, C_MQK:C_MQK + 2 * MLSTM_WIDTH])
    for c in range(2 * MLSTM_WIDTH // LANES):
        ext_ref[c, halo + r0:halo + r0 + ts, :] = z[:, c * LANES:(c + 1) * LANES]
    mvt_out[tile, :, lanes] = (_dot_nt(w_mv_ref[...], u) + vones_ref[...]).astype(BF16)
    mo_out[rows, :] = jax.nn.sigmoid(_dot(u, w_in_ref[:, C_MO:C_MO + MLSTM_WIDTH])).astype(BF16)
    cq = _rms(cq_raw, qn_ref[...]).astype(BF16)
    ckv = _rms(ckv_raw, kvn_ref[...]).astype(BF16)
    qt = _dot_nt(w_uq_ref[...], cq)
    kn = _dot(ckv, w_uk_ref[...])
    vt_out[tile, :, lanes] = (_dot_nt(w_uv_ref[...], ckv) + vones_ref[...]).astype(BF16)

    ang = invf_ref[...] * pos_ref[:, rows]
    cos = jnp.cos(ang)
    sin = jnp.sin(ang)

    def rope_t(t1, t2):
        return t1 * cos - t2 * sin, t2 * cos + t1 * sin

    scale = QK_HEAD_DIM ** -0.5 * math.log2(math.e)
    for h in range(MLA_HEADS):
        o = h * QK_PAD
        pe = o + QK_NOPE_DIM
        r1, r2 = rope_t(qt[pe:pe + ROPE_HALF], qt[pe + ROPE_HALF:pe + QK_ROPE_DIM])
        qt_out[tile, o:pe, lanes] = (qt[o:pe] * scale).astype(BF16)
        qt_out[tile, pe:pe + ROPE_HALF, lanes] = (r1 * scale).astype(BF16)
        qt_out[tile, pe + ROPE_HALF:pe + QK_ROPE_DIM, lanes] = (r2 * scale).astype(BF16)
        qt_out[tile, pe + QK_ROPE_DIM:o + QK_PAD, lanes] = jnp.zeros((QK_PAD - QK_HEAD_DIM, ts), BF16)

    r1, r2 = rope_t(tail_t[0:ROPE_HALF], tail_t[ROPE_HALF:QK_ROPE_DIM])
    kpe_t = jnp.concatenate([r1, r2, jnp.zeros((LANES - QK_ROPE_DIM, ts), F32)], axis=0)
    kpe = kpe_t.T.astype(BF16)
    for h in range(MLA_HEADS):
        o = h * QK_PAD
        k_out[rows, o:o + LANES] = kn[:, h * LANES:(h + 1) * LANES].astype(BF16)
        k_out[rows, o + LANES:o + QK_PAD] = kpe

    gates = tail_t[GATE_I:GATE_I + 2 * SUBLANES] + gbias_ref[...]
    pos_in_chunk = lax.rem(r0 + lax.broadcasted_iota(jnp.int32, (SUBLANES, ts), 1), chunk)

    def chunk_scan(v, op, identity):
        shift = 1
        while shift < chunk:
            v = op(v, jnp.where(pos_in_chunk >= shift, pltpu.roll(v, shift, 1), identity))
            shift *= 2
        return v

    b8 = chunk_scan(jax.nn.log_sigmoid(gates[SUBLANES:]), jnp.add, 0.0)
    key8 = gates[:SUBLANES] - b8
    grow_out[:, rows] = jnp.concatenate([key8, chunk_scan(key8, jnp.maximum, -jnp.inf), b8], axis=0)
    keycol_out[rows, :] = jnp.concatenate([key8, jnp.zeros((LANES - SUBLANES, ts), F32)], axis=0).T

    n = ts // CONV_STRIDE
    for c in range(2 * MLSTM_WIDTH // LANES):
        cols = slice(c * LANES, (c + 1) * LANES)
        taps = {q: ext_ref[c, pl.ds(halo + r0 + q, n, stride=CONV_STRIDE), :]
                for q in range(1 - CONV_WIDTH, CONV_STRIDE)}
        for r in range(CONV_STRIDE):
            acc = conv_b_ref[:, cols]
            for j in range(CONV_WIDTH):
                acc = acc + conv_w_ref[j:j + 1, cols] * taps[r - (CONV_WIDTH - 1) + j]
            stage_ref[c, pl.ds(r0 + r, n, stride=CONV_STRIDE), :] = acc * jax.nn.sigmoid(acc)
    for c in range(MLSTM_WIDTH // LANES):
        cols = slice(c * LANES, (c + 1) * LANES)
        mqt_out[tile, cols, lanes] = stage_ref[c, rows, :].T.astype(BF16)
        k_slab = stage_ref[MLSTM_WIDTH // LANES + c, rows, :]
        mk_out[rows, cols] = (k_slab * (MLSTM_HEAD_DIM ** -0.5)).astype(BF16)


def _in_proj(x2, pos_row, invf, g_pre, w_in_p, w_tail, qn, kvn, w_uq_t, w_uk, w_uv_t, w_mv_t, vones,
             conv_w, conv_b, gbias, seq):
    t = x2.shape[0]
    tm = TM_IN
    nt = t // TILE_T
    row = lambda width: pl.BlockSpec((tm, width), lambda i: (i, 0))
    tile_t = lambda rows: pl.BlockSpec((tm // TILE_T, rows, TILE_T), lambda i: (i, 0, 0))
    bf = lambda *shape: jax.ShapeDtypeStruct(shape, BF16)
    out_shape = [
        bf(nt, MLA_HEADS * QK_PAD, TILE_T), bf(t, MLA_HEADS * QK_PAD), bf(nt, MLA_HEADS * V_ROWS, TILE_T),
        bf(nt, MLSTM_WIDTH, TILE_T), bf(t, MLSTM_WIDTH), bf(nt, MLSTM_HEADS * V_ROWS, TILE_T),
        bf(t, MLSTM_WIDTH), jax.ShapeDtypeStruct((GATE_ROWS, t), F32),
        jax.ShapeDtypeStruct((t, LANES), F32),
    ]
    out_specs = [
        tile_t(MLA_HEADS * QK_PAD), row(MLA_HEADS * QK_PAD), tile_t(MLA_HEADS * V_ROWS),
        tile_t(MLSTM_WIDTH), row(MLSTM_WIDTH), tile_t(MLSTM_HEADS * V_ROWS), row(MLSTM_WIDTH),
        pl.BlockSpec((GATE_ROWS, tm), lambda i: (0, i)), row(LANES),
    ]
    consts = (invf, g_pre, w_in_p, w_tail, qn, kvn, w_uq_t, w_uk, w_uv_t, w_mv_t, vones, conv_w,
              conv_b, gbias)
    return pl.pallas_call(
        functools.partial(_in_proj_body, tm=tm, sub=SUB_IN, tiles_per_seq=seq // tm, chunk=CHUNK),
        out_shape=out_shape,
        grid=(t // tm,),
        in_specs=[row(D_MODEL), pl.BlockSpec((1, tm), lambda i: (0, i))]
        + [_const_spec(c.shape) for c in consts],
        out_specs=out_specs,
        scratch_shapes=[pltpu.VMEM((2 * MLSTM_WIDTH // LANES, tm + 2 * SUBLANES, LANES), F32),
                        pltpu.VMEM((2 * MLSTM_WIDTH // LANES, tm, LANES), F32)],
        compiler_params=pltpu.CompilerParams(dimension_semantics=("arbitrary",),
                                             vmem_limit_bytes=VMEM_LIMIT),
        name="in_proj",
    )(x2, pos_row, *consts)


def _attn_body(qt_ref, k_ref, vt_ref, o_ref, m_ref, use_ref, viol_ref, acc_ref, pa_ref, pb_ref,
               ua_ref, ub_ref, *, tq, heads):
    qi = pl.program_id(2)
    hq = [slice(h * QK_PAD, (h + 1) * QK_PAD) for h in range(heads)]
    hv = [slice(h * V_ROWS, (h + 1) * V_ROWS) for h in range(heads)]
    stat = m_ref.shape[1:]

    def qk(ki, h):
        start = pl.multiple_of(ki * tq, tq)
        return _dot(k_ref[pl.ds(start, tq), hq[h]], qt_ref[0, hq[h], :])

    def causal(s):
        r = lax.broadcasted_iota(jnp.int32, s.shape, 0)
        c = lax.broadcasted_iota(jnp.int32, s.shape, 1)
        return jnp.where(r <= c, s, -jnp.inf)

    def fast_scores(ki, p_ref, u_ref):
        for h in range(heads):
            s = qk(ki, h)
            m_use = m_ref[h, 0:1, :]
            p_ref[h] = jnp.exp2(s - m_use).astype(BF16)
            cmax = jnp.max(s, axis=0, keepdims=True)
            excess = jnp.where(qi > 0, cmax - m_use, 0.0)
            u_ref[h] = jnp.broadcast_to(m_use, stat)
            viol_ref[h] = jnp.broadcast_to(jnp.maximum(viol_ref[h, 0:1, :], excess), stat)
            m_ref[h] = jnp.broadcast_to(jnp.maximum(m_use, cmax), stat)

    def fast_pv(ki, p_ref, u_ref):
        for h in range(heads):
            m_use = u_ref[h, 0:1, :]
            alpha = jnp.exp2(use_ref[h, 0:1, :] - m_use)
            acc_ref[h] = alpha * acc_ref[h] + _dot(vt_ref[ki, hv[h], :], p_ref[h])
            use_ref[h] = jnp.broadcast_to(m_use, stat)

    def exact_block(ki, diagonal):
        for h in range(heads):
            s = causal(qk(ki, h)) if diagonal else qk(ki, h)
            m_old = m_ref[h, 0:1, :]
            m_new = jnp.maximum(m_old, jnp.max(s, axis=0, keepdims=True))
            p = jnp.exp2(s - m_new).astype(BF16)
            acc_ref[h] = jnp.exp2(m_old - m_new) * acc_ref[h] + _dot(vt_ref[ki, hv[h], :], p)
            m_ref[h] = jnp.broadcast_to(m_new, stat)

    def start():
        viol_ref[...] = jnp.zeros(viol_ref.shape, F32)
        s_diag = [causal(qk(qi, h)) for h in range(heads)]
        for h in range(heads):
            cmax = jnp.broadcast_to(jnp.max(s_diag[h], axis=0, keepdims=True), stat)
            m_ref[h] = cmax
            use_ref[h] = cmax
        fast_scores(0, pa_ref, ua_ref)
        for h in range(heads):
            p = jnp.exp2(s_diag[h] - use_ref[h, 0:1, :]).astype(BF16)
            acc_ref[h] = _dot(vt_ref[qi, hv[h], :], p)

    bufs = ((pa_ref, ua_ref), (pb_ref, ub_ref))

    def run(first, count, score_next):
        for i in range(count):
            if i + 1 < count or score_next:
                fast_scores(first + i + 1, *bufs[(i + 1) % 2])
            fast_pv(first + i, *bufs[i % 2])

    def four(j, carry):
        run(4 * j, 4, True)
        return carry

    for short in range(5):
        @pl.when(qi == short)
        def _(short=short):
            start()
            run(0, short, False)

    @pl.when(qi > 4)
    def _():
        start()
        run(0, 4, True)

    n_loop = jnp.maximum(qi - 1, 0) // 4
    lax.fori_loop(1, n_loop, four, 0)
    for left in range(1, 5):
        @pl.when(jnp.logical_and(qi > 4, qi - 4 * n_loop == left))
        def _(left=left):
            run(4 * n_loop, left, False)

    @pl.when(jnp.max(viol_ref[...]) > MAX_STALE_EXCESS)
    def _():
        m_ref[...] = jnp.full(m_ref.shape, -jnp.inf, F32)
        acc_ref[...] = jnp.zeros(acc_ref.shape, F32)

        def blk(ki, carry):
            exact_block(ki, False)
            return carry

        lax.fori_loop(0, qi, blk, 0)
        exact_block(qi, True)

    for h in range(heads):
        acc = acc_ref[h]
        out_t = acc[0:V_HEAD_DIM] / acc[V_HEAD_DIM:V_HEAD_DIM + 1]
        o_ref[:, h * V_HEAD_DIM:(h + 1) * V_HEAD_DIM] = out_t.T.astype(o_ref.dtype)


def _attention(qt, k, vt, batch, seq):
    t = k.shape[0]
    tq = TQ
    nq = seq // tq
    heads = HEADS_PER_STEP
    return pl.pallas_call(
        functools.partial(_attn_body, tq=tq, heads=heads),
        out_shape=jax.ShapeDtypeStruct((t, MLA_WIDTH), BF16),
        grid=(batch, MLA_HEADS // heads, nq),
        in_specs=[
            pl.BlockSpec((1, heads * QK_PAD, tq), lambda b, g, i: (b * nq + i, g, 0)),
            pl.BlockSpec((seq, heads * QK_PAD), lambda b, g, i: (b, g)),
            pl.BlockSpec((nq, heads * V_ROWS, tq), lambda b, g, i: (b, g, 0)),
        ],
        out_specs=pl.BlockSpec((tq, heads * V_HEAD_DIM), lambda b, g, i: (b * nq + i, g)),
        scratch_shapes=[pltpu.VMEM((heads, SUBLANES, tq), F32),
                        pltpu.VMEM((heads, SUBLANES, tq), F32),
                        pltpu.VMEM((heads, SUBLANES, tq), F32),
                        pltpu.VMEM((heads, V_ROWS, tq), F32),
                        pltpu.VMEM((heads, tq, tq), BF16),
                        pltpu.VMEM((heads, tq, tq), BF16),
                        pltpu.VMEM((heads, SUBLANES, tq), F32),
                        pltpu.VMEM((heads, SUBLANES, tq), F32)],
        compiler_params=pltpu.CompilerParams(
            dimension_semantics=("arbitrary", "arbitrary", "arbitrary"),
            vmem_limit_bytes=VMEM_LIMIT),
        name="mla_attention",
    )(qt, k, vt)


def _mlstm_body(qt_ref, k_ref, vt_ref, og_ref, grow_ref, keycol_ref, norm_ref, out_ref, ct_ref,
                m_ref, *, chunk, chunks_per_step):
    @pl.when(pl.program_id(1) == 0)
    def _():
        ct_ref[...] = jnp.zeros(ct_ref.shape, F32)
        m_ref[...] = jnp.zeros(m_ref.shape, F32)

    heads = range(MLSTM_HEADS)
    hs = [slice(h * MLSTM_HEAD_DIM, (h + 1) * MLSTM_HEAD_DIM) for h in heads]
    vs = [slice(h * V_ROWS, (h + 1) * V_ROWS) for h in heads]
    tok = [slice(c * chunk, (c + 1) * chunk) for c in range(chunks_per_step)]
    ttok = [(c * chunk // TILE_T, slice(c * chunk % TILE_T, c * chunk % TILE_T + chunk))
            for c in range(chunks_per_step)]
    r = lax.broadcasted_iota(jnp.int32, (chunk, chunk), 0)
    c = lax.broadcasted_iota(jnp.int32, (chunk, chunk), 1)
    upper = r <= c

    s_raw = [[_dot(k_ref[t, hs[h]], qt_ref[tile, hs[h], lanes]) for h in heads]
             for t, (tile, lanes) in zip(tok, ttok)]
    ct = [ct_ref[h] for h in heads]
    m_prev = [m_ref[h, 0:1, 0:1] for h in heads]
    for ci, (t, (tile, lanes)) in enumerate(zip(tok, ttok)):
        qt = [qt_ref[tile, hs[h], lanes] for h in heads]
        vt = [vt_ref[tile, vs[h], lanes] for h in heads]
        cq = [_dot(ct[h].astype(BF16), qt[h]) for h in heads]
        grow = grow_ref[:, t]
        mm = [jnp.maximum(m_prev[h], grow[ROW_KMAX + h:ROW_KMAX + h + 1]) for h in heads]
        b_row = [grow[ROW_B + h:ROW_B + h + 1] for h in heads]
        inter = [jnp.exp(m_prev[h] - mm[h]) for h in heads]
        for h in heads:
            mm_last = mm[h][:, chunk - 1:chunk]
            w_row = jnp.exp(grow[ROW_KEY + h:ROW_KEY + h + 1] - mm_last)
            vtw = (vt[h].astype(F32) * w_row).astype(BF16)
            ct[h] = jnp.exp(m_prev[h] - mm_last) * ct[h] + _dot(vtw, k_ref[t, hs[h]])
            m_prev[h] = b_row[h][:, chunk - 1:chunk] + mm_last
        keycol = keycol_ref[t, :]
        for h in heads:
            p = jnp.exp(jnp.where(upper, keycol[:, h:h + 1] - mm[h], -jnp.inf))
            st = (s_raw[ci][h] * p).astype(BF16)
            tot = _dot(vt[h], st) + inter[h] * cq[h]
            den = tot[MLSTM_HEAD_DIM:MLSTM_HEAD_DIM + 1]
            ht = tot[0:MLSTM_HEAD_DIM] / jnp.maximum(jnp.abs(den), jnp.exp(-(b_row[h] + mm[h])))
            hn_t = ht * lax.rsqrt(jnp.mean(ht * ht, axis=0, keepdims=True) + EPS)
            out_ref[t, hs[h]] = (hn_t.T * norm_ref[0:1, hs[h]]
                                 * og_ref[t, hs[h]].astype(F32)).astype(out_ref.dtype)
    for h in heads:
        ct_ref[h] = ct[h]
        m_ref[h] = jnp.broadcast_to(m_prev[h], m_ref.shape[1:])


def _mlstm(mqt, mk, mvt, mo, grow, keycol, norm, batch, seq):
    t = mk.shape[0]
    tile = MLSTM_TILES * TILE_T
    ns = seq // tile
    row = lambda width: pl.BlockSpec((tile, width), lambda b, j: (b * ns + j, 0))
    tile_t = lambda rows: pl.BlockSpec((MLSTM_TILES, rows, TILE_T), lambda b, j: (b * ns + j, 0, 0))
    return pl.pallas_call(
        functools.partial(_mlstm_body, chunk=CHUNK, chunks_per_step=tile // CHUNK),
        out_shape=jax.ShapeDtypeStruct((t, MLSTM_WIDTH), BF16),
        grid=(batch, ns),
        in_specs=[tile_t(MLSTM_WIDTH), row(MLSTM_WIDTH), tile_t(MLSTM_HEADS * V_ROWS),
                  row(MLSTM_WIDTH), pl.BlockSpec((GATE_ROWS, tile), lambda b, j: (0, b * ns + j)),
                  row(LANES), _const_spec(norm.shape)],
        out_specs=row(MLSTM_WIDTH),
        scratch_shapes=[
            pltpu.VMEM((MLSTM_HEADS, V_ROWS, MLSTM_HEAD_DIM), F32),
            pltpu.VMEM((MLSTM_HEADS, SUBLANES, LANES), F32),
        ],
        compiler_params=pltpu.CompilerParams(dimension_semantics=("arbitrary", "arbitrary"),
                                             vmem_limit_bytes=VMEM_LIMIT),
        name="mlstm",
    )(mqt, mk, mvt, mo, grow, keycol, norm)


def _out_ffn_body(x_ref, mla_ref, mlstm_ref, p_ref, w_out_ref, g_post_ref, g_fpre_ref, g_fpost_ref,
                  w_gate_ref, w_up_ref, w_down_ref, w_pg_ref, w_pp_ref, o_ref):
    tm = x_ref.shape[0]
    subs = [slice(r0, r0 + SUB_OUT) for r0 in range(0, tm, SUB_OUT)]
    mix = [_dot(mla_ref[s, :], w_out_ref[0:MLA_WIDTH, :])
           + _dot(mlstm_ref[s, :], w_out_ref[MLA_WIDTH:MLA_WIDTH + MLSTM_WIDTH, :]) for s in subs]
    h1 = [x_ref[s, :] + _rms(m, g_post_ref[...]) for s, m in zip(subs, mix)]
    f = [_rms(h, g_fpre_ref[...]).astype(BF16) for h in h1]
    ffn = [None] * len(subs)
    off = 0
    for width in FF_CHUNKS:
        cols = slice(off, off + width)
        gate = [_dot(v, w_gate_ref[:, cols]) for v in f]
        up = [_dot(v, w_up_ref[:, cols]) for v in f]
        act = [(g * jax.nn.sigmoid(g) * u).astype(BF16) for g, u in zip(gate, up)]
        part = [_dot(a, w_down_ref[cols, :]) for a in act]
        ffn = [p if acc is None else acc + p for acc, p in zip(ffn, part)]
        off += width
    h2 = [h + _rms(v, g_fpost_ref[...]) for h, v in zip(h1, ffn)]
    pgate = [jax.nn.sigmoid(_dot(h.astype(BF16), w_pg_ref[...])) for h in h2]
    for s, h, g in zip(subs, h2, pgate):
        o_ref[s, :] = h + g * _dot(p_ref[s, :].astype(BF16), w_pp_ref[...])


def _out_ffn(x2, mla, mlstm, p2, w_out, g_post, g_fpre, g_fpost, w_gate, w_up, w_down, w_pg, w_pp):
    t = x2.shape[0]
    tm = TM_OUT
    row = lambda width: pl.BlockSpec((tm, width), lambda i: (i, 0))
    consts = (w_out, g_post, g_fpre, g_fpost, w_gate, w_up, w_down, w_pg, w_pp)
    return pl.pallas_call(
        _out_ffn_body,
        out_shape=jax.ShapeDtypeStruct((t, D_MODEL), F32),
        grid=(t // tm,),
        in_specs=[row(D_MODEL), row(MLA_WIDTH), row(MLSTM_WIDTH), row(PLE_DIM)]
        + [_const_spec(c.shape) for c in consts],
        out_specs=row(D_MODEL),
        compiler_params=pltpu.CompilerParams(dimension_semantics=("arbitrary",),
                                             vmem_limit_bytes=VMEM_LIMIT),
        name="out_ffn",
    )(x2, mla, mlstm, p2, *consts)


def _pack_w_in(w_in):
    o_krope = Q_LORA_RANK + KV_LORA_RANK
    o_mq = o_krope + QK_ROPE_DIM
    o_mv = o_mq + 2 * MLSTM_WIDTH
    o_mo = o_mv + MLSTM_WIDTH
    o_gates = o_mo + MLSTM_WIDTH
    main = jnp.concatenate([w_in[:, :o_krope], w_in[:, o_mq:o_mv], w_in[:, o_mo:o_gates]], axis=1)
    gap = jnp.zeros((D_MODEL, SUBLANES - MLSTM_HEADS), w_in.dtype)
    tail = jnp.concatenate([w_in[:, o_krope:o_mq], w_in[:, o_gates:o_gates + MLSTM_HEADS], gap,
                            w_in[:, o_gates + MLSTM_HEADS:], gap], axis=1).T
    tail = jnp.pad(tail, ((0, LANES - tail.shape[0]), (0, 0)))
    mv_t = w_in[:, o_mv:o_mo].T.reshape(MLSTM_HEADS, MLSTM_HEAD_DIM, D_MODEL)
    mv_t = jnp.pad(mv_t, ((0, 0), (0, V_ROWS - MLSTM_HEAD_DIM), (0, 0)))
    return main.astype(BF16), tail.astype(BF16), mv_t.reshape(MLSTM_HEADS * V_ROWS, D_MODEL).astype(BF16)


def _pack_w_uq_t(w_uq):
    w = w_uq.T.reshape(MLA_HEADS, QK_HEAD_DIM, Q_LORA_RANK)
    w = jnp.pad(w, ((0, 0), (0, QK_PAD - QK_HEAD_DIM), (0, 0)))
    return w.reshape(MLA_HEADS * QK_PAD, Q_LORA_RANK).astype(BF16)


def _pack_w_ukv(w_ukv):
    w = w_ukv.reshape(KV_LORA_RANK, MLA_HEADS, QK_NOPE_DIM + V_HEAD_DIM)
    w_uk = w[:, :, :QK_NOPE_DIM].reshape(KV_LORA_RANK, MLA_HEADS * QK_NOPE_DIM)
    w_uv_t = jnp.transpose(w[:, :, QK_NOPE_DIM:], (1, 2, 0))
    w_uv_t = jnp.pad(w_uv_t, ((0, 0), (0, V_ROWS - V_HEAD_DIM), (0, 0)))
    return w_uk.astype(BF16), w_uv_t.reshape(MLA_HEADS * V_ROWS, KV_LORA_RANK).astype(BF16)


def _layer(h, p_i, pos_row, invf, attn_pre_norm, attn_post_norm, w_in, q_norm, kv_norm, w_uq, w_ukv,
           conv_w, conv_b, gate_bias_i, gate_bias_f, mlstm_norm, w_out, ffn_pre_norm, ffn_post_norm,
           w_gate, w_up, w_down, w_ple_proj, w_ple_gate, batch, seq):
    row = lambda a: a.reshape(1, -1)
    w_in_p, w_tail, w_mv_t = _pack_w_in(w_in)
    w_uk, w_uv_t = _pack_w_ukv(w_ukv)
    gap = jnp.zeros((SUBLANES - MLSTM_HEADS,), F32)
    gbias = jnp.concatenate([gate_bias_i, gap, gate_bias_f, gap]).astype(F32).reshape(2 * SUBLANES, 1)
    vones = jnp.zeros((MLA_HEADS, V_ROWS, 1), F32).at[:, V_HEAD_DIM, 0].set(1.0)
    qt, k, vt, mqt, mk, mvt, mo, grow, keycol = _in_proj(
        h, pos_row, invf, row(attn_pre_norm), w_in_p, w_tail, row(q_norm), row(kv_norm),
        _pack_w_uq_t(w_uq), w_uk, w_uv_t, w_mv_t, vones.reshape(MLA_HEADS * V_ROWS, 1), conv_w,
        row(conv_b), gbias, seq)
    mla = _attention(qt, k, vt, batch, seq)
    mlstm = _mlstm(mqt, mk, mvt, mo, grow, keycol, row(mlstm_norm), batch, seq)
    return _out_ffn(h, mla, mlstm, p_i, w_out.astype(BF16), row(attn_post_norm), row(ffn_pre_norm),
                    row(ffn_post_norm), w_gate.astype(BF16), w_up.astype(BF16),
                    w_down.astype(BF16), w_ple_gate.astype(BF16), w_ple_proj.astype(BF16))


@jax.jit
def kernel(x, p, positions, attn_pre_norm, attn_post_norm, w_in, q_norm, kv_norm, w_uq, w_ukv,
           conv_w, conv_b, gate_bias_i, gate_bias_f, mlstm_norm, w_out, ffn_pre_norm, ffn_post_norm,
           w_gate, w_up, w_down, w_ple_proj, w_ple_gate):
    batch, seq, _ = x.shape
    t = batch * seq
    depth = p.shape[0]
    inv_freq = ROPE_THETA ** (-jnp.arange(0, QK_ROPE_DIM, 2, dtype=F32) / QK_ROPE_DIM)
    invf = inv_freq.reshape(ROPE_HALF, 1)
    pos_row = positions.astype(F32).reshape(1, t)
    h = x.reshape(t, D_MODEL)
    for i in range(depth):
        h = _layer(h, p[i].reshape(t, PLE_DIM), pos_row, invf, attn_pre_norm[i], attn_post_norm[i],
                   w_in[i], q_norm[i], kv_norm[i], w_uq[i], w_ukv[i], conv_w[i], conv_b[i],
                   gate_bias_i[i], gate_bias_f[i], mlstm_norm[i], w_out[i], ffn_pre_norm[i],
                   ffn_post_norm[i], w_gate[i], w_up[i], w_down[i], w_ple_proj[i], w_ple_gate[i],
                   batch, seq)
    return h.reshape(batch, seq, D_MODEL)
```

```python
import functools
import math

import jax
import jax.numpy as jnp
from jax import lax
from jax.experimental import pallas as pl
from jax.experimental.pallas import tpu as pltpu

F32 = jnp.float32
BF16 = jnp.bfloat16

D_MODEL = 1024
PLE_DIM = 256
MLA_HEADS = 4
QK_NOPE_DIM = 128
QK_ROPE_DIM = 64
QK_HEAD_DIM = QK_NOPE_DIM + QK_ROPE_DIM
V_HEAD_DIM = 128
Q_LORA_RANK = 256
KV_LORA_RANK = 128
ROPE_THETA = 10000.0
MLA_WIDTH = MLA_HEADS * V_HEAD_DIM
MLSTM_HEADS = 4
MLSTM_HEAD_DIM = 128
MLSTM_WIDTH = MLSTM_HEADS * MLSTM_HEAD_DIM
CONV_WIDTH = 4
CONV_STRIDE = 4
D_FF = 2816
EPS = 1e-6

LANES = 128
SUBLANES = 8
BF16_ROWS = 16
QK_PAD = 2 * LANES
V_ROWS = V_HEAD_DIM + BF16_ROWS
ROPE_HALF = QK_ROPE_DIM // 2
VMEM_LIMIT = 56 * 1024 * 1024

C_Q = 0
C_KV = C_Q + Q_LORA_RANK
C_MQK = C_KV + KV_LORA_RANK
D_IN_PACKED = C_MQK + 2 * MLSTM_WIDTH
GATE_I = QK_ROPE_DIM
GATE_F = GATE_I + SUBLANES
ROW_KEY, ROW_KMAX, ROW_B, GATE_ROWS = 0, SUBLANES, 2 * SUBLANES, 3 * SUBLANES

TILE_T = 512
TM_IN = 512
SUB_IN = 256
TQ = TILE_T
HEADS_PER_STEP = 2
MAX_STALE_EXCESS = 64.0
CHUNK = 256
MLSTM_TILES = 2
TM_OUT = 512
SUB_OUT = 256
FF_CHUNKS = (1024, 1024, 768)


def _rms(x, g):
    return x * lax.rsqrt(jnp.mean(x * x, axis=-1, keepdims=True) + EPS) * g


def _dot(a, b):
    return jnp.dot(a, b, preferred_element_type=F32)


def _dot_nt(a, b):
    return lax.dot_general(a, b, (((1,), (1,)), ((), ())), preferred_element_type=F32)


def _dot_tn(a, b):
    return lax.dot_general(a, b, (((0,), (0,)), ((), ())), preferred_element_type=F32)


def _const_spec(shape):
    return pl.BlockSpec(shape, lambda *_: (0,) * len(shape), pipeline_mode=pl.Buffered(1))


def _in_proj_body(x_ref, pos_ref, invf_ref, g_pre_ref, w_in_ref, w_tail_ref, qn_ref, kvn_ref,
                  w_uq_ref, w_uk_ref, w_uv_ref, w_mv_ref, vones_ref, conv_w_ref, conv_b_ref, gbias_ref,
                  qt_out, k_out, vt_out, mqt_out, mk_out, mvt_out, mot_out, grow_out, keycol_out,
                  ext_ref, stage_ref, *, tm, sub, tiles_per_seq, chunk):
    assert tm % sub == 0 and sub % chunk == 0
    i = pl.program_id(0)
    halo = SUBLANES

    @pl.when(lax.rem(i, tiles_per_seq) == 0)
    def _():
        ext_ref[:, 0:halo, :] = jnp.zeros((ext_ref.shape[0], halo, LANES), F32)

    for r0 in range(0, tm, sub):
        _in_proj_rows(r0, sub, halo, chunk, x_ref, pos_ref, invf_ref, g_pre_ref, w_in_ref, w_tail_ref,
                      qn_ref, kvn_ref, w_uq_ref, w_uk_ref, w_uv_ref, w_mv_ref, vones_ref, conv_w_ref,
                      conv_b_ref, gbias_ref, qt_out, k_out, vt_out, mqt_out, mk_out, mvt_out, mot_out,
                      grow_out, keycol_out, ext_ref, stage_ref)
    ext_ref[:, 0:halo, :] = ext_ref[:, tm:tm + halo, :]


def _in_proj_rows(r0, ts, halo, chunk, x_ref, pos_ref, invf_ref, g_pre_ref, w_in_ref, w_tail_ref,
                  qn_ref, kvn_ref, w_uq_ref, w_uk_ref, w_uv_ref, w_mv_ref, vones_ref, conv_w_ref,
                  conv_b_ref, gbias_ref, qt_out, k_out, vt_out, mqt_out, mk_out, mvt_out, mot_out,
                  grow_out, keycol_out, ext_ref, stage_ref):
    rows = slice(r0, r0 + ts)
    tile, lanes = r0 // TILE_T, slice(r0 % TILE_T, r0 % TILE_T + ts)
    u = _rms(x_ref[rows, :], g_pre_ref[...]).astype(BF16)

    cq_raw = _dot(u, w_in_ref[:, C_Q:C_Q + Q_LORA_RANK])
    ckv_raw = _dot(u, w_in_ref[:, C_KV:C_KV + KV_LORA_RANK])
    tail_t = _dot_nt(w_tail_ref[...], u)
    z = _dot(u, w_in_ref[:, C_MQK:C_MQK + 2 * MLSTM_WIDTH])
    for c in range(2 * MLSTM_WIDTH // LANES):
        ext_ref[c, halo + r0:halo + r0 + ts, :] = z[:, c * LANES:(c + 1) * LANES]
    mvo_t = _dot_nt(w_mv_ref[...], u)
    n_v = MLSTM_HEADS * V_ROWS
    mvt_out[tile, :, lanes] = (mvo_t[:n_v] + vones_ref[...]).astype(BF16)
    mot_out[tile, :, lanes] = jax.nn.sigmoid(mvo_t[n_v:]).astype(BF16)
    cq = _rms(cq_raw, qn_ref[...]).astype(BF16)
    ckv = _rms(ckv_raw, kvn_ref[...]).astype(BF16)
    qt = _dot_nt(w_uq_ref[...], cq)
    kn = _dot(ckv, w_uk_ref[...])
    vt_out[tile, :, lanes] = (_dot_nt(w_uv_ref[...], ckv) + vones_ref[...]).astype(BF16)

    ang = invf_ref[...] * pos_ref[:, rows]
    cos = jnp.cos(ang)
    sin = jnp.sin(ang)

    def rope_t(t1, t2):
        return t1 * cos - t2 * sin, t2 * cos + t1 * sin

    scale = QK_HEAD_DIM ** -0.5 * math.log2(math.e)
    for h in range(MLA_HEADS):
        o = h * QK_PAD
        pe = o + QK_NOPE_DIM
        r1, r2 = rope_t(qt[pe:pe + ROPE_HALF], qt[pe + ROPE_HALF:pe + QK_ROPE_DIM])
        qt_out[tile, o:pe, lanes] = (qt[o:pe] * scale).astype(BF16)
        qt_out[tile, pe:pe + ROPE_HALF, lanes] = (r1 * scale).astype(BF16)
        qt_out[tile, pe + ROPE_HALF:pe + QK_ROPE_DIM, lanes] = (r2 * scale).astype(BF16)
        qt_out[tile, pe + QK_ROPE_DIM:o + QK_PAD, lanes] = jnp.zeros((QK_PAD - QK_HEAD_DIM, ts), BF16)

    r1, r2 = rope_t(tail_t[0:ROPE_HALF], tail_t[ROPE_HALF:QK_ROPE_DIM])
    kpe_t = jnp.concatenate([r1, r2, jnp.zeros((LANES - QK_ROPE_DIM, ts), F32)], axis=0)
    kpe = kpe_t.T.astype(BF16)
    for h in range(MLA_HEADS):
        o = h * QK_PAD
        k_out[rows, o:o + LANES] = kn[:, h * LANES:(h + 1) * LANES].astype(BF16)
        k_out[rows, o + LANES:o + QK_PAD] = kpe

    gates = tail_t[GATE_I:GATE_I + 2 * SUBLANES] + gbias_ref[...]
    pos_in_chunk = lax.rem(r0 + lax.broadcasted_iota(jnp.int32, (SUBLANES, ts), 1), chunk)

    def chunk_scan(v, op, identity):
        shift = 1
        while shift < chunk:
            v = op(v, jnp.where(pos_in_chunk >= shift, pltpu.roll(v, shift, 1), identity))
            shift *= 2
        return v

    b8 = chunk_scan(jax.nn.log_sigmoid(gates[SUBLANES:]), jnp.add, 0.0)
    key8 = gates[:SUBLANES] - b8
    grow_out[:, rows] = jnp.concatenate([key8, chunk_scan(key8, jnp.maximum, -jnp.inf), b8], axis=0)
    keycol_out[rows, :] = jnp.concatenate([key8, jnp.zeros((LANES - SUBLANES, ts), F32)], axis=0).T

    n = ts // CONV_STRIDE
    for c in range(2 * MLSTM_WIDTH // LANES):
        cols = slice(c * LANES, (c + 1) * LANES)
        taps = {q: ext_ref[c, pl.ds(halo + r0 + q, n, stride=CONV_STRIDE), :]
                for q in range(1 - CONV_WIDTH, CONV_STRIDE)}
        for r in range(CONV_STRIDE):
            acc = conv_b_ref[:, cols]
            for j in range(CONV_WIDTH):
                acc = acc + conv_w_ref[j:j + 1, cols] * taps[r - (CONV_WIDTH - 1) + j]
            stage_ref[c, pl.ds(r0 + r, n, stride=CONV_STRIDE), :] = acc * jax.nn.sigmoid(acc)
    for c in range(MLSTM_WIDTH // LANES):
        cols = slice(c * LANES, (c + 1) * LANES)
        mqt_out[tile, cols, lanes] = stage_ref[c, rows, :].T.astype(BF16)
        k_slab = stage_ref[MLSTM_WIDTH // LANES + c, rows, :]
        mk_out[rows, cols] = (k_slab * (MLSTM_HEAD_DIM ** -0.5)).astype(BF16)


def _in_proj(x2, pos_row, invf, g_pre, w_in_p, w_tail, qn, kvn, w_uq_t, w_uk, w_uv_t, w_mv_t, vones,
             conv_w, conv_b, gbias, seq):
    t = x2.shape[0]
    tm = TM_IN
    nt = t // TILE_T
    row = lambda width: pl.BlockSpec((tm, width), lambda i: (i, 0))
    tile_t = lambda rows: pl.BlockSpec((tm // TILE_T, rows, TILE_T), lambda i: (i, 0, 0))
    bf = lambda *shape: jax.ShapeDtypeStruct(shape, BF16)
    out_shape = [
        bf(nt, MLA_HEADS * QK_PAD, TILE_T), bf(t, MLA_HEADS * QK_PAD), bf(nt, MLA_HEADS * V_ROWS, TILE_T),
        bf(nt, MLSTM_WIDTH, TILE_T), bf(t, MLSTM_WIDTH), bf(nt, MLSTM_HEADS * V_ROWS, TILE_T),
        bf(nt, MLSTM_WIDTH, TILE_T), jax.ShapeDtypeStruct((GATE_ROWS, t), F32),
        jax.ShapeDtypeStruct((t, LANES), F32),
    ]
    out_specs = [
        tile_t(MLA_HEADS * QK_PAD), row(MLA_HEADS * QK_PAD), tile_t(MLA_HEADS * V_ROWS),
        tile_t(MLSTM_WIDTH), row(MLSTM_WIDTH), tile_t(MLSTM_HEADS * V_ROWS), tile_t(MLSTM_WIDTH),
        pl.BlockSpec((GATE_ROWS, tm), lambda i: (0, i)), row(LANES),
    ]
    consts = (invf, g_pre, w_in_p, w_tail, qn, kvn, w_uq_t, w_uk, w_uv_t, w_mv_t, vones, conv_w,
              conv_b, gbias)
    return pl.pallas_call(
        functools.partial(_in_proj_body, tm=tm, sub=SUB_IN, tiles_per_seq=seq // tm, chunk=CHUNK),
        out_shape=out_shape,
        grid=(t // tm,),
        in_specs=[row(D_MODEL), pl.BlockSpec((1, tm), lambda i: (0, i))]
        + [_const_spec(c.shape) for c in consts],
        out_specs=out_specs,
        scratch_shapes=[pltpu.VMEM((2 * MLSTM_WIDTH // LANES, tm + 2 * SUBLANES, LANES), F32),
                        pltpu.VMEM((2 * MLSTM_WIDTH // LANES, tm, LANES), F32)],
        compiler_params=pltpu.CompilerParams(dimension_semantics=("arbitrary",),
                                             vmem_limit_bytes=VMEM_LIMIT),
        name="in_proj",
    )(x2, pos_row, *consts)


def _attn_body(qt_ref, k_ref, vt_ref, o_ref, m_ref, use_ref, viol_ref, acc_ref, pa_ref, pb_ref,
               ua_ref, ub_ref, *, tq, heads):
    qi = pl.program_id(2)
    hq = [slice(h * QK_PAD, (h + 1) * QK_PAD) for h in range(heads)]
    hv = [slice(h * V_ROWS, (h + 1) * V_ROWS) for h in range(heads)]
    stat = m_ref.shape[1:]

    def qk(ki, h):
        start = pl.multiple_of(ki * tq, tq)
        return _dot(k_ref[pl.ds(start, tq), hq[h]], qt_ref[0, hq[h], :])

    def causal(s):
        r = lax.broadcasted_iota(jnp.int32, s.shape, 0)
        c = lax.broadcasted_iota(jnp.int32, s.shape, 1)
        return jnp.where(r <= c, s, -jnp.inf)

    def fast_scores(ki, p_ref, u_ref):
        for h in range(heads):
            s = qk(ki, h)
            m_use = m_ref[h, 0:1, :]
            p_ref[h] = jnp.exp2(s - m_use).astype(BF16)
            cmax = jnp.max(s, axis=0, keepdims=True)
            excess = jnp.where(qi > 0, cmax - m_use, 0.0)
            u_ref[h] = jnp.broadcast_to(m_use, stat)
            viol_ref[h] = jnp.broadcast_to(jnp.maximum(viol_ref[h, 0:1, :], excess), stat)
            m_ref[h] = jnp.broadcast_to(jnp.maximum(m_use, cmax), stat)

    def fast_pv(ki, p_ref, u_ref):
        for h in range(heads):
            m_use = u_ref[h, 0:1, :]
            alpha = jnp.exp2(use_ref[h, 0:1, :] - m_use)
            acc_ref[h] = alpha * acc_ref[h] + _dot(vt_ref[ki, hv[h], :], p_ref[h])
            use_ref[h] = jnp.broadcast_to(m_use, stat)

    def exact_block(ki, diagonal):
        for h in range(heads):
            s = causal(qk(ki, h)) if diagonal else qk(ki, h)
            m_old = m_ref[h, 0:1, :]
            m_new = jnp.maximum(m_old, jnp.max(s, axis=0, keepdims=True))
            p = jnp.exp2(s - m_new).astype(BF16)
            acc_ref[h] = jnp.exp2(m_old - m_new) * acc_ref[h] + _dot(vt_ref[ki, hv[h], :], p)
            m_ref[h] = jnp.broadcast_to(m_new, stat)

    def start():
        viol_ref[...] = jnp.zeros(viol_ref.shape, F32)
        s_diag = [causal(qk(qi, h)) for h in range(heads)]
        for h in range(heads):
            cmax = jnp.broadcast_to(jnp.max(s_diag[h], axis=0, keepdims=True), stat)
            m_ref[h] = cmax
            use_ref[h] = cmax
        fast_scores(0, pa_ref, ua_ref)
        for h in range(heads):
            p = jnp.exp2(s_diag[h] - use_ref[h, 0:1, :]).astype(BF16)
            acc_ref[h] = _dot(vt_ref[qi, hv[h], :], p)

    bufs = ((pa_ref, ua_ref), (pb_ref, ub_ref))

    def run(first, count, score_next):
        for i in range(count):
            if i + 1 < count or score_next:
                fast_scores(first + i + 1, *bufs[(i + 1) % 2])
            fast_pv(first + i, *bufs[i % 2])

    def four(j, carry):
        run(4 * j, 4, True)
        return carry

    for short in range(5):
        @pl.when(qi == short)
        def _(short=short):
            start()
            run(0, short, False)

    @pl.when(qi > 4)
    def _():
        start()
        run(0, 4, True)

    n_loop = jnp.maximum(qi - 1, 0) // 4
    lax.fori_loop(1, n_loop, four, 0)
    for left in range(1, 5):
        @pl.when(jnp.logical_and(qi > 4, qi - 4 * n_loop == left))
        def _(left=left):
            run(4 * n_loop, left, False)

    @pl.when(jnp.max(viol_ref[...]) > MAX_STALE_EXCESS)
    def _():
        m_ref[...] = jnp.full(m_ref.shape, -jnp.inf, F32)
        acc_ref[...] = jnp.zeros(acc_ref.shape, F32)

        def blk(ki, carry):
            exact_block(ki, False)
            return carry

        lax.fori_loop(0, qi, blk, 0)
        exact_block(qi, True)

    for h in range(heads):
        acc = acc_ref[h]
        out_t = acc[0:V_HEAD_DIM] / acc[V_HEAD_DIM:V_HEAD_DIM + 1]
        o_ref[0, h * V_HEAD_DIM:(h + 1) * V_HEAD_DIM, :] = out_t.astype(o_ref.dtype)


def _attention(qt, k, vt, batch, seq):
    t = k.shape[0]
    tq = TQ
    nq = seq // tq
    heads = HEADS_PER_STEP
    return pl.pallas_call(
        functools.partial(_attn_body, tq=tq, heads=heads),
        out_shape=jax.ShapeDtypeStruct((t // tq, MLA_WIDTH, tq), BF16),
        grid=(batch, MLA_HEADS // heads, nq),
        in_specs=[
            pl.BlockSpec((1, heads * QK_PAD, tq), lambda b, g, i: (b * nq + i, g, 0)),
            pl.BlockSpec((seq, heads * QK_PAD), lambda b, g, i: (b, g)),
            pl.BlockSpec((nq, heads * V_ROWS, tq), lambda b, g, i: (b, g, 0)),
        ],
        out_specs=pl.BlockSpec((1, heads * V_HEAD_DIM, tq), lambda b, g, i: (b * nq + i, g, 0)),
        scratch_shapes=[pltpu.VMEM((heads, SUBLANES, tq), F32),
                        pltpu.VMEM((heads, SUBLANES, tq), F32),
                        pltpu.VMEM((heads, SUBLANES, tq), F32),
                        pltpu.VMEM((heads, V_ROWS, tq), F32),
                        pltpu.VMEM((heads, tq, tq), BF16),
                        pltpu.VMEM((heads, tq, tq), BF16),
                        pltpu.VMEM((heads, SUBLANES, tq), F32),
                        pltpu.VMEM((heads, SUBLANES, tq), F32)],
        compiler_params=pltpu.CompilerParams(
            dimension_semantics=("arbitrary", "arbitrary", "arbitrary"),
            vmem_limit_bytes=VMEM_LIMIT),
        name="mla_attention",
    )(qt, k, vt)


def _mlstm_body(qt_ref, k_ref, vt_ref, og_ref, grow_ref, keycol_ref, norm_ref, out_ref, ct_ref,
                m_ref, *, chunk, chunks_per_step):
    @pl.when(pl.program_id(1) == 0)
    def _():
        ct_ref[...] = jnp.zeros(ct_ref.shape, F32)
        m_ref[...] = jnp.zeros(m_ref.shape, F32)

    heads = range(MLSTM_HEADS)
    hs = [slice(h * MLSTM_HEAD_DIM, (h + 1) * MLSTM_HEAD_DIM) for h in heads]
    vs = [slice(h * V_ROWS, (h + 1) * V_ROWS) for h in heads]
    tok = [slice(c * chunk, (c + 1) * chunk) for c in range(chunks_per_step)]
    ttok = [(c * chunk // TILE_T, slice(c * chunk % TILE_T, c * chunk % TILE_T + chunk))
            for c in range(chunks_per_step)]
    r = lax.broadcasted_iota(jnp.int32, (chunk, chunk), 0)
    c = lax.broadcasted_iota(jnp.int32, (chunk, chunk), 1)
    upper = r <= c

    s_raw = [[_dot(k_ref[t, hs[h]], qt_ref[tile, hs[h], lanes]) for h in heads]
             for t, (tile, lanes) in zip(tok, ttok)]
    ct = [ct_ref[h] for h in heads]
    m_prev = [m_ref[h, 0:1, 0:1] for h in heads]
    for ci, (t, (tile, lanes)) in enumerate(zip(tok, ttok)):
        qt = [qt_ref[tile, hs[h], lanes] for h in heads]
        vt = [vt_ref[tile, vs[h], lanes] for h in heads]
        cq = [_dot(ct[h].astype(BF16), qt[h]) for h in heads]
        grow = grow_ref[:, t]
        mm = [jnp.maximum(m_prev[h], grow[ROW_KMAX + h:ROW_KMAX + h + 1]) for h in heads]
        b_row = [grow[ROW_B + h:ROW_B + h + 1] for h in heads]
        inter = [jnp.exp(m_prev[h] - mm[h]) for h in heads]
        for h in heads:
            mm_last = mm[h][:, chunk - 1:chunk]
            w_row = jnp.exp(grow[ROW_KEY + h:ROW_KEY + h + 1] - mm_last)
            vtw = (vt[h].astype(F32) * w_row).astype(BF16)
            ct[h] = jnp.exp(m_prev[h] - mm_last) * ct[h] + _dot(vtw, k_ref[t, hs[h]])
            m_prev[h] = b_row[h][:, chunk - 1:chunk] + mm_last
        keycol = keycol_ref[t, :]
        for h in heads:
            p = jnp.exp(jnp.where(upper, keycol[:, h:h + 1] - mm[h], -jnp.inf))
            st = (s_raw[ci][h] * p).astype(BF16)
            tot = _dot(vt[h], st) + inter[h] * cq[h]
            den = tot[MLSTM_HEAD_DIM:MLSTM_HEAD_DIM + 1]
            ht = tot[0:MLSTM_HEAD_DIM] / jnp.maximum(jnp.abs(den), jnp.exp(-(b_row[h] + mm[h])))
            hn_t = ht * lax.rsqrt(jnp.mean(ht * ht, axis=0, keepdims=True) + EPS)
            out_ref[tile, hs[h], lanes] = (hn_t * norm_ref[hs[h], :]
                                           * og_ref[tile, hs[h], lanes].astype(F32)).astype(out_ref.dtype)
    for h in heads:
        ct_ref[h] = ct[h]
        m_ref[h] = jnp.broadcast_to(m_prev[h], m_ref.shape[1:])


def _mlstm(mqt, mk, mvt, mo, grow, keycol, norm, batch, seq):
    t = mk.shape[0]
    tile = MLSTM_TILES * TILE_T
    ns = seq // tile
    row = lambda width: pl.BlockSpec((tile, width), lambda b, j: (b * ns + j, 0))
    tile_t = lambda rows: pl.BlockSpec((MLSTM_TILES, rows, TILE_T), lambda b, j: (b * ns + j, 0, 0))
    return pl.pallas_call(
        functools.partial(_mlstm_body, chunk=CHUNK, chunks_per_step=tile // CHUNK),
        out_shape=jax.ShapeDtypeStruct((t // TILE_T, MLSTM_WIDTH, TILE_T), BF16),
        grid=(batch, ns),
        in_specs=[tile_t(MLSTM_WIDTH), row(MLSTM_WIDTH), tile_t(MLSTM_HEADS * V_ROWS),
                  tile_t(MLSTM_WIDTH), pl.BlockSpec((GATE_ROWS, tile), lambda b, j: (0, b * ns + j)),
                  row(LANES), _const_spec(norm.shape)],
        out_specs=tile_t(MLSTM_WIDTH),
        scratch_shapes=[
            pltpu.VMEM((MLSTM_HEADS, V_ROWS, MLSTM_HEAD_DIM), F32),
            pltpu.VMEM((MLSTM_HEADS, SUBLANES, LANES), F32),
        ],
        compiler_params=pltpu.CompilerParams(dimension_semantics=("arbitrary", "arbitrary"),
                                             vmem_limit_bytes=VMEM_LIMIT),
        name="mlstm",
    )(mqt, mk, mvt, mo, grow, keycol, norm)


def _out_ffn_body(x_ref, mla_ref, mlstm_ref, p_ref, w_out_ref, g_post_ref, g_fpre_ref, g_fpost_ref,
                  w_gate_ref, w_up_ref, w_down_ref, w_pg_ref, w_pp_ref, o_ref):
    tm = x_ref.shape[0]
    subs = [slice(r0, r0 + SUB_OUT) for r0 in range(0, tm, SUB_OUT)]
    tsubs = [(s.start // TILE_T, slice(s.start % TILE_T, s.start % TILE_T + SUB_OUT)) for s in subs]
    mix = [_dot_tn(mla_ref[tile, :, lanes], w_out_ref[0:MLA_WIDTH, :])
           + _dot_tn(mlstm_ref[tile, :, lanes], w_out_ref[MLA_WIDTH:MLA_WIDTH + MLSTM_WIDTH, :])
           for tile, lanes in tsubs]
    h1 = [x_ref[s, :] + _rms(m, g_post_ref[...]) for s, m in zip(subs, mix)]
    f = [_rms(h, g_fpre_ref[...]).astype(BF16) for h in h1]
    ffn = [None] * len(subs)
    off = 0
    for width in FF_CHUNKS:
        cols = slice(off, off + width)
        gate = [_dot(v, w_gate_ref[:, cols]) for v in f]
        up = [_dot(v, w_up_ref[:, cols]) for v in f]
        act = [(g * jax.nn.sigmoid(g) * u).astype(BF16) for g, u in zip(gate, up)]
        part = [_dot(a, w_down_ref[cols, :]) for a in act]
        ffn = [p if acc is None else acc + p for acc, p in zip(ffn, part)]
        off += width
    h2 = [h + _rms(v, g_fpost_ref[...]) for h, v in zip(h1, ffn)]
    pgate = [jax.nn.sigmoid(_dot(h.astype(BF16), w_pg_ref[...])) for h in h2]
    for s, h, g in zip(subs, h2, pgate):
        o_ref[s, :] = h + g * _dot(p_ref[s, :].astype(BF16), w_pp_ref[...])


def _out_ffn(x2, mla, mlstm, p2, w_out, g_post, g_fpre, g_fpost, w_gate, w_up, w_down, w_pg, w_pp):
    t = x2.shape[0]
    tm = TM_OUT
    row = lambda width: pl.BlockSpec((tm, width), lambda i: (i, 0))
    consts = (w_out, g_post, g_fpre, g_fpost, w_gate, w_up, w_down, w_pg, w_pp)
    return pl.pallas_call(
        _out_ffn_body,
        out_shape=jax.ShapeDtypeStruct((t, D_MODEL), F32),
        grid=(t // tm,),
        in_specs=[row(D_MODEL), pl.BlockSpec((tm // TILE_T, MLA_WIDTH, TILE_T), lambda i: (i, 0, 0)),
                  pl.BlockSpec((tm // TILE_T, MLSTM_WIDTH, TILE_T), lambda i: (i, 0, 0)), row(PLE_DIM)]
        + [_const_spec(c.shape) for c in consts],
        out_specs=row(D_MODEL),
        compiler_params=pltpu.CompilerParams(dimension_semantics=("arbitrary",),
                                             vmem_limit_bytes=VMEM_LIMIT),
        name="out_ffn",
    )(x2, mla, mlstm, p2, *consts)


def _pack_w_in(w_in):
    o_krope = Q_LORA_RANK + KV_LORA_RANK
    o_mq = o_krope + QK_ROPE_DIM
    o_mv = o_mq + 2 * MLSTM_WIDTH
    o_mo = o_mv + MLSTM_WIDTH
    o_gates = o_mo + MLSTM_WIDTH
    main = jnp.concatenate([w_in[:, :o_krope], w_in[:, o_mq:o_mv]], axis=1)
    gap = jnp.zeros((D_MODEL, SUBLANES - MLSTM_HEADS), w_in.dtype)
    tail = jnp.concatenate([w_in[:, o_krope:o_mq], w_in[:, o_gates:o_gates + MLSTM_HEADS], gap,
                            w_in[:, o_gates + MLSTM_HEADS:], gap], axis=1).T
    tail = jnp.pad(tail, ((0, LANES - tail.shape[0]), (0, 0)))
    mv_t = w_in[:, o_mv:o_mo].T.reshape(MLSTM_HEADS, MLSTM_HEAD_DIM, D_MODEL)
    mv_t = jnp.pad(mv_t, ((0, 0), (0, V_ROWS - MLSTM_HEAD_DIM), (0, 0)))
    mvo_t = jnp.concatenate([mv_t.reshape(MLSTM_HEADS * V_ROWS, D_MODEL), w_in[:, o_mo:o_gates].T], axis=0)
    return main.astype(BF16), tail.astype(BF16), mvo_t.astype(BF16)


def _pack_w_uq_t(w_uq):
    w = w_uq.T.reshape(MLA_HEADS, QK_HEAD_DIM, Q_LORA_RANK)
    w = jnp.pad(w, ((0, 0), (0, QK_PAD - QK_HEAD_DIM), (0, 0)))
    return w.reshape(MLA_HEADS * QK_PAD, Q_LORA_RANK).astype(BF16)


def _pack_w_ukv(w_ukv):
    w = w_ukv.reshape(KV_LORA_RANK, MLA_HEADS, QK_NOPE_DIM + V_HEAD_DIM)
    w_uk = w[:, :, :QK_NOPE_DIM].reshape(KV_LORA_RANK, MLA_HEADS * QK_NOPE_DIM)
    w_uv_t = jnp.transpose(w[:, :, QK_NOPE_DIM:], (1, 2, 0))
    w_uv_t = jnp.pad(w_uv_t, ((0, 0), (0, V_ROWS - V_HEAD_DIM), (0, 0)))
    return w_uk.astype(BF16), w_uv_t.reshape(MLA_HEADS * V_ROWS, KV_LORA_RANK).astype(BF16)


def _layer(h, p_i, pos_row, invf, attn_pre_norm, attn_post_norm, w_in, q_norm, kv_norm, w_uq, w_ukv,
           conv_w, conv_b, gate_bias_i, gate_bias_f, mlstm_norm, w_out, ffn_pre_norm, ffn_post_norm,
           w_gate, w_up, w_down, w_ple_proj, w_ple_gate, batch, seq):
    row = lambda a: a.reshape(1, -1)
    w_in_p, w_tail, w_mv_t = _pack_w_in(w_in)
    w_uk, w_uv_t = _pack_w_ukv(w_ukv)
    gap = jnp.zeros((SUBLANES - MLSTM_HEADS,), F32)
    gbias = jnp.concatenate([gate_bias_i, gap, gate_bias_f, gap]).astype(F32).reshape(2 * SUBLANES, 1)
    vones = jnp.zeros((MLA_HEADS, V_ROWS, 1), F32).at[:, V_HEAD_DIM, 0].set(1.0)
    qt, k, vt, mqt, mk, mvt, mo, grow, keycol = _in_proj(
        h, pos_row, invf, row(attn_pre_norm), w_in_p, w_tail, row(q_norm), row(kv_norm),
        _pack_w_uq_t(w_uq), w_uk, w_uv_t, w_mv_t, vones.reshape(MLA_HEADS * V_ROWS, 1), conv_w,
        row(conv_b), gbias, seq)
    mla = _attention(qt, k, vt, batch, seq)
    mlstm = _mlstm(mqt, mk, mvt, mo, grow, keycol, mlstm_norm.reshape(-1, 1), batch, seq)
    return _out_ffn(h, mla, mlstm, p_i, w_out.astype(BF16), row(attn_post_norm), row(ffn_pre_norm),
                    row(ffn_post_norm), w_gate.astype(BF16), w_up.astype(BF16),
                    w_down.astype(BF16), w_ple_gate.astype(BF16), w_ple_proj.astype(BF16))


@jax.jit
def kernel(x, p, positions, attn_pre_norm, attn_post_norm, w_in, q_norm, kv_norm, w_uq, w_ukv,
           conv_w, conv_b, gate_bias_i, gate_bias_f, mlstm_norm, w_out, ffn_pre_norm, ffn_post_norm,
           w_gate, w_up, w_down, w_ple_proj, w_ple_gate):
    batch, seq, _ = x.shape
    t = batch * seq
    depth = p.shape[0]
    inv_freq = ROPE_THETA ** (-jnp.arange(0, QK_ROPE_DIM, 2, dtype=F32) / QK_ROPE_DIM)
    invf = inv_freq.reshape(ROPE_HALF, 1)
    pos_row = positions.astype(F32).reshape(1, t)
    h = x.reshape(t, D_MODEL)
    for i in range(depth):
        h = _layer(h, p[i].reshape(t, PLE_DIM), pos_row, invf, attn_pre_norm[i], attn_post_norm[i],
                   w_in[i], q_norm[i], kv_norm[i], w_uq[i], w_ukv[i], conv_w[i], conv_b[i],
                   gate_bias_i[i], gate_bias_f[i], mlstm_norm[i], w_out[i], ffn_pre_norm[i],
                   ffn_post_norm[i], w_gate[i], w_up[i], w_down[i], w_ple_proj[i], w_ple_gate[i],
                   batch, seq)
    return h.reshape(batch, seq, D_MODEL)
```

```python
import functools
import math

import jax
import jax.numpy as jnp
from jax import lax
from jax.experimental import pallas as pl
from jax.experimental.pallas import tpu as pltpu

F32 = jnp.float32
BF16 = jnp.bfloat16

D_MODEL = 1024
PLE_DIM = 256
MLA_HEADS = 4
QK_NOPE_DIM = 128
QK_ROPE_DIM = 64
QK_HEAD_DIM = QK_NOPE_DIM + QK_ROPE_DIM
V_HEAD_DIM = 128
Q_LORA_RANK = 256
KV_LORA_RANK = 128
ROPE_THETA = 10000.0
MLA_WIDTH = MLA_HEADS * V_HEAD_DIM
MLSTM_HEADS = 4
MLSTM_HEAD_DIM = 128
MLSTM_WIDTH = MLSTM_HEADS * MLSTM_HEAD_DIM
CONV_WIDTH = 4
CONV_STRIDE = 4
D_FF = 2816
EPS = 1e-6

LANES = 128
SUBLANES = 8
BF16_ROWS = 16
QK_PAD = 2 * LANES
V_ROWS = V_HEAD_DIM + BF16_ROWS
ROPE_HALF = QK_ROPE_DIM // 2
VMEM_LIMIT = 56 * 1024 * 1024

C_Q = 0
C_KV = C_Q + Q_LORA_RANK
C_MQK = C_KV + KV_LORA_RANK
C_MO = C_MQK + 2 * MLSTM_WIDTH
D_IN_PACKED = C_MO + MLSTM_WIDTH
GATE_I = QK_ROPE_DIM
GATE_F = GATE_I + SUBLANES
ROW_KEY, ROW_KMAX, ROW_B, GATE_ROWS = 0, SUBLANES, 2 * SUBLANES, 3 * SUBLANES

TILE_T = 512
TM_IN = 512
SUB_IN = 256
TQ = TILE_T
HEADS_PER_STEP = 2
MAX_STALE_EXCESS = 64.0
CHUNK = 256
MLSTM_TILES = 2
TM_OUT = 512
SUB_OUT = 256
FF_CHUNKS = (1024, 1024, 768)


def _rms(x, g):
    return x * lax.rsqrt(jnp.mean(x * x, axis=-1, keepdims=True) + EPS) * g


def _dot(a, b):
    return jnp.dot(a, b, preferred_element_type=F32)


def _dot_nt(a, b):
    return lax.dot_general(a, b, (((1,), (1,)), ((), ())), preferred_element_type=F32)


def _dot_tn(a, b):
    return lax.dot_general(a, b, (((0,), (0,)), ((), ())), preferred_element_type=F32)


def _const_spec(shape):
    return pl.BlockSpec(shape, lambda *_: (0,) * len(shape), pipeline_mode=pl.Buffered(1))


def _in_proj_body(x_ref, pos_ref, invf_ref, g_pre_ref, w_in_ref, w_tail_ref, qn_ref, kvn_ref,
                  w_uq_ref, w_uk_ref, w_uv_ref, w_mv_ref, vones_ref, conv_w_ref, conv_b_ref, gbias_ref,
                  qt_out, k_out, vt_out, mqt_out, mk_out, mvt_out, mo_out, grow_out, keycol_out,
                  ext_ref, stage_ref, *, tm, sub, tiles_per_seq, chunk):
    assert tm % sub == 0 and sub % chunk == 0
    i = pl.program_id(0)
    halo = SUBLANES

    @pl.when(lax.rem(i, tiles_per_seq) == 0)
    def _():
        ext_ref[:, 0:halo, :] = jnp.zeros((ext_ref.shape[0], halo, LANES), F32)

    for r0 in range(0, tm, sub):
        _in_proj_rows(r0, sub, halo, chunk, x_ref, pos_ref, invf_ref, g_pre_ref, w_in_ref, w_tail_ref,
                      qn_ref, kvn_ref, w_uq_ref, w_uk_ref, w_uv_ref, w_mv_ref, vones_ref, conv_w_ref,
                      conv_b_ref, gbias_ref, qt_out, k_out, vt_out, mqt_out, mk_out, mvt_out, mo_out,
                      grow_out, keycol_out, ext_ref, stage_ref)
    ext_ref[:, 0:halo, :] = ext_ref[:, tm:tm + halo, :]


def _in_proj_rows(r0, ts, halo, chunk, x_ref, pos_ref, invf_ref, g_pre_ref, w_in_ref, w_tail_ref,
                  qn_ref, kvn_ref, w_uq_ref, w_uk_ref, w_uv_ref, w_mv_ref, vones_ref, conv_w_ref,
                  conv_b_ref, gbias_ref, qt_out, k_out, vt_out, mqt_out, mk_out, mvt_out, mo_out,
                  grow_out, keycol_out, ext_ref, stage_ref):
    rows = slice(r0, r0 + ts)
    tile, lanes = r0 // TILE_T, slice(r0 % TILE_T, r0 % TILE_T + ts)
    u = _rms(x_ref[rows, :], g_pre_ref[...]).astype(BF16)

    cq_raw = _dot(u, w_in_ref[:, C_Q:C_Q + Q_LORA_RANK])
    ckv_raw = _dot(u, w_in_ref[:, C_KV:C_KV + KV_LORA_RANK])
    tail_t = _dot_nt(w_tail_ref[...], u)
    z = _dot(u, w_in_ref[:, C_MQK:C_MQK + 2 * MLSTM_WIDTH])
    for c in range(2 * MLSTM_WIDTH // LANES):
        ext_ref[c, halo + r0:halo + r0 + ts, :] = z[:, c * LANES:(c + 1) * LANES]
    mvt_out[tile, :, lanes] = (_dot_nt(w_mv_ref[...], u) + vones_ref[...]).astype(BF16)
    mo_out[rows, :] = jax.nn.sigmoid(_dot(u, w_in_ref[:, C_MO:C_MO + MLSTM_WIDTH])).astype(BF16)
    cq = _rms(cq_raw, qn_ref[...]).astype(BF16)
    ckv = _rms(ckv_raw, kvn_ref[...]).astype(BF16)
    qt = _dot_nt(w_uq_ref[...], cq)
    kn = _dot(ckv, w_uk_ref[...])
    vt_out[tile, :, lanes] = (_dot_nt(w_uv_ref[...], ckv) + vones_ref[...]).astype(BF16)

    ang = invf_ref[...] * pos_ref[:, rows]
    cos = jnp.cos(ang)
    sin = jnp.sin(ang)

    def rope_t(t1, t2):
        return t1 * cos - t2 * sin, t2 * cos + t1 * sin

    scale = QK_HEAD_DIM ** -0.5 * math.log2(math.e)
    for h in range(MLA_HEADS):
        o = h * QK_PAD
        pe = o + QK_NOPE_DIM
        r1, r2 = rope_t(qt[pe:pe + ROPE_HALF], qt[pe + ROPE_HALF:pe + QK_ROPE_DIM])
        qt_out[tile, o:pe, lanes] = (qt[o:pe] * scale).astype(BF16)
        qt_out[tile, pe:pe + ROPE_HALF, lanes] = (r1 * scale).astype(BF16)
        qt_out[tile, pe + ROPE_HALF:pe + QK_ROPE_DIM, lanes] = (r2 * scale).astype(BF16)
        qt_out[tile, pe + QK_ROPE_DIM:o + QK_PAD, lanes] = jnp.zeros((QK_PAD - QK_HEAD_DIM, ts), BF16)

    r1, r2 = rope_t(tail_t[0:ROPE_HALF], tail_t[ROPE_HALF:QK_ROPE_DIM])
    kpe_t = jnp.concatenate([r1, r2, jnp.zeros((LANES - QK_ROPE_DIM, ts), F32)], axis=0)
    kpe = kpe_t.T.astype(BF16)
    for h in range(MLA_HEADS):
        o = h * QK_PAD
        k_out[rows, o:o + LANES] = kn[:, h * LANES:(h + 1) * LANES].astype(BF16)
        k_out[rows, o + LANES:o + QK_PAD] = kpe

    gates = tail_t[GATE_I:GATE_I + 2 * SUBLANES] + gbias_ref[...]
    pos_in_chunk = lax.rem(r0 + lax.broadcasted_iota(jnp.int32, (SUBLANES, ts), 1), chunk)

    def chunk_scan(v, op, identity):
        shift = 1
        while shift < chunk:
            v = op(v, jnp.where(pos_in_chunk >= shift, pltpu.roll(v, shift, 1), identity))
            shift *= 2
        return v

    b8 = chunk_scan(jax.nn.log_sigmoid(gates[SUBLANES:]), jnp.add, 0.0)
    key8 = gates[:SUBLANES] - b8
    grow_out[:, rows] = jnp.concatenate([key8, chunk_scan(key8, jnp.maximum, -jnp.inf), b8], axis=0)
    keycol_out[rows, :] = jnp.concatenate([key8, jnp.zeros((LANES - SUBLANES, ts), F32)], axis=0).T

    n = ts // CONV_STRIDE
    for c in range(2 * MLSTM_WIDTH // LANES):
        cols = slice(c * LANES, (c + 1) * LANES)
        taps = {q: ext_ref[c, pl.ds(halo + r0 + q, n, stride=CONV_STRIDE), :]
                for q in range(1 - CONV_WIDTH, CONV_STRIDE)}
        for r in range(CONV_STRIDE):
            acc = conv_b_ref[:, cols]
            for j in range(CONV_WIDTH):
                acc = acc + conv_w_ref[j:j + 1, cols] * taps[r - (CONV_WIDTH - 1) + j]
            stage_ref[c, pl.ds(r0 + r, n, stride=CONV_STRIDE), :] = acc * jax.nn.sigmoid(acc)
    for c in range(MLSTM_WIDTH // LANES):
        cols = slice(c * LANES, (c + 1) * LANES)
        mqt_out[tile, cols, lanes] = stage_ref[c, rows, :].T.astype(BF16)
        k_slab = stage_ref[MLSTM_WIDTH // LANES + c, rows, :]
        mk_out[rows, cols] = (k_slab * (MLSTM_HEAD_DIM ** -0.5)).astype(BF16)


def _in_proj(x2, pos_row, invf, g_pre, w_in_p, w_tail, qn, kvn, w_uq_t, w_uk, w_uv_t, w_mv_t, vones,
             conv_w, conv_b, gbias, seq):
    t = x2.shape[0]
    tm = TM_IN
    nt = t // TILE_T
    row = lambda width: pl.BlockSpec((tm, width), lambda i: (i, 0))
    tile_t = lambda rows: pl.BlockSpec((tm // TILE_T, rows, TILE_T), lambda i: (i, 0, 0))
    bf = lambda *shape: jax.ShapeDtypeStruct(shape, BF16)
    out_shape = [
        bf(nt, MLA_HEADS * QK_PAD, TILE_T), bf(t, MLA_HEADS * QK_PAD), bf(nt, MLA_HEADS * V_ROWS, TILE_T),
        bf(nt, MLSTM_WIDTH, TILE_T), bf(t, MLSTM_WIDTH), bf(nt, MLSTM_HEADS * V_ROWS, TILE_T),
        bf(t, MLSTM_WIDTH), jax.ShapeDtypeStruct((GATE_ROWS, t), F32),
        jax.ShapeDtypeStruct((t, LANES), F32),
    ]
    out_specs = [
        tile_t(MLA_HEADS * QK_PAD), row(MLA_HEADS * QK_PAD), tile_t(MLA_HEADS * V_ROWS),
        tile_t(MLSTM_WIDTH), row(MLSTM_WIDTH), tile_t(MLSTM_HEADS * V_ROWS), row(MLSTM_WIDTH),
        pl.BlockSpec((GATE_ROWS, tm), lambda i: (0, i)), row(LANES),
    ]
    consts = (invf, g_pre, w_in_p, w_tail, qn, kvn, w_uq_t, w_uk, w_uv_t, w_mv_t, vones, conv_w,
              conv_b, gbias)
    return pl.pallas_call(
        functools.partial(_in_proj_body, tm=tm, sub=SUB_IN, tiles_per_seq=seq // tm, chunk=CHUNK),
        out_shape=out_shape,
        grid=(t // tm,),
        in_specs=[row(D_MODEL), pl.BlockSpec((1, tm), lambda i: (0, i))]
        + [_const_spec(c.shape) for c in consts],
        out_specs=out_specs,
        scratch_shapes=[pltpu.VMEM((2 * MLSTM_WIDTH // LANES, tm + 2 * SUBLANES, LANES), F32),
                        pltpu.VMEM((2 * MLSTM_WIDTH // LANES, tm, LANES), F32)],
        compiler_params=pltpu.CompilerParams(dimension_semantics=("arbitrary",),
                                             vmem_limit_bytes=VMEM_LIMIT),
        name="in_proj",
    )(x2, pos_row, *consts)


def _attn_body(qt_ref, k_ref, vt_ref, o_ref, m_ref, use_ref, viol_ref, acc_ref, pa_ref, pb_ref,
               ua_ref, ub_ref, *, tq, heads):
    qi = pl.program_id(2)
    hq = [slice(h * QK_PAD, (h + 1) * QK_PAD) for h in range(heads)]
    hv = [slice(h * V_ROWS, (h + 1) * V_ROWS) for h in range(heads)]
    stat = m_ref.shape[1:]

    def qk(ki, h):
        start = pl.multiple_of(ki * tq, tq)
        return _dot(k_ref[pl.ds(start, tq), hq[h]], qt_ref[0, hq[h], :])

    def causal(s):
        r = lax.broadcasted_iota(jnp.int32, s.shape, 0)
        c = lax.broadcasted_iota(jnp.int32, s.shape, 1)
        return jnp.where(r <= c, s, -jnp.inf)

    def fast_scores(ki, p_ref, u_ref):
        for h in range(heads):
            s = qk(ki, h)
            m_use = m_ref[h, 0:1, :]
            p_ref[h] = jnp.exp2(s - m_use).astype(BF16)
            cmax = jnp.max(s, axis=0, keepdims=True)
            excess = jnp.where(qi > 0, cmax - m_use, 0.0)
            u_ref[h] = jnp.broadcast_to(m_use, stat)
            viol_ref[h] = jnp.broadcast_to(jnp.maximum(viol_ref[h, 0:1, :], excess), stat)
            m_ref[h] = jnp.broadcast_to(jnp.maximum(m_use, cmax), stat)

    def fast_pv(ki, p_ref, u_ref):
        for h in range(heads):
            m_use = u_ref[h, 0:1, :]
            alpha = jnp.exp2(use_ref[h, 0:1, :] - m_use)
            acc_ref[h] = alpha * acc_ref[h] + _dot(vt_ref[ki, hv[h], :], p_ref[h])
            use_ref[h] = jnp.broadcast_to(m_use, stat)

    def exact_block(ki, diagonal):
        for h in range(heads):
            s = causal(qk(ki, h)) if diagonal else qk(ki, h)
            m_old = m_ref[h, 0:1, :]
            m_new = jnp.maximum(m_old, jnp.max(s, axis=0, keepdims=True))
            p = jnp.exp2(s - m_new).astype(BF16)
            acc_ref[h] = jnp.exp2(m_old - m_new) * acc_ref[h] + _dot(vt_ref[ki, hv[h], :], p)
            m_ref[h] = jnp.broadcast_to(m_new, stat)

    def start():
        viol_ref[...] = jnp.zeros(viol_ref.shape, F32)
        s_diag = [causal(qk(qi, h)) for h in range(heads)]
        for h in range(heads):
            cmax = jnp.broadcast_to(jnp.max(s_diag[h], axis=0, keepdims=True), stat)
            m_ref[h] = cmax
            use_ref[h] = cmax
        fast_scores(0, pa_ref, ua_ref)
        for h in range(heads):
            p = jnp.exp2(s_diag[h] - use_ref[h, 0:1, :]).astype(BF16)
            acc_ref[h] = _dot(vt_ref[qi, hv[h], :], p)

    bufs = ((pa_ref, ua_ref), (pb_ref, ub_ref))

    def run(first, count, score_next):
        for i in range(count):
            if i + 1 < count or score_next:
                fast_scores(first + i + 1, *bufs[(i + 1) % 2])
            fast_pv(first + i, *bufs[i % 2])

    def four(j, carry):
        run(4 * j, 4, True)
        return carry

    for short in range(5):
        @pl.when(qi == short)
        def _(short=short):
            start()
            run(0, short, False)

    @pl.when(qi > 4)
    def _():
        start()
        run(0, 4, True)

    n_loop = jnp.maximum(qi - 1, 0) // 4
    lax.fori_loop(1, n_loop, four, 0)
    for left in range(1, 5):
        @pl.when(jnp.logical_and(qi > 4, qi - 4 * n_loop == left))
        def _(left=left):
            run(4 * n_loop, left, False)

    @pl.when(jnp.max(viol_ref[...]) > MAX_STALE_EXCESS)
    def _():
        m_ref[...] = jnp.full(m_ref.shape, -jnp.inf, F32)
        acc_ref[...] = jnp.zeros(acc_ref.shape, F32)

        def blk(ki, carry):
            exact_block(ki, False)
            return carry

        lax.fori_loop(0, qi, blk, 0)
        exact_block(qi, True)

    for h in range(heads):
        acc = acc_ref[h]
        out_t = acc[0:V_HEAD_DIM] / acc[V_HEAD_DIM:V_HEAD_DIM + 1]
        o_ref[0, h * V_HEAD_DIM:(h + 1) * V_HEAD_DIM, :] = out_t.astype(o_ref.dtype)


def _attention(qt, k, vt, batch, seq):
    t = k.shape[0]
    tq = TQ
    nq = seq // tq
    heads = HEADS_PER_STEP
    return pl.pallas_call(
        functools.partial(_attn_body, tq=tq, heads=heads),
        out_shape=jax.ShapeDtypeStruct((t // tq, MLA_WIDTH, tq), BF16),
        grid=(batch, MLA_HEADS // heads, nq),
        in_specs=[
            pl.BlockSpec((1, heads * QK_PAD, tq), lambda b, g, i: (b * nq + i, g, 0)),
            pl.BlockSpec((seq, heads * QK_PAD), lambda b, g, i: (b, g)),
            pl.BlockSpec((nq, heads * V_ROWS, tq), lambda b, g, i: (b, g, 0)),
        ],
        out_specs=pl.BlockSpec((1, heads * V_HEAD_DIM, tq), lambda b, g, i: (b * nq + i, g, 0)),
        scratch_shapes=[pltpu.VMEM((heads, SUBLANES, tq), F32),
                        pltpu.VMEM((heads, SUBLANES, tq), F32),
                        pltpu.VMEM((heads, SUBLANES, tq), F32),
                        pltpu.VMEM((heads, V_ROWS, tq), F32),
                        pltpu.VMEM((heads, tq, tq), BF16),
                        pltpu.VMEM((heads, tq, tq), BF16),
                        pltpu.VMEM((heads, SUBLANES, tq), F32),
                        pltpu.VMEM((heads, SUBLANES, tq), F32)],
        compiler_params=pltpu.CompilerParams(
            dimension_semantics=("arbitrary", "arbitrary", "arbitrary"),
            vmem_limit_bytes=VMEM_LIMIT),
        name="mla_attention",
    )(qt, k, vt)


def _mlstm_body(qt_ref, k_ref, vt_ref, og_ref, grow_ref, keycol_ref, norm_ref, out_ref, ct_ref,
                m_ref, *, chunk, chunks_per_step):
    @pl.when(pl.program_id(1) == 0)
    def _():
        ct_ref[...] = jnp.zeros(ct_ref.shape, F32)
        m_ref[...] = jnp.zeros(m_ref.shape, F32)

    heads = range(MLSTM_HEADS)
    hs = [slice(h * MLSTM_HEAD_DIM, (h + 1) * MLSTM_HEAD_DIM) for h in heads]
    vs = [slice(h * V_ROWS, (h + 1) * V_ROWS) for h in heads]
    tok = [slice(c * chunk, (c + 1) * chunk) for c in range(chunks_per_step)]
    ttok = [(c * chunk // TILE_T, slice(c * chunk % TILE_T, c * chunk % TILE_T + chunk))
            for c in range(chunks_per_step)]
    r = lax.broadcasted_iota(jnp.int32, (chunk, chunk), 0)
    c = lax.broadcasted_iota(jnp.int32, (chunk, chunk), 1)
    upper = r <= c

    s_raw = [[_dot(k_ref[t, hs[h]], qt_ref[tile, hs[h], lanes]) for h in heads]
             for t, (tile, lanes) in zip(tok, ttok)]
    ct = [ct_ref[h] for h in heads]
    m_prev = [m_ref[h, 0:1, 0:1] for h in heads]
    for ci, (t, (tile, lanes)) in enumerate(zip(tok, ttok)):
        qt = [qt_ref[tile, hs[h], lanes] for h in heads]
        vt = [vt_ref[tile, vs[h], lanes] for h in heads]
        cq = [_dot(ct[h].astype(BF16), qt[h]) for h in heads]
        grow = grow_ref[:, t]
        mm = [jnp.maximum(m_prev[h], grow[ROW_KMAX + h:ROW_KMAX + h + 1]) for h in heads]
        b_row = [grow[ROW_B + h:ROW_B + h + 1] for h in heads]
        inter = [jnp.exp(m_prev[h] - mm[h]) for h in heads]
        for h in heads:
            mm_last = mm[h][:, chunk - 1:chunk]
            w_row = jnp.exp(grow[ROW_KEY + h:ROW_KEY + h + 1] - mm_last)
            vtw = (vt[h].astype(F32) * w_row).astype(BF16)
            ct[h] = jnp.exp(m_prev[h] - mm_last) * ct[h] + _dot(vtw, k_ref[t, hs[h]])
            m_prev[h] = b_row[h][:, chunk - 1:chunk] + mm_last
        keycol = keycol_ref[t, :]
        for h in heads:
            p = jnp.exp(jnp.where(upper, keycol[:, h:h + 1] - mm[h], -jnp.inf))
            st = (s_raw[ci][h] * p).astype(BF16)
            tot = _dot(vt[h], st) + inter[h] * cq[h]
            den = tot[MLSTM_HEAD_DIM:MLSTM_HEAD_DIM + 1]
            ht = tot[0:MLSTM_HEAD_DIM] / jnp.maximum(jnp.abs(den), jnp.exp(-(b_row[h] + mm[h])))
            hn_t = ht * lax.rsqrt(jnp.mean(ht * ht, axis=0, keepdims=True) + EPS)
            out_ref[t, hs[h]] = (hn_t.T * norm_ref[0:1, hs[h]]
                                 * og_ref[t, hs[h]].astype(F32)).astype(out_ref.dtype)
    for h in heads:
        ct_ref[h] = ct[h]
        m_ref[h] = jnp.broadcast_to(m_prev[h], m_ref.shape[1:])


def _mlstm(mqt, mk, mvt, mo, grow, keycol, norm, batch, seq):
    t = mk.shape[0]
    tile = MLSTM_TILES * TILE_T
    ns = seq // tile
    row = lambda width: pl.BlockSpec((tile, width), lambda b, j: (b * ns + j, 0))
    tile_t = lambda rows: pl.BlockSpec((MLSTM_TILES, rows, TILE_T), lambda b, j: (b * ns + j, 0, 0))
    return pl.pallas_call(
        functools.partial(_mlstm_body, chunk=CHUNK, chunks_per_step=tile // CHUNK),
        out_shape=jax.ShapeDtypeStruct((t, MLSTM_WIDTH), BF16),
        grid=(batch, ns),
        in_specs=[tile_t(MLSTM_WIDTH), row(MLSTM_WIDTH), tile_t(MLSTM_HEADS * V_ROWS),
                  row(MLSTM_WIDTH), pl.BlockSpec((GATE_ROWS, tile), lambda b, j: (0, b * ns + j)),
                  row(LANES), _const_spec(norm.shape)],
        out_specs=row(MLSTM_WIDTH),
        scratch_shapes=[
            pltpu.VMEM((MLSTM_HEADS, V_ROWS, MLSTM_HEAD_DIM), F32),
            pltpu.VMEM((MLSTM_HEADS, SUBLANES, LANES), F32),
        ],
        compiler_params=pltpu.CompilerParams(dimension_semantics=("arbitrary", "arbitrary"),
                                             vmem_limit_bytes=VMEM_LIMIT),
        name="mlstm",
    )(mqt, mk, mvt, mo, grow, keycol, norm)


def _out_ffn_body(x_ref, mla_ref, mlstm_ref, p_ref, w_out_ref, g_post_ref, g_fpre_ref, g_fpost_ref,
                  w_gate_ref, w_up_ref, w_down_ref, w_pg_ref, w_pp_ref, o_ref):
    tm = x_ref.shape[0]
    subs = [slice(r0, r0 + SUB_OUT) for r0 in range(0, tm, SUB_OUT)]
    mla_t = [mla_ref[s.start // TILE_T, :, s.start % TILE_T:s.start % TILE_T + SUB_OUT] for s in subs]
    mix = [_dot_tn(a, w_out_ref[0:MLA_WIDTH, :])
           + _dot(mlstm_ref[s, :], w_out_ref[MLA_WIDTH:MLA_WIDTH + MLSTM_WIDTH, :])
           for s, a in zip(subs, mla_t)]
    h1 = [x_ref[s, :] + _rms(m, g_post_ref[...]) for s, m in zip(subs, mix)]
    f = [_rms(h, g_fpre_ref[...]).astype(BF16) for h in h1]
    ffn = [None] * len(subs)
    off = 0
    for width in FF_CHUNKS:
        cols = slice(off, off + width)
        gate = [_dot(v, w_gate_ref[:, cols]) for v in f]
        up = [_dot(v, w_up_ref[:, cols]) for v in f]
        act = [(g * jax.nn.sigmoid(g) * u).astype(BF16) for g, u in zip(gate, up)]
        part = [_dot(a, w_down_ref[cols, :]) for a in act]
        ffn = [p if acc is None else acc + p for acc, p in zip(ffn, part)]
        off += width
    h2 = [h + _rms(v, g_fpost_ref[...]) for h, v in zip(h1, ffn)]
    pgate = [jax.nn.sigmoid(_dot(h.astype(BF16), w_pg_ref[...])) for h in h2]
    for s, h, g in zip(subs, h2, pgate):
        o_ref[s, :] = h + g * _dot(p_ref[s, :].astype(BF16), w_pp_ref[...])


def _out_ffn(x2, mla, mlstm, p2, w_out, g_post, g_fpre, g_fpost, w_gate, w_up, w_down, w_pg, w_pp):
    t = x2.shape[0]
    tm = TM_OUT
    row = lambda width: pl.BlockSpec((tm, width), lambda i: (i, 0))
    consts = (w_out, g_post, g_fpre, g_fpost, w_gate, w_up, w_down, w_pg, w_pp)
    return pl.pallas_call(
        _out_ffn_body,
        out_shape=jax.ShapeDtypeStruct((t, D_MODEL), F32),
        grid=(t // tm,),
        in_specs=[row(D_MODEL), pl.BlockSpec((tm // TILE_T, MLA_WIDTH, TILE_T), lambda i: (i, 0, 0)),
                  row(MLSTM_WIDTH), row(PLE_DIM)]
        + [_const_spec(c.shape) for c in consts],
        out_specs=row(D_MODEL),
        compiler_params=pltpu.CompilerParams(dimension_semantics=("arbitrary",),
                                             vmem_limit_bytes=VMEM_LIMIT),
        name="out_ffn",
    )(x2, mla, mlstm, p2, *consts)


def _pack_w_in(w_in):
    o_krope = Q_LORA_RANK + KV_LORA_RANK
    o_mq = o_krope + QK_ROPE_DIM
    o_mv = o_mq + 2 * MLSTM_WIDTH
    o_mo = o_mv + MLSTM_WIDTH
    o_gates = o_mo + MLSTM_WIDTH
    main = jnp.concatenate([w_in[:, :o_krope], w_in[:, o_mq:o_mv], w_in[:, o_mo:o_gates]], axis=1)
    gap = jnp.zeros((D_MODEL, SUBLANES - MLSTM_HEADS), w_in.dtype)
    tail = jnp.concatenate([w_in[:, o_krope:o_mq], w_in[:, o_gates:o_gates + MLSTM_HEADS], gap,
                            w_in[:, o_gates + MLSTM_HEADS:], gap], axis=1).T
    tail = jnp.pad(tail, ((0, LANES - tail.shape[0]), (0, 0)))
    mv_t = w_in[:, o_mv:o_mo].T.reshape(MLSTM_HEADS, MLSTM_HEAD_DIM, D_MODEL)
    mv_t = jnp.pad(mv_t, ((0, 0), (0, V_ROWS - MLSTM_HEAD_DIM), (0, 0)))
    return main.astype(BF16), tail.astype(BF16), mv_t.reshape(MLSTM_HEADS * V_ROWS, D_MODEL).astype(BF16)


def _pack_w_uq_t(w_uq):
    w = w_uq.T.reshape(MLA_HEADS, QK_HEAD_DIM, Q_LORA_RANK)
    w = jnp.pad(w, ((0, 0), (0, QK_PAD - QK_HEAD_DIM), (0, 0)))
    return w.reshape(MLA_HEADS * QK_PAD, Q_LORA_RANK).astype(BF16)


def _pack_w_ukv(w_ukv):
    w = w_ukv.reshape(KV_LORA_RANK, MLA_HEADS, QK_NOPE_DIM + V_HEAD_DIM)
    w_uk = w[:, :, :QK_NOPE_DIM].reshape(KV_LORA_RANK, MLA_HEADS * QK_NOPE_DIM)
    w_uv_t = jnp.transpose(w[:, :, QK_NOPE_DIM:], (1, 2, 0))
    w_uv_t = jnp.pad(w_uv_t, ((0, 0), (0, V_ROWS - V_HEAD_DIM), (0, 0)))
    return w_uk.astype(BF16), w_uv_t.reshape(MLA_HEADS * V_ROWS, KV_LORA_RANK).astype(BF16)


def _layer(h, p_i, pos_row, invf, attn_pre_norm, attn_post_norm, w_in, q_norm, kv_norm, w_uq, w_ukv,
           conv_w, conv_b, gate_bias_i, gate_bias_f, mlstm_norm, w_out, ffn_pre_norm, ffn_post_norm,
           w_gate, w_up, w_down, w_ple_proj, w_ple_gate, batch, seq):
    row = lambda a: a.reshape(1, -1)
    w_in_p, w_tail, w_mv_t = _pack_w_in(w_in)
    w_uk, w_uv_t = _pack_w_ukv(w_ukv)
    gap = jnp.zeros((SUBLANES - MLSTM_HEADS,), F32)
    gbias = jnp.concatenate([gate_bias_i, gap, gate_bias_f, gap]).astype(F32).reshape(2 * SUBLANES, 1)
    vones = jnp.zeros((MLA_HEADS, V_ROWS, 1), F32).at[:, V_HEAD_DIM, 0].set(1.0)
    qt, k, vt, mqt, mk, mvt, mo, grow, keycol = _in_proj(
        h, pos_row, invf, row(attn_pre_norm), w_in_p, w_tail, row(q_norm), row(kv_norm),
        _pack_w_uq_t(w_uq), w_uk, w_uv_t, w_mv_t, vones.reshape(MLA_HEADS * V_ROWS, 1), conv_w,
        row(conv_b), gbias, seq)
    mla = _attention(qt, k, vt, batch, seq)
    mlstm = _mlstm(mqt, mk, mvt, mo, grow, keycol, row(mlstm_norm), batch, seq)
    return _out_ffn(h, mla, mlstm, p_i, w_out.astype(BF16), row(attn_post_norm), row(ffn_pre_norm),
                    row(ffn_post_norm), w_gate.astype(BF16), w_up.astype(BF16),
                    w_down.astype(BF16), w_ple_gate.astype(BF16), w_ple_proj.astype(BF16))


@jax.jit
def kernel(x, p, positions, attn_pre_norm, attn_post_norm, w_in, q_norm, kv_norm, w_uq, w_ukv,
           conv_w, conv_b, gate_bias_i, gate_bias_f, mlstm_norm, w_out, ffn_pre_norm, ffn_post_norm,
           w_gate, w_up, w_down, w_ple_proj, w_ple_gate):
    batch, seq, _ = x.shape
    t = batch * seq
    depth = p.shape[0]
    inv_freq = ROPE_THETA ** (-jnp.arange(0, QK_ROPE_DIM, 2, dtype=F32) / QK_ROPE_DIM)
    invf = inv_freq.reshape(ROPE_HALF, 1)
    pos_row = positions.astype(F32).reshape(1, t)
    h = x.reshape(t, D_MODEL)
    for i in range(depth):
        h = _layer(h, p[i].reshape(t, PLE_DIM), pos_row, invf, attn_pre_norm[i], attn_post_norm[i],
                   w_in[i], q_norm[i], kv_norm[i], w_uq[i], w_ukv[i], conv_w[i], conv_b[i],
                   gate_bias_i[i], gate_bias_f[i], mlstm_norm[i], w_out[i], ffn_pre_norm[i],
                   ffn_post_norm[i], w_gate[i], w_up[i], w_down[i], w_ple_proj[i], w_ple_gate[i],
                   batch, seq)
    return h.reshape(batch, seq, D_MODEL)
```

```python
import functools
import math

import jax
import jax.numpy as jnp
from jax import lax
from jax.experimental import pallas as pl
from jax.experimental.pallas import tpu as pltpu

F32 = jnp.float32
BF16 = jnp.bfloat16

D_MODEL = 1024
PLE_DIM = 256
MLA_HEADS = 4
QK_NOPE_DIM = 128
QK_ROPE_DIM = 64
QK_HEAD_DIM = QK_NOPE_DIM + QK_ROPE_DIM
V_HEAD_DIM = 128
Q_LORA_RANK = 256
KV_LORA_RANK = 128
ROPE_THETA = 10000.0
MLA_WIDTH = MLA_HEADS * V_HEAD_DIM
MLSTM_HEADS = 4
MLSTM_HEAD_DIM = 128
MLSTM_WIDTH = MLSTM_HEADS * MLSTM_HEAD_DIM
CONV_WIDTH = 4
CONV_STRIDE = 4
D_FF = 2816
EPS = 1e-6

LANES = 128
SUBLANES = 8
BF16_ROWS = 16
QK_PAD = 2 * LANES
V_ROWS = V_HEAD_DIM + BF16_ROWS
ROPE_HALF = QK_ROPE_DIM // 2
VMEM_LIMIT = 56 * 1024 * 1024

C_Q = 0
C_KV = C_Q + Q_LORA_RANK
C_MQK = C_KV + KV_LORA_RANK
C_MO = C_MQK + 2 * MLSTM_WIDTH
D_IN_PACKED = C_MO + MLSTM_WIDTH
GATE_I = QK_ROPE_DIM
GATE_F = GATE_I + SUBLANES
ROW_KEY, ROW_KMAX, ROW_B, GATE_ROWS = 0, SUBLANES, 2 * SUBLANES, 3 * SUBLANES

TILE_T = 512
TM_IN = 512
SUB_IN = 256
TQ = TILE_T
HEADS_PER_STEP = 4
MAX_STALE_EXCESS = 64.0
CHUNK = 256
MLSTM_TILES = 2
TM_OUT = 512
SUB_OUT = 256
FF_CHUNKS = (1024, 1024, 768)


def _rms(x, g):
    return x * lax.rsqrt(jnp.mean(x * x, axis=-1, keepdims=True) + EPS) * g


def _dot(a, b):
    return jnp.dot(a, b, preferred_element_type=F32)


def _dot_nt(a, b):
    return lax.dot_general(a, b, (((1,), (1,)), ((), ())), preferred_element_type=F32)


def _dot_tn(a, b):
    return lax.dot_general(a, b, (((0,), (0,)), ((), ())), preferred_element_type=F32)


def _const_spec(shape):
    return pl.BlockSpec(shape, lambda *_: (0,) * len(shape), pipeline_mode=pl.Buffered(1))


def _in_proj_body(x_ref, pos_ref, invf_ref, g_pre_ref, w_in_ref, w_tail_ref, qn_ref, kvn_ref,
                  w_uq_ref, w_uk_ref, w_uv_ref, w_mv_ref, vones_ref, conv_w_ref, conv_b_ref, gbias_ref,
                  qt_out, k_out, vt_out, mqt_out, mk_out, mvt_out, mo_out, grow_out, keycol_out,
                  ext_ref, stage_ref, *, tm, sub, tiles_per_seq, chunk):
    assert tm % sub == 0 and sub % chunk == 0
    i = pl.program_id(0)
    halo = SUBLANES

    @pl.when(lax.rem(i, tiles_per_seq) == 0)
    def _():
        ext_ref[:, 0:halo, :] = jnp.zeros((ext_ref.shape[0], halo, LANES), F32)

    for r0 in range(0, tm, sub):
        _in_proj_rows(r0, sub, halo, chunk, x_ref, pos_ref, invf_ref, g_pre_ref, w_in_ref, w_tail_ref,
                      qn_ref, kvn_ref, w_uq_ref, w_uk_ref, w_uv_ref, w_mv_ref, vones_ref, conv_w_ref,
                      conv_b_ref, gbias_ref, qt_out, k_out, vt_out, mqt_out, mk_out, mvt_out, mo_out,
                      grow_out, keycol_out, ext_ref, stage_ref)
    ext_ref[:, 0:halo, :] = ext_ref[:, tm:tm + halo, :]


def _in_proj_rows(r0, ts, halo, chunk, x_ref, pos_ref, invf_ref, g_pre_ref, w_in_ref, w_tail_ref,
                  qn_ref, kvn_ref, w_uq_ref, w_uk_ref, w_uv_ref, w_mv_ref, vones_ref, conv_w_ref,
                  conv_b_ref, gbias_ref, qt_out, k_out, vt_out, mqt_out, mk_out, mvt_out, mo_out,
                  grow_out, keycol_out, ext_ref, stage_ref):
    rows = slice(r0, r0 + ts)
    tile, lanes = r0 // TILE_T, slice(r0 % TILE_T, r0 % TILE_T + ts)
    u = _rms(x_ref[rows, :], g_pre_ref[...]).astype(BF16)

    cq_raw = _dot(u, w_in_ref[:, C_Q:C_Q + Q_LORA_RANK])
    ckv_raw = _dot(u, w_in_ref[:, C_KV:C_KV + KV_LORA_RANK])
    tail_t = _dot_nt(w_tail_ref[...], u)
    z = _dot(u, w_in_ref[:, C_MQK:C_MQK + 2 * MLSTM_WIDTH])
    for c in range(2 * MLSTM_WIDTH // LANES):
        ext_ref[c, halo + r0:halo + r0 + ts, :] = z[:, c * LANES:(c + 1) * LANES]
    mvt_out[tile, :, lanes] = (_dot_nt(w_mv_ref[...], u) + vones_ref[...]).astype(BF16)
    mo_out[rows, :] = jax.nn.sigmoid(_dot(u, w_in_ref[:, C_MO:C_MO + MLSTM_WIDTH])).astype(BF16)
    cq = _rms(cq_raw, qn_ref[...]).astype(BF16)
    ckv = _rms(ckv_raw, kvn_ref[...]).astype(BF16)
    qt = _dot_nt(w_uq_ref[...], cq)
    kn = _dot(ckv, w_uk_ref[...])
    vt_out[tile, :, lanes] = (_dot_nt(w_uv_ref[...], ckv) + vones_ref[...]).astype(BF16)

    ang = invf_ref[...] * pos_ref[:, rows]
    cos = jnp.cos(ang)
    sin = jnp.sin(ang)

    def rope_t(t1, t2):
        return t1 * cos - t2 * sin, t2 * cos + t1 * sin

    scale = QK_HEAD_DIM ** -0.5 * math.log2(math.e)
    for h in range(MLA_HEADS):
        o = h * QK_PAD
        pe = o + QK_NOPE_DIM
        r1, r2 = rope_t(qt[pe:pe + ROPE_HALF], qt[pe + ROPE_HALF:pe + QK_ROPE_DIM])
        qt_out[tile, o:pe, lanes] = (qt[o:pe] * scale).astype(BF16)
        qt_out[tile, pe:pe + ROPE_HALF, lanes] = (r1 * scale).astype(BF16)
        qt_out[tile, pe + ROPE_HALF:pe + QK_ROPE_DIM, lanes] = (r2 * scale).astype(BF16)
        qt_out[tile, pe + QK_ROPE_DIM:o + QK_PAD, lanes] = jnp.zeros((QK_PAD - QK_HEAD_DIM, ts), BF16)

    r1, r2 = rope_t(tail_t[0:ROPE_HALF], tail_t[ROPE_HALF:QK_ROPE_DIM])
    kpe_t = jnp.concatenate([r1, r2, jnp.zeros((LANES - QK_ROPE_DIM, ts), F32)], axis=0)
    kpe = kpe_t.T.astype(BF16)
    for h in range(MLA_HEADS):
        o = h * QK_PAD
        k_out[rows, o:o + LANES] = kn[:, h * LANES:(h + 1) * LANES].astype(BF16)
        k_out[rows, o + LANES:o + QK_PAD] = kpe

    gates = tail_t[GATE_I:GATE_I + 2 * SUBLANES] + gbias_ref[...]
    pos_in_chunk = lax.rem(r0 + lax.broadcasted_iota(jnp.int32, (SUBLANES, ts), 1), chunk)

    def chunk_scan(v, op, identity):
        shift = 1
        while shift < chunk:
            v = op(v, jnp.where(pos_in_chunk >= shift, pltpu.roll(v, shift, 1), identity))
            shift *= 2
        return v

    b8 = chunk_scan(jax.nn.log_sigmoid(gates[SUBLANES:]), jnp.add, 0.0)
    key8 = gates[:SUBLANES] - b8
    grow_out[:, rows] = jnp.concatenate([key8, chunk_scan(key8, jnp.maximum, -jnp.inf), b8], axis=0)
    keycol_out[rows, :] = jnp.concatenate([key8, jnp.zeros((LANES - SUBLANES, ts), F32)], axis=0).T

    n = ts // CONV_STRIDE
    for c in range(2 * MLSTM_WIDTH // LANES):
        cols = slice(c * LANES, (c + 1) * LANES)
        taps = {q: ext_ref[c, pl.ds(halo + r0 + q, n, stride=CONV_STRIDE), :]
                for q in range(1 - CONV_WIDTH, CONV_STRIDE)}
        for r in range(CONV_STRIDE):
            acc = conv_b_ref[:, cols]
            for j in range(CONV_WIDTH):
                acc = acc + conv_w_ref[j:j + 1, cols] * taps[r - (CONV_WIDTH - 1) + j]
            stage_ref[c, pl.ds(r0 + r, n, stride=CONV_STRIDE), :] = acc * jax.nn.sigmoid(acc)
    for c in range(MLSTM_WIDTH // LANES):
        cols = slice(c * LANES, (c + 1) * LANES)
        mqt_out[tile, cols, lanes] = stage_ref[c, rows, :].T.astype(BF16)
        k_slab = stage_ref[MLSTM_WIDTH // LANES + c, rows, :]
        mk_out[rows, cols] = (k_slab * (MLSTM_HEAD_DIM ** -0.5)).astype(BF16)


def _in_proj(x2, pos_row, invf, g_pre, w_in_p, w_tail, qn, kvn, w_uq_t, w_uk, w_uv_t, w_mv_t, vones,
             conv_w, conv_b, gbias, seq):
    t = x2.shape[0]
    tm = TM_IN
    nt = t // TILE_T
    row = lambda width: pl.BlockSpec((tm, width), lambda i: (i, 0))
    tile_t = lambda rows: pl.BlockSpec((tm // TILE_T, rows, TILE_T), lambda i: (i, 0, 0))
    bf = lambda *shape: jax.ShapeDtypeStruct(shape, BF16)
    out_shape = [
        bf(nt, MLA_HEADS * QK_PAD, TILE_T), bf(t, MLA_HEADS * QK_PAD), bf(nt, MLA_HEADS * V_ROWS, TILE_T),
        bf(nt, MLSTM_WIDTH, TILE_T), bf(t, MLSTM_WIDTH), bf(nt, MLSTM_HEADS * V_ROWS, TILE_T),
        bf(t, MLSTM_WIDTH), jax.ShapeDtypeStruct((GATE_ROWS, t), F32),
        jax.ShapeDtypeStruct((t, LANES), F32),
    ]
    out_specs = [
        tile_t(MLA_HEADS * QK_PAD), row(MLA_HEADS * QK_PAD), tile_t(MLA_HEADS * V_ROWS),
        tile_t(MLSTM_WIDTH), row(MLSTM_WIDTH), tile_t(MLSTM_HEADS * V_ROWS), row(MLSTM_WIDTH),
        pl.BlockSpec((GATE_ROWS, tm), lambda i: (0, i)), row(LANES),
    ]
    consts = (invf, g_pre, w_in_p, w_tail, qn, kvn, w_uq_t, w_uk, w_uv_t, w_mv_t, vones, conv_w,
              conv_b, gbias)
    return pl.pallas_call(
        functools.partial(_in_proj_body, tm=tm, sub=SUB_IN, tiles_per_seq=seq // tm, chunk=CHUNK),
        out_shape=out_shape,
        grid=(t // tm,),
        in_specs=[row(D_MODEL), pl.BlockSpec((1, tm), lambda i: (0, i))]
        + [_const_spec(c.shape) for c in consts],
        out_specs=out_specs,
        scratch_shapes=[pltpu.VMEM((2 * MLSTM_WIDTH // LANES, tm + 2 * SUBLANES, LANES), F32),
                        pltpu.VMEM((2 * MLSTM_WIDTH // LANES, tm, LANES), F32)],
        compiler_params=pltpu.CompilerParams(dimension_semantics=("arbitrary",),
                                             vmem_limit_bytes=VMEM_LIMIT),
        name="in_proj",
    )(x2, pos_row, *consts)


def _attn_body(qt_ref, k_ref, vt_ref, o_ref, m_ref, use_ref, viol_ref, acc_ref, pa_ref, pb_ref,
               ua_ref, ub_ref, *, tq, heads):
    qi = pl.program_id(2)
    hq = [slice(h * QK_PAD, (h + 1) * QK_PAD) for h in range(heads)]
    hv = [slice(h * V_ROWS, (h + 1) * V_ROWS) for h in range(heads)]
    stat = m_ref.shape[1:]

    def qk(ki, h):
        start = pl.multiple_of(ki * tq, tq)
        return _dot(k_ref[pl.ds(start, tq), hq[h]], qt_ref[0, hq[h], :])

    def causal(s):
        r = lax.broadcasted_iota(jnp.int32, s.shape, 0)
        c = lax.broadcasted_iota(jnp.int32, s.shape, 1)
        return jnp.where(r <= c, s, -jnp.inf)

    def fast_scores(ki, p_ref, u_ref):
        for h in range(heads):
            s = qk(ki, h)
            m_use = m_ref[h, 0:1, :]
            p_ref[h] = jnp.exp2(s - m_use).astype(BF16)
            cmax = jnp.max(s, axis=0, keepdims=True)
            excess = jnp.where(qi > 0, cmax - m_use, 0.0)
            u_ref[h] = jnp.broadcast_to(m_use, stat)
            viol_ref[h] = jnp.broadcast_to(jnp.maximum(viol_ref[h, 0:1, :], excess), stat)
            m_ref[h] = jnp.broadcast_to(jnp.maximum(m_use, cmax), stat)

    def fast_pv(ki, p_ref, u_ref):
        for h in range(heads):
            m_use = u_ref[h, 0:1, :]
            alpha = jnp.exp2(use_ref[h, 0:1, :] - m_use)
            acc_ref[h] = alpha * acc_ref[h] + _dot(vt_ref[ki, hv[h], :], p_ref[h])
            use_ref[h] = jnp.broadcast_to(m_use, stat)

    def exact_block(ki, diagonal):
        for h in range(heads):
            s = causal(qk(ki, h)) if diagonal else qk(ki, h)
            m_old = m_ref[h, 0:1, :]
            m_new = jnp.maximum(m_old, jnp.max(s, axis=0, keepdims=True))
            p = jnp.exp2(s - m_new).astype(BF16)
            acc_ref[h] = jnp.exp2(m_old - m_new) * acc_ref[h] + _dot(vt_ref[ki, hv[h], :], p)
            m_ref[h] = jnp.broadcast_to(m_new, stat)

    def start():
        viol_ref[...] = jnp.zeros(viol_ref.shape, F32)
        s_diag = [causal(qk(qi, h)) for h in range(heads)]
        for h in range(heads):
            cmax = jnp.broadcast_to(jnp.max(s_diag[h], axis=0, keepdims=True), stat)
            m_ref[h] = cmax
            use_ref[h] = cmax
        fast_scores(0, pa_ref, ua_ref)
        for h in range(heads):
            p = jnp.exp2(s_diag[h] - use_ref[h, 0:1, :]).astype(BF16)
            acc_ref[h] = _dot(vt_ref[qi, hv[h], :], p)

    bufs = ((pa_ref, ua_ref), (pb_ref, ub_ref))

    def run(first, count, score_next):
        for i in range(count):
            if i + 1 < count or score_next:
                fast_scores(first + i + 1, *bufs[(i + 1) % 2])
            fast_pv(first + i, *bufs[i % 2])

    def four(j, carry):
        run(4 * j, 4, True)
        return carry

    for short in range(5):
        @pl.when(qi == short)
        def _(short=short):
            start()
            run(0, short, False)

    @pl.when(qi > 4)
    def _():
        start()
        run(0, 4, True)

    n_loop = jnp.maximum(qi - 1, 0) // 4
    lax.fori_loop(1, n_loop, four, 0)
    for left in range(1, 5):
        @pl.when(jnp.logical_and(qi > 4, qi - 4 * n_loop == left))
        def _(left=left):
            run(4 * n_loop, left, False)

    @pl.when(jnp.max(viol_ref[...]) > MAX_STALE_EXCESS)
    def _():
        m_ref[...] = jnp.full(m_ref.shape, -jnp.inf, F32)
        acc_ref[...] = jnp.zeros(acc_ref.shape, F32)

        def blk(ki, carry):
            exact_block(ki, False)
            return carry

        lax.fori_loop(0, qi, blk, 0)
        exact_block(qi, True)

    for h in range(heads):
        acc = acc_ref[h]
        out_t = acc[0:V_HEAD_DIM] / acc[V_HEAD_DIM:V_HEAD_DIM + 1]
        o_ref[0, h * V_HEAD_DIM:(h + 1) * V_HEAD_DIM, :] = out_t.astype(o_ref.dtype)


def _attention(qt, k, vt, batch, seq):
    t = k.shape[0]
    tq = TQ
    nq = seq // tq
    heads = HEADS_PER_STEP
    return pl.pallas_call(
        functools.partial(_attn_body, tq=tq, heads=heads),
        out_shape=jax.ShapeDtypeStruct((t // tq, MLA_WIDTH, tq), BF16),
        grid=(batch, MLA_HEADS // heads, nq),
        in_specs=[
            pl.BlockSpec((1, heads * QK_PAD, tq), lambda b, g, i: (b * nq + i, g, 0)),
            pl.BlockSpec((seq, heads * QK_PAD), lambda b, g, i: (b, g), pipeline_mode=pl.Buffered(1)),
            pl.BlockSpec((nq, heads * V_ROWS, tq), lambda b, g, i: (b, g, 0),
                         pipeline_mode=pl.Buffered(1)),
        ],
        out_specs=pl.BlockSpec((1, heads * V_HEAD_DIM, tq), lambda b, g, i: (b * nq + i, g, 0)),
        scratch_shapes=[pltpu.VMEM((heads, SUBLANES, tq), F32),
                        pltpu.VMEM((heads, SUBLANES, tq), F32),
                        pltpu.VMEM((heads, SUBLANES, tq), F32),
                        pltpu.VMEM((heads, V_ROWS, tq), F32),
                        pltpu.VMEM((heads, tq, tq), BF16),
                        pltpu.VMEM((heads, tq, tq), BF16),
                        pltpu.VMEM((heads, SUBLANES, tq), F32),
                        pltpu.VMEM((heads, SUBLANES, tq), F32)],
        compiler_params=pltpu.CompilerParams(
            dimension_semantics=("arbitrary", "arbitrary", "arbitrary"),
            vmem_limit_bytes=VMEM_LIMIT),
        name="mla_attention",
    )(qt, k, vt)


def _mlstm_body(qt_ref, k_ref, vt_ref, og_ref, grow_ref, keycol_ref, norm_ref, out_ref, ct_ref,
                m_ref, *, chunk, chunks_per_step):
    @pl.when(pl.program_id(1) == 0)
    def _():
        ct_ref[...] = jnp.zeros(ct_ref.shape, F32)
        m_ref[...] = jnp.zeros(m_ref.shape, F32)

    heads = range(MLSTM_HEADS)
    hs = [slice(h * MLSTM_HEAD_DIM, (h + 1) * MLSTM_HEAD_DIM) for h in heads]
    vs = [slice(h * V_ROWS, (h + 1) * V_ROWS) for h in heads]
    tok = [slice(c * chunk, (c + 1) * chunk) for c in range(chunks_per_step)]
    ttok = [(c * chunk // TILE_T, slice(c * chunk % TILE_T, c * chunk % TILE_T + chunk))
            for c in range(chunks_per_step)]
    r = lax.broadcasted_iota(jnp.int32, (chunk, chunk), 0)
    c = lax.broadcasted_iota(jnp.int32, (chunk, chunk), 1)
    upper = r <= c

    s_raw = [[_dot(k_ref[t, hs[h]], qt_ref[tile, hs[h], lanes]) for h in heads]
             for t, (tile, lanes) in zip(tok, ttok)]
    ct = [ct_ref[h] for h in heads]
    m_prev = [m_ref[h, 0:1, 0:1] for h in heads]
    for ci, (t, (tile, lanes)) in enumerate(zip(tok, ttok)):
        qt = [qt_ref[tile, hs[h], lanes] for h in heads]
        vt = [vt_ref[tile, vs[h], lanes] for h in heads]
        cq = [_dot(ct[h].astype(BF16), qt[h]) for h in heads]
        grow = grow_ref[:, t]
        mm = [jnp.maximum(m_prev[h], grow[ROW_KMAX + h:ROW_KMAX + h + 1]) for h in heads]
        b_row = [grow[ROW_B + h:ROW_B + h + 1] for h in heads]
        inter = [jnp.exp(m_prev[h] - mm[h]) for h in heads]
        for h in heads:
            mm_last = mm[h][:, chunk - 1:chunk]
            w_row = jnp.exp(grow[ROW_KEY + h:ROW_KEY + h + 1] - mm_last)
            vtw = (vt[h].astype(F32) * w_row).astype(BF16)
            ct[h] = jnp.exp(m_prev[h] - mm_last) * ct[h] + _dot(vtw, k_ref[t, hs[h]])
            m_prev[h] = b_row[h][:, chunk - 1:chunk] + mm_last
        keycol = keycol_ref[t, :]
        for h in heads:
            p = jnp.exp(jnp.where(upper, keycol[:, h:h + 1] - mm[h], -jnp.inf))
            st = (s_raw[ci][h] * p).astype(BF16)
            tot = _dot(vt[h], st) + inter[h] * cq[h]
            den = tot[MLSTM_HEAD_DIM:MLSTM_HEAD_DIM + 1]
            ht = tot[0:MLSTM_HEAD_DIM] / jnp.maximum(jnp.abs(den), jnp.exp(-(b_row[h] + mm[h])))
            hn_t = ht * lax.rsqrt(jnp.mean(ht * ht, axis=0, keepdims=True) + EPS)
            out_ref[t, hs[h]] = (hn_t.T * norm_ref[0:1, hs[h]]
                                 * og_ref[t, hs[h]].astype(F32)).astype(out_ref.dtype)
    for h in heads:
        ct_ref[h] = ct[h]
        m_ref[h] = jnp.broadcast_to(m_prev[h], m_ref.shape[1:])


def _mlstm(mqt, mk, mvt, mo, grow, keycol, norm, batch, seq):
    t = mk.shape[0]
    tile = MLSTM_TILES * TILE_T
    ns = seq // tile
    row = lambda width: pl.BlockSpec((tile, width), lambda b, j: (b * ns + j, 0))
    tile_t = lambda rows: pl.BlockSpec((MLSTM_TILES, rows, TILE_T), lambda b, j: (b * ns + j, 0, 0))
    return pl.pallas_call(
        functools.partial(_mlstm_body, chunk=CHUNK, chunks_per_step=tile // CHUNK),
        out_shape=jax.ShapeDtypeStruct((t, MLSTM_WIDTH), BF16),
        grid=(batch, ns),
        in_specs=[tile_t(MLSTM_WIDTH), row(MLSTM_WIDTH), tile_t(MLSTM_HEADS * V_ROWS),
                  row(MLSTM_WIDTH), pl.BlockSpec((GATE_ROWS, tile), lambda b, j: (0, b * ns + j)),
                  row(LANES), _const_spec(norm.shape)],
        out_specs=row(MLSTM_WIDTH),
        scratch_shapes=[
            pltpu.VMEM((MLSTM_HEADS, V_ROWS, MLSTM_HEAD_DIM), F32),
            pltpu.VMEM((MLSTM_HEADS, SUBLANES, LANES), F32),
        ],
        compiler_params=pltpu.CompilerParams(dimension_semantics=("arbitrary", "arbitrary"),
                                             vmem_limit_bytes=VMEM_LIMIT),
        name="mlstm",
    )(mqt, mk, mvt, mo, grow, keycol, norm)


def _out_ffn_body(x_ref, mla_ref, mlstm_ref, p_ref, w_out_ref, g_post_ref, g_fpre_ref, g_fpost_ref,
                  w_gate_ref, w_up_ref, w_down_ref, w_pg_ref, w_pp_ref, o_ref):
    tm = x_ref.shape[0]
    subs = [slice(r0, r0 + SUB_OUT) for r0 in range(0, tm, SUB_OUT)]
    mla_t = [mla_ref[s.start // TILE_T, :, s.start % TILE_T:s.start % TILE_T + SUB_OUT] for s in subs]
    mix = [_dot_tn(a, w_out_ref[0:MLA_WIDTH, :])
           + _dot(mlstm_ref[s, :], w_out_ref[MLA_WIDTH:MLA_WIDTH + MLSTM_WIDTH, :])
           for s, a in zip(subs, mla_t)]
    h1 = [x_ref[s, :] + _rms(m, g_post_ref[...]) for s, m in zip(subs, mix)]
    f = [_rms(h, g_fpre_ref[...]).astype(BF16) for h in h1]
    ffn = [None] * len(subs)
    off = 0
    for width in FF_CHUNKS:
        cols = slice(off, off + width)
        gate = [_dot(v, w_gate_ref[:, cols]) for v in f]
        up = [_dot(v, w_up_ref[:, cols]) for v in f]
        act = [(g * jax.nn.sigmoid(g) * u).astype(BF16) for g, u in zip(gate, up)]
        part = [_dot(a, w_down_ref[cols, :]) for a in act]
        ffn = [p if acc is None else acc + p for acc, p in zip(ffn, part)]
        off += width
    h2 = [h + _rms(v, g_fpost_ref[...]) for h, v in zip(h1, ffn)]
    pgate = [jax.nn.sigmoid(_dot(h.astype(BF16), w_pg_ref[...])) for h in h2]
    for s, h, g in zip(subs, h2, pgate):
        o_ref[s, :] = h + g * _dot(p_ref[s, :].astype(BF16), w_pp_ref[...])


def _out_ffn(x2, mla, mlstm, p2, w_out, g_post, g_fpre, g_fpost, w_gate, w_up, w_down, w_pg, w_pp):
    t = x2.shape[0]
    tm = TM_OUT
    row = lambda width: pl.BlockSpec((tm, width), lambda i: (i, 0))
    consts = (w_out, g_post, g_fpre, g_fpost, w_gate, w_up, w_down, w_pg, w_pp)
    return pl.pallas_call(
        _out_ffn_body,
        out_shape=jax.ShapeDtypeStruct((t, D_MODEL), F32),
        grid=(t // tm,),
        in_specs=[row(D_MODEL), pl.BlockSpec((tm // TILE_T, MLA_WIDTH, TILE_T), lambda i: (i, 0, 0)),
                  row(MLSTM_WIDTH), row(PLE_DIM)]
        + [_const_spec(c.shape) for c in consts],
        out_specs=row(D_MODEL),
        compiler_params=pltpu.CompilerParams(dimension_semantics=("arbitrary",),
                                             vmem_limit_bytes=VMEM_LIMIT),
        name="out_ffn",
    )(x2, mla, mlstm, p2, *consts)


def _pack_w_in(w_in):
    o_krope = Q_LORA_RANK + KV_LORA_RANK
    o_mq = o_krope + QK_ROPE_DIM
    o_mv = o_mq + 2 * MLSTM_WIDTH
    o_mo = o_mv + MLSTM_WIDTH
    o_gates = o_mo + MLSTM_WIDTH
    main = jnp.concatenate([w_in[:, :o_krope], w_in[:, o_mq:o_mv], w_in[:, o_mo:o_gates]], axis=1)
    gap = jnp.zeros((D_MODEL, SUBLANES - MLSTM_HEADS), w_in.dtype)
    tail = jnp.concatenate([w_in[:, o_krope:o_mq], w_in[:, o_gates:o_gates + MLSTM_HEADS], gap,
                            w_in[:, o_gates + MLSTM_HEADS:], gap], axis=1).T
    tail = jnp.pad(tail, ((0, LANES - tail.shape[0]), (0, 0)))
    mv_t = w_in[:, o_mv:o_mo].T.reshape(MLSTM_HEADS, MLSTM_HEAD_DIM, D_MODEL)
    mv_t = jnp.pad(mv_t, ((0, 0), (0, V_ROWS - MLSTM_HEAD_DIM), (0, 0)))
    return main.astype(BF16), tail.astype(BF16), mv_t.reshape(MLSTM_HEADS * V_ROWS, D_MODEL).astype(BF16)


def _pack_w_uq_t(w_uq):
    w = w_uq.T.reshape(MLA_HEADS, QK_HEAD_DIM, Q_LORA_RANK)
    w = jnp.pad(w, ((0, 0), (0, QK_PAD - QK_HEAD_DIM), (0, 0)))
    return w.reshape(MLA_HEADS * QK_PAD, Q_LORA_RANK).astype(BF16)


def _pack_w_ukv(w_ukv):
    w = w_ukv.reshape(KV_LORA_RANK, MLA_HEADS, QK_NOPE_DIM + V_HEAD_DIM)
    w_uk = w[:, :, :QK_NOPE_DIM].reshape(KV_LORA_RANK, MLA_HEADS * QK_NOPE_DIM)
    w_uv_t = jnp.transpose(w[:, :, QK_NOPE_DIM:], (1, 2, 0))
    w_uv_t = jnp.pad(w_uv_t, ((0, 0), (0, V_ROWS - V_HEAD_DIM), (0, 0)))
    return w_uk.astype(BF16), w_uv_t.reshape(MLA_HEADS * V_ROWS, KV_LORA_RANK).astype(BF16)


def _layer(h, p_i, pos_row, invf, attn_pre_norm, attn_post_norm, w_in, q_norm, kv_norm, w_uq, w_ukv,
           conv_w, conv_b, gate_bias_i, gate_bias_f, mlstm_norm, w_out, ffn_pre_norm, ffn_post_norm,
           w_gate, w_up, w_down, w_ple_proj, w_ple_gate, batch, seq):
    row = lambda a: a.reshape(1, -1)
    w_in_p, w_tail, w_mv_t = _pack_w_in(w_in)
    w_uk, w_uv_t = _pack_w_ukv(w_ukv)
    gap = jnp.zeros((SUBLANES - MLSTM_HEADS,), F32)
    gbias = jnp.concatenate([gate_bias_i, gap, gate_bias_f, gap]).astype(F32).reshape(2 * SUBLANES, 1)
    vones = jnp.zeros((MLA_HEADS, V_ROWS, 1), F32).at[:, V_HEAD_DIM, 0].set(1.0)
    qt, k, vt, mqt, mk, mvt, mo, grow, keycol = _in_proj(
        h, pos_row, invf, row(attn_pre_norm), w_in_p, w_tail, row(q_norm), row(kv_norm),
        _pack_w_uq_t(w_uq), w_uk, w_uv_t, w_mv_t, vones.reshape(MLA_HEADS * V_ROWS, 1), conv_w,
        row(conv_b), gbias, seq)
    mla = _attention(qt, k, vt, batch, seq)
    mlstm = _mlstm(mqt, mk, mvt, mo, grow, keycol, row(mlstm_norm), batch, seq)
    return _out_ffn(h, mla, mlstm, p_i, w_out.astype(BF16), row(attn_post_norm), row(ffn_pre_norm),
                    row(ffn_post_norm), w_gate.astype(BF16), w_up.astype(BF16),
                    w_down.astype(BF16), w_ple_gate.astype(BF16), w_ple_proj.astype(BF16))


@jax.jit
def kernel(x, p, positions, attn_pre_norm, attn_post_norm, w_in, q_norm, kv_norm, w_uq, w_ukv,
           conv_w, conv_b, gate_bias_i, gate_bias_f, mlstm_norm, w_out, ffn_pre_norm, ffn_post_norm,
           w_gate, w_up, w_down, w_ple_proj, w_ple_gate):
    batch, seq, _ = x.shape
    t = batch * seq
    depth = p.shape[0]
    inv_freq = ROPE_THETA ** (-jnp.arange(0, QK_ROPE_DIM, 2, dtype=F32) / QK_ROPE_DIM)
    invf = inv_freq.reshape(ROPE_HALF, 1)
    pos_row = positions.astype(F32).reshape(1, t)
    h = x.reshape(t, D_MODEL)
    for i in range(depth):
        h = _layer(h, p[i].reshape(t, PLE_DIM), pos_row, invf, attn_pre_norm[i], attn_post_norm[i],
                   w_in[i], q_norm[i], kv_norm[i], w_uq[i], w_ukv[i], conv_w[i], conv_b[i],
                   gate_bias_i[i], gate_bias_f[i], mlstm_norm[i], w_out[i], ffn_pre_norm[i],
                   ffn_post_norm[i], w_gate[i], w_up[i], w_down[i], w_ple_proj[i], w_ple_gate[i],
                   batch, seq)
    return h.reshape(batch, seq, D_MODEL)
```

```python
import functools
import math

import jax
import jax.numpy as jnp
from jax import lax
from jax.experimental import pallas as pl
from jax.experimental.pallas import tpu as pltpu

F32 = jnp.float32
BF16 = jnp.bfloat16

D_MODEL = 1024
PLE_DIM = 256
MLA_HEADS = 4
QK_NOPE_DIM = 128
QK_ROPE_DIM = 64
QK_HEAD_DIM = QK_NOPE_DIM + QK_ROPE_DIM
V_HEAD_DIM = 128
Q_LORA_RANK = 256
KV_LORA_RANK = 128
ROPE_THETA = 10000.0
MLA_WIDTH = MLA_HEADS * V_HEAD_DIM
MLSTM_HEADS = 4
MLSTM_HEAD_DIM = 128
MLSTM_WIDTH = MLSTM_HEADS * MLSTM_HEAD_DIM
CONV_WIDTH = 4
CONV_STRIDE = 4
D_FF = 2816
EPS = 1e-6

LANES = 128
SUBLANES = 8
BF16_ROWS = 16
QK_PAD = 2 * LANES
V_ROWS = V_HEAD_DIM + BF16_ROWS
ROPE_HALF = QK_ROPE_DIM // 2
VMEM_LIMIT = 56 * 1024 * 1024

C_Q = 0
C_KV = C_Q + Q_LORA_RANK
C_MQK = C_KV + KV_LORA_RANK
C_MO = C_MQK + 2 * MLSTM_WIDTH
GATE_I = QK_ROPE_DIM
ROW_KEY, ROW_KMAX, ROW_B, GATE_ROWS = 0, SUBLANES, 2 * SUBLANES, 3 * SUBLANES

TILE_T = 512
TM_IN = 512
SUB_IN = 256
TQ = TILE_T
HEADS_PER_STEP = 4
MAX_STALE_EXCESS = 64.0
CHUNK = 256
MLSTM_TILES = 4
TM_OUT = 512
SUB_OUT = 256
FF_CHUNKS = (1024, 1024, 768)


def _rms(x, g):
    return x * lax.rsqrt(jnp.mean(x * x, axis=-1, keepdims=True) + EPS) * g


def _dot(a, b):
    return jnp.dot(a, b, preferred_element_type=F32)


def _dot_nt(a, b):
    return lax.dot_general(a, b, (((1,), (1,)), ((), ())), preferred_element_type=F32)


def _dot_tn(a, b):
    return lax.dot_general(a, b, (((0,), (0,)), ((), ())), preferred_element_type=F32)


def _const_spec(shape):
    return pl.BlockSpec(shape, lambda *_: (0,) * len(shape), pipeline_mode=pl.Buffered(1))


def _in_proj_body(x_ref, pos_ref, invf_ref, g_pre_ref, w_in_ref, w_tail_ref, qn_ref, kvn_ref,
                  w_uq_ref, w_uk_ref, w_uv_ref, w_mv_ref, vones_ref, conv_w_ref, conv_b_ref, gbias_ref,
                  qt_out, k_out, vt_out, mqt_out, mk_out, mvt_out, mo_out, grow_out, keycol_out,
                  ext_ref, stage_ref, *, tm, sub, tiles_per_seq, chunk):
    assert tm % sub == 0 and sub % chunk == 0
    i = pl.program_id(0)
    halo = SUBLANES

    @pl.when(lax.rem(i, tiles_per_seq) == 0)
    def _():
        ext_ref[:, 0:halo, :] = jnp.zeros((ext_ref.shape[0], halo, LANES), F32)

    for r0 in range(0, tm, sub):
        _in_proj_rows(r0, sub, halo, chunk, x_ref, pos_ref, invf_ref, g_pre_ref, w_in_ref, w_tail_ref,
                      qn_ref, kvn_ref, w_uq_ref, w_uk_ref, w_uv_ref, w_mv_ref, vones_ref, conv_w_ref,
                      conv_b_ref, gbias_ref, qt_out, k_out, vt_out, mqt_out, mk_out, mvt_out, mo_out,
                      grow_out, keycol_out, ext_ref, stage_ref)
    ext_ref[:, 0:halo, :] = ext_ref[:, tm:tm + halo, :]


def _in_proj_rows(r0, ts, halo, chunk, x_ref, pos_ref, invf_ref, g_pre_ref, w_in_ref, w_tail_ref,
                  qn_ref, kvn_ref, w_uq_ref, w_uk_ref, w_uv_ref, w_mv_ref, vones_ref, conv_w_ref,
                  conv_b_ref, gbias_ref, qt_out, k_out, vt_out, mqt_out, mk_out, mvt_out, mo_out,
                  grow_out, keycol_out, ext_ref, stage_ref):
    rows = slice(r0, r0 + ts)
    tile, lanes = r0 // TILE_T, slice(r0 % TILE_T, r0 % TILE_T + ts)
    u = _rms(x_ref[rows, :], g_pre_ref[...]).astype(BF16)

    cq_raw = _dot(u, w_in_ref[:, C_Q:C_Q + Q_LORA_RANK])
    ckv_raw = _dot(u, w_in_ref[:, C_KV:C_KV + KV_LORA_RANK])
    tail_t = _dot_nt(w_tail_ref[...], u)
    z = _dot(u, w_in_ref[:, C_MQK:C_MQK + 2 * MLSTM_WIDTH])
    for c in range(2 * MLSTM_WIDTH // LANES):
        ext_ref[c, halo + r0:halo + r0 + ts, :] = z[:, c * LANES:(c + 1) * LANES]
    mvt_out[tile, :, lanes] = (_dot_nt(w_mv_ref[...], u) + vones_ref[...]).astype(BF16)
    mo_out[rows, :] = jax.nn.sigmoid(_dot(u, w_in_ref[:, C_MO:C_MO + MLSTM_WIDTH])).astype(BF16)
    cq = _rms(cq_raw, qn_ref[...]).astype(BF16)
    ckv = _rms(ckv_raw, kvn_ref[...]).astype(BF16)
    qt = _dot_nt(w_uq_ref[...], cq)
    kn = _dot(ckv, w_uk_ref[...])
    vt_out[tile, :, lanes] = (_dot_nt(w_uv_ref[...], ckv) + vones_ref[...]).astype(BF16)

    ang = invf_ref[...] * pos_ref[:, rows]
    cos = jnp.cos(ang)
    sin = jnp.sin(ang)

    def rope_t(t1, t2):
        return t1 * cos - t2 * sin, t2 * cos + t1 * sin

    scale = QK_HEAD_DIM ** -0.5 * math.log2(math.e)
    for h in range(MLA_HEADS):
        o = h * QK_PAD
        pe = o + QK_NOPE_DIM
        r1, r2 = rope_t(qt[pe:pe + ROPE_HALF], qt[pe + ROPE_HALF:pe + QK_ROPE_DIM])
        qt_out[tile, o:pe, lanes] = (qt[o:pe] * scale).astype(BF16)
        qt_out[tile, pe:pe + ROPE_HALF, lanes] = (r1 * scale).astype(BF16)
        qt_out[tile, pe + ROPE_HALF:pe + QK_ROPE_DIM, lanes] = (r2 * scale).astype(BF16)
        qt_out[tile, pe + QK_ROPE_DIM:o + QK_PAD, lanes] = jnp.zeros((QK_PAD - QK_HEAD_DIM, ts), BF16)

    r1, r2 = rope_t(tail_t[0:ROPE_HALF], tail_t[ROPE_HALF:QK_ROPE_DIM])
    kpe_t = jnp.concatenate([r1, r2, jnp.zeros((LANES - QK_ROPE_DIM, ts), F32)], axis=0)
    kpe = kpe_t.T.astype(BF16)
    for h in range(MLA_HEADS):
        o = h * QK_PAD
        k_out[rows, o:o + LANES] = kn[:, h * LANES:(h + 1) * LANES].astype(BF16)
        k_out[rows, o + LANES:o + QK_PAD] = kpe

    gates = tail_t[GATE_I:GATE_I + 2 * SUBLANES] + gbias_ref[...]
    pos_in_chunk = lax.rem(r0 + lax.broadcasted_iota(jnp.int32, (SUBLANES, ts), 1), chunk)

    def chunk_scan(v, op, identity):
        shift = 1
        while shift < chunk:
            v = op(v, jnp.where(pos_in_chunk >= shift, pltpu.roll(v, shift, 1), identity))
            shift *= 2
        return v

    b8 = chunk_scan(jax.nn.log_sigmoid(gates[SUBLANES:]), jnp.add, 0.0)
    key8 = gates[:SUBLANES] - b8
    grow_out[:, rows] = jnp.concatenate([key8, chunk_scan(key8, jnp.maximum, -jnp.inf), b8], axis=0)
    keycol_out[rows, :] = jnp.concatenate([key8, jnp.zeros((LANES - SUBLANES, ts), F32)], axis=0).T

    n = ts // CONV_STRIDE
    for c in range(2 * MLSTM_WIDTH // LANES):
        cols = slice(c * LANES, (c + 1) * LANES)
        taps = {q: ext_ref[c, pl.ds(halo + r0 + q, n, stride=CONV_STRIDE), :]
                for q in range(1 - CONV_WIDTH, CONV_STRIDE)}
        for r in range(CONV_STRIDE):
            acc = conv_b_ref[:, cols]
            for j in range(CONV_WIDTH):
                acc = acc + conv_w_ref[j:j + 1, cols] * taps[r - (CONV_WIDTH - 1) + j]
            stage_ref[c, pl.ds(r0 + r, n, stride=CONV_STRIDE), :] = acc * jax.nn.sigmoid(acc)
    for c in range(MLSTM_WIDTH // LANES):
        cols = slice(c * LANES, (c + 1) * LANES)
        mqt_out[tile, cols, lanes] = stage_ref[c, rows, :].T.astype(BF16)
        k_slab = stage_ref[MLSTM_WIDTH // LANES + c, rows, :]
        mk_out[rows, cols] = (k_slab * (MLSTM_HEAD_DIM ** -0.5)).astype(BF16)


def _in_proj(x2, pos_row, invf, g_pre, w_in_p, w_tail, qn, kvn, w_uq_t, w_uk, w_uv_t, w_mv_t, vones,
             conv_w, conv_b, gbias, seq):
    t = x2.shape[0]
    tm = TM_IN
    nt = t // TILE_T
    row = lambda width: pl.BlockSpec((tm, width), lambda i: (i, 0))
    tile_t = lambda rows: pl.BlockSpec((tm // TILE_T, rows, TILE_T), lambda i: (i, 0, 0))
    bf = lambda *shape: jax.ShapeDtypeStruct(shape, BF16)
    out_shape = [
        bf(nt, MLA_HEADS * QK_PAD, TILE_T), bf(t, MLA_HEADS * QK_PAD), bf(nt, MLA_HEADS * V_ROWS, TILE_T),
        bf(nt, MLSTM_WIDTH, TILE_T), bf(t, MLSTM_WIDTH), bf(nt, MLSTM_HEADS * V_ROWS, TILE_T),
        bf(t, MLSTM_WIDTH), jax.ShapeDtypeStruct((GATE_ROWS, t), F32),
        jax.ShapeDtypeStruct((t, LANES), F32),
    ]
    out_specs = [
        tile_t(MLA_HEADS * QK_PAD), row(MLA_HEADS * QK_PAD), tile_t(MLA_HEADS * V_ROWS),
        tile_t(MLSTM_WIDTH), row(MLSTM_WIDTH), tile_t(MLSTM_HEADS * V_ROWS), row(MLSTM_WIDTH),
        pl.BlockSpec((GATE_ROWS, tm), lambda i: (0, i)), row(LANES),
    ]
    consts = (invf, g_pre, w_in_p, w_tail, qn, kvn, w_uq_t, w_uk, w_uv_t, w_mv_t, vones, conv_w,
              conv_b, gbias)
    return pl.pallas_call(
        functools.partial(_in_proj_body, tm=tm, sub=SUB_IN, tiles_per_seq=seq // tm, chunk=CHUNK),
        out_shape=out_shape,
        grid=(t // tm,),
        in_specs=[row(D_MODEL), pl.BlockSpec((1, tm), lambda i: (0, i))]
        + [_const_spec(c.shape) for c in consts],
        out_specs=out_specs,
        scratch_shapes=[pltpu.VMEM((2 * MLSTM_WIDTH // LANES, tm + 2 * SUBLANES, LANES), F32),
                        pltpu.VMEM((2 * MLSTM_WIDTH // LANES, tm, LANES), F32)],
        compiler_params=pltpu.CompilerParams(dimension_semantics=("arbitrary",),
                                             vmem_limit_bytes=VMEM_LIMIT),
        name="in_proj",
    )(x2, pos_row, *consts)


def _attn_body(qt_ref, k_ref, vt_ref, o_ref, m_ref, use_ref, viol_ref, acc_ref, pa_ref, pb_ref,
               ua_ref, ub_ref, *, tq, heads):
    qi = pl.program_id(2)
    hq = [slice(h * QK_PAD, (h + 1) * QK_PAD) for h in range(heads)]
    hv = [slice(h * V_ROWS, (h + 1) * V_ROWS) for h in range(heads)]
    stat = m_ref.shape[1:]

    def qk(ki, h):
        start = pl.multiple_of(ki * tq, tq)
        return _dot(k_ref[pl.ds(start, tq), hq[h]], qt_ref[0, hq[h], :])

    def causal(s):
        r = lax.broadcasted_iota(jnp.int32, s.shape, 0)
        c = lax.broadcasted_iota(jnp.int32, s.shape, 1)
        return jnp.where(r <= c, s, -jnp.inf)

    def fast_scores(ki, p_ref, u_ref):
        for h in range(heads):
            s = qk(ki, h)
            m_use = m_ref[h, 0:1, :]
            p_ref[h] = jnp.exp2(s - m_use).astype(BF16)
            cmax = jnp.max(s, axis=0, keepdims=True)
            excess = jnp.where(qi > 0, cmax - m_use, 0.0)
            u_ref[h] = jnp.broadcast_to(m_use, stat)
            viol_ref[h] = jnp.broadcast_to(jnp.maximum(viol_ref[h, 0:1, :], excess), stat)
            m_ref[h] = jnp.broadcast_to(jnp.maximum(m_use, cmax), stat)

    def fast_pv(ki, p_ref, u_ref):
        for h in range(heads):
            m_use = u_ref[h, 0:1, :]
            alpha = jnp.exp2(use_ref[h, 0:1, :] - m_use)
            acc_ref[h] = alpha * acc_ref[h] + _dot(vt_ref[ki, hv[h], :], p_ref[h])
            use_ref[h] = jnp.broadcast_to(m_use, stat)

    def exact_block(ki, diagonal):
        for h in range(heads):
            s = causal(qk(ki, h)) if diagonal else qk(ki, h)
            m_old = m_ref[h, 0:1, :]
            m_new = jnp.maximum(m_old, jnp.max(s, axis=0, keepdims=True))
            p = jnp.exp2(s - m_new).astype(BF16)
            acc_ref[h] = jnp.exp2(m_old - m_new) * acc_ref[h] + _dot(vt_ref[ki, hv[h], :], p)
            m_ref[h] = jnp.broadcast_to(m_new, stat)

    def start():
        viol_ref[...] = jnp.zeros(viol_ref.shape, F32)
        s_diag = [causal(qk(qi, h)) for h in range(heads)]
        for h in range(heads):
            cmax = jnp.broadcast_to(jnp.max(s_diag[h], axis=0, keepdims=True), stat)
            m_ref[h] = cmax
            use_ref[h] = cmax
        fast_scores(0, pa_ref, ua_ref)
        for h in range(heads):
            p = jnp.exp2(s_diag[h] - use_ref[h, 0:1, :]).astype(BF16)
            acc_ref[h] = _dot(vt_ref[qi, hv[h], :], p)

    bufs = ((pa_ref, ua_ref), (pb_ref, ub_ref))

    def run(first, count, score_next):
        for i in range(count):
            if i + 1 < count or score_next:
                fast_scores(first + i + 1, *bufs[(i + 1) % 2])
            fast_pv(first + i, *bufs[i % 2])

    def four(j, carry):
        run(4 * j, 4, True)
        return carry

    for short in range(5):
        @pl.when(qi == short)
        def _(short=short):
            start()
            run(0, short, False)

    @pl.when(qi > 4)
    def _():
        start()
        run(0, 4, True)

    n_loop = jnp.maximum(qi - 1, 0) // 4
    lax.fori_loop(1, n_loop, four, 0)
    for left in range(1, 5):
        @pl.when(jnp.logical_and(qi > 4, qi - 4 * n_loop == left))
        def _(left=left):
            run(4 * n_loop, left, False)

    @pl.when(jnp.max(viol_ref[...]) > MAX_STALE_EXCESS)
    def _():
        m_ref[...] = jnp.full(m_ref.shape, -jnp.inf, F32)
        acc_ref[...] = jnp.zeros(acc_ref.shape, F32)

        def blk(ki, carry):
            exact_block(ki, False)
            return carry

        lax.fori_loop(0, qi, blk, 0)
        exact_block(qi, True)

    for h in range(heads):
        acc = acc_ref[h]
        out_t = acc[0:V_HEAD_DIM] / acc[V_HEAD_DIM:V_HEAD_DIM + 1]
        o_ref[0, h * V_HEAD_DIM:(h + 1) * V_HEAD_DIM, :] = out_t.astype(o_ref.dtype)


def _attention(qt, k, vt, batch, seq):
    t = k.shape[0]
    tq = TQ
    nq = seq // tq
    heads = HEADS_PER_STEP
    return pl.pallas_call(
        functools.partial(_attn_body, tq=tq, heads=heads),
        out_shape=jax.ShapeDtypeStruct((t // tq, MLA_WIDTH, tq), BF16),
        grid=(batch, MLA_HEADS // heads, nq),
        in_specs=[
            pl.BlockSpec((1, heads * QK_PAD, tq), lambda b, g, i: (b * nq + i, g, 0)),
            pl.BlockSpec((seq, heads * QK_PAD), lambda b, g, i: (b, g), pipeline_mode=pl.Buffered(1)),
            pl.BlockSpec((nq, heads * V_ROWS, tq), lambda b, g, i: (b, g, 0),
                         pipeline_mode=pl.Buffered(1)),
        ],
        out_specs=pl.BlockSpec((1, heads * V_HEAD_DIM, tq), lambda b, g, i: (b * nq + i, g, 0)),
        scratch_shapes=[pltpu.VMEM((heads, SUBLANES, tq), F32),
                        pltpu.VMEM((heads, SUBLANES, tq), F32),
                        pltpu.VMEM((heads, SUBLANES, tq), F32),
                        pltpu.VMEM((heads, V_ROWS, tq), F32),
                        pltpu.VMEM((heads, tq, tq), BF16),
                        pltpu.VMEM((heads, tq, tq), BF16),
                        pltpu.VMEM((heads, SUBLANES, tq), F32),
                        pltpu.VMEM((heads, SUBLANES, tq), F32)],
        compiler_params=pltpu.CompilerParams(
            dimension_semantics=("arbitrary", "arbitrary", "arbitrary"),
            vmem_limit_bytes=VMEM_LIMIT),
        name="mla_attention",
    )(qt, k, vt)


def _mlstm_body(qt_ref, k_ref, vt_ref, og_ref, grow_ref, keycol_ref, norm_ref, out_ref, ct_ref,
                m_ref, *, chunk, chunks_per_step):
    @pl.when(pl.program_id(1) == 0)
    def _():
        ct_ref[...] = jnp.zeros(ct_ref.shape, F32)
        m_ref[...] = jnp.zeros(m_ref.shape, F32)

    heads = range(MLSTM_HEADS)
    hs = [slice(h * MLSTM_HEAD_DIM, (h + 1) * MLSTM_HEAD_DIM) for h in heads]
    vs = [slice(h * V_ROWS, (h + 1) * V_ROWS) for h in heads]
    tok = [slice(c * chunk, (c + 1) * chunk) for c in range(chunks_per_step)]
    ttok = [(c * chunk // TILE_T, slice(c * chunk % TILE_T, c * chunk % TILE_T + chunk))
            for c in range(chunks_per_step)]
    r = lax.broadcasted_iota(jnp.int32, (chunk, chunk), 0)
    c = lax.broadcasted_iota(jnp.int32, (chunk, chunk), 1)
    upper = r <= c

    s_raw = [[_dot(k_ref[t, hs[h]], qt_ref[tile, hs[h], lanes]) for h in heads]
             for t, (tile, lanes) in zip(tok, ttok)]
    ct = [ct_ref[h] for h in heads]
    m_prev = [m_ref[h, 0:1, 0:1] for h in heads]
    for ci, (t, (tile, lanes)) in enumerate(zip(tok, ttok)):
        qt = [qt_ref[tile, hs[h], lanes] for h in heads]
        vt = [vt_ref[tile, vs[h], lanes] for h in heads]
        cq = [_dot(ct[h].astype(BF16), qt[h]) for h in heads]
        grow = grow_ref[:, t]
        mm = [jnp.maximum(m_prev[h], grow[ROW_KMAX + h:ROW_KMAX + h + 1]) for h in heads]
        b_row = [grow[ROW_B + h:ROW_B + h + 1] for h in heads]
        inter = [jnp.exp(m_prev[h] - mm[h]) for h in heads]
        for h in heads:
            mm_last = mm[h][:, chunk - 1:chunk]
            w_row = jnp.exp(grow[ROW_KEY + h:ROW_KEY + h + 1] - mm_last)
            vtw = (vt[h].astype(F32) * w_row).astype(BF16)
            ct[h] = jnp.exp(m_prev[h] - mm_last) * ct[h] + _dot(vtw, k_ref[t, hs[h]])
            m_prev[h] = b_row[h][:, chunk - 1:chunk] + mm_last
        keycol = keycol_ref[t, :]
        for h in heads:
            p = jnp.exp(jnp.where(upper, keycol[:, h:h + 1] - mm[h], -jnp.inf))
            st = (s_raw[ci][h] * p).astype(BF16)
            tot = _dot(vt[h], st) + inter[h] * cq[h]
            den = tot[MLSTM_HEAD_DIM:MLSTM_HEAD_DIM + 1]
            ht = tot[0:MLSTM_HEAD_DIM] / jnp.maximum(jnp.abs(den), jnp.exp(-(b_row[h] + mm[h])))
            hn_t = ht * lax.rsqrt(jnp.mean(ht * ht, axis=0, keepdims=True) + EPS)
            out_ref[t, hs[h]] = (hn_t.T * norm_ref[0:1, hs[h]]
                                 * og_ref[t, hs[h]].astype(F32)).astype(out_ref.dtype)
    for h in heads:
        ct_ref[h] = ct[h]
        m_ref[h] = jnp.broadcast_to(m_prev[h], m_ref.shape[1:])


def _mlstm(mqt, mk, mvt, mo, grow, keycol, norm, batch, seq):
    t = mk.shape[0]
    tile = MLSTM_TILES * TILE_T
    ns = seq // tile
    row = lambda width: pl.BlockSpec((tile, width), lambda b, j: (b * ns + j, 0))
    tile_t = lambda rows: pl.BlockSpec((MLSTM_TILES, rows, TILE_T), lambda b, j: (b * ns + j, 0, 0))
    return pl.pallas_call(
        functools.partial(_mlstm_body, chunk=CHUNK, chunks_per_step=tile // CHUNK),
        out_shape=jax.ShapeDtypeStruct((t, MLSTM_WIDTH), BF16),
        grid=(batch, ns),
        in_specs=[tile_t(MLSTM_WIDTH), row(MLSTM_WIDTH), tile_t(MLSTM_HEADS * V_ROWS),
                  row(MLSTM_WIDTH), pl.BlockSpec((GATE_ROWS, tile), lambda b, j: (0, b * ns + j)),
                  row(LANES), _const_spec(norm.shape)],
        out_specs=row(MLSTM_WIDTH),
        scratch_shapes=[
            pltpu.VMEM((MLSTM_HEADS, V_ROWS, MLSTM_HEAD_DIM), F32),
            pltpu.VMEM((MLSTM_HEADS, SUBLANES, LANES), F32),
        ],
        compiler_params=pltpu.CompilerParams(dimension_semantics=("arbitrary", "arbitrary"),
                                             vmem_limit_bytes=VMEM_LIMIT),
        name="mlstm",
    )(mqt, mk, mvt, mo, grow, keycol, norm)


def _out_ffn_body(x_ref, mla_ref, mlstm_ref, p_ref, w_out_ref, g_post_ref, g_fpre_ref, g_fpost_ref,
                  w_gate_ref, w_up_ref, w_down_ref, w_pg_ref, w_pp_ref, o_ref):
    tm = x_ref.shape[0]
    subs = [slice(r0, r0 + SUB_OUT) for r0 in range(0, tm, SUB_OUT)]
    mla_t = [mla_ref[s.start // TILE_T, :, s.start % TILE_T:s.start % TILE_T + SUB_OUT] for s in subs]
    mix = [_dot_tn(a, w_out_ref[0:MLA_WIDTH, :])
           + _dot(mlstm_ref[s, :], w_out_ref[MLA_WIDTH:MLA_WIDTH + MLSTM_WIDTH, :])
           for s, a in zip(subs, mla_t)]
    h1 = [x_ref[s, :] + _rms(m, g_post_ref[...]) for s, m in zip(subs, mix)]
    f = [_rms(h, g_fpre_ref[...]).astype(BF16) for h in h1]
    ffn = [None] * len(subs)
    off = 0
    for width in FF_CHUNKS:
        cols = slice(off, off + width)
        gate = [_dot(v, w_gate_ref[:, cols]) for v in f]
        up = [_dot(v, w_up_ref[:, cols]) for v in f]
        act = [(g * jax.nn.sigmoid(g) * u).astype(BF16) for g, u in zip(gate, up)]
        part = [_dot(a, w_down_ref[cols, :]) for a in act]
        ffn = [p if acc is None else acc + p for acc, p in zip(ffn, part)]
        off += width
    h2 = [h + _rms(v, g_fpost_ref[...]) for h, v in zip(h1, ffn)]
    pgate = [jax.nn.sigmoid(_dot(h.astype(BF16), w_pg_ref[...])) for h in h2]
    for s, h, g in zip(subs, h2, pgate):
        o_ref[s, :] = h + g * _dot(p_ref[s, :].astype(BF16), w_pp_ref[...])


def _out_ffn(x2, mla, mlstm, p2, w_out, g_post, g_fpre, g_fpost, w_gate, w_up, w_down, w_pg, w_pp):
    t = x2.shape[0]
    tm = TM_OUT
    row = lambda width: pl.BlockSpec((tm, width), lambda i: (i, 0))
    consts = (w_out, g_post, g_fpre, g_fpost, w_gate, w_up, w_down, w_pg, w_pp)
    return pl.pallas_call(
        _out_ffn_body,
        out_shape=jax.ShapeDtypeStruct((t, D_MODEL), F32),
        grid=(t // tm,),
        in_specs=[row(D_MODEL), pl.BlockSpec((tm // TILE_T, MLA_WIDTH, TILE_T), lambda i: (i, 0, 0)),
                  row(MLSTM_WIDTH), row(PLE_DIM)]
        + [_const_spec(c.shape) for c in consts],
        out_specs=row(D_MODEL),
        compiler_params=pltpu.CompilerParams(dimension_semantics=("arbitrary",),
                                             vmem_limit_bytes=VMEM_LIMIT),
        name="out_ffn",
    )(x2, mla, mlstm, p2, *consts)


def _pack_w_in(w_in):
    o_krope = Q_LORA_RANK + KV_LORA_RANK
    o_mq = o_krope + QK_ROPE_DIM
    o_mv = o_mq + 2 * MLSTM_WIDTH
    o_mo = o_mv + MLSTM_WIDTH
    o_gates = o_mo + MLSTM_WIDTH
    main = jnp.concatenate([w_in[:, :o_krope], w_in[:, o_mq:o_mv], w_in[:, o_mo:o_gates]], axis=1)
    gap = jnp.zeros((D_MODEL, SUBLANES - MLSTM_HEADS), w_in.dtype)
    tail = jnp.concatenate([w_in[:, o_krope:o_mq], w_in[:, o_gates:o_gates + MLSTM_HEADS], gap,
                            w_in[:, o_gates + MLSTM_HEADS:], gap], axis=1).T
    tail = jnp.pad(tail, ((0, LANES - tail.shape[0]), (0, 0)))
    mv_t = w_in[:, o_mv:o_mo].T.reshape(MLSTM_HEADS, MLSTM_HEAD_DIM, D_MODEL)
    mv_t = jnp.pad(mv_t, ((0, 0), (0, V_ROWS - MLSTM_HEAD_DIM), (0, 0)))
    return main.astype(BF16), tail.astype(BF16), mv_t.reshape(MLSTM_HEADS * V_ROWS, D_MODEL).astype(BF16)


def _pack_w_uq_t(w_uq):
    w = w_uq.T.reshape(MLA_HEADS, QK_HEAD_DIM, Q_LORA_RANK)
    w = jnp.pad(w, ((0, 0), (0, QK_PAD - QK_HEAD_DIM), (0, 0)))
    return w.reshape(MLA_HEADS * QK_PAD, Q_LORA_RANK).astype(BF16)


def _pack_w_ukv(w_ukv):
    w = w_ukv.reshape(KV_LORA_RANK, MLA_HEADS, QK_NOPE_DIM + V_HEAD_DIM)
    w_uk = w[:, :, :QK_NOPE_DIM].reshape(KV_LORA_RANK, MLA_HEADS * QK_NOPE_DIM)
    w_uv_t = jnp.transpose(w[:, :, QK_NOPE_DIM:], (1, 2, 0))
    w_uv_t = jnp.pad(w_uv_t, ((0, 0), (0, V_ROWS - V_HEAD_DIM), (0, 0)))
    return w_uk.astype(BF16), w_uv_t.reshape(MLA_HEADS * V_ROWS, KV_LORA_RANK).astype(BF16)


def _layer(h, p_i, pos_row, invf, attn_pre_norm, attn_post_norm, w_in, q_norm, kv_norm, w_uq, w_ukv,
           conv_w, conv_b, gate_bias_i, gate_bias_f, mlstm_norm, w_out, ffn_pre_norm, ffn_post_norm,
           w_gate, w_up, w_down, w_ple_proj, w_ple_gate, batch, seq):
    row = lambda a: a.reshape(1, -1)
    w_in_p, w_tail, w_mv_t = _pack_w_in(w_in)
    w_uk, w_uv_t = _pack_w_ukv(w_ukv)
    gap = jnp.zeros((SUBLANES - MLSTM_HEADS,), F32)
    gbias = jnp.concatenate([gate_bias_i, gap, gate_bias_f, gap]).astype(F32).reshape(2 * SUBLANES, 1)
    vones = jnp.zeros((MLA_HEADS, V_ROWS, 1), F32).at[:, V_HEAD_DIM, 0].set(1.0)
    qt, k, vt, mqt, mk, mvt, mo, grow, keycol = _in_proj(
        h, pos_row, invf, row(attn_pre_norm), w_in_p, w_tail, row(q_norm), row(kv_norm),
        _pack_w_uq_t(w_uq), w_uk, w_uv_t, w_mv_t, vones.reshape(MLA_HEADS * V_ROWS, 1), conv_w,
        row(conv_b), gbias, seq)
    mla = _attention(qt, k, vt, batch, seq)
    mlstm = _mlstm(mqt, mk, mvt, mo, grow, keycol, row(mlstm_norm), batch, seq)
    return _out_ffn(h, mla, mlstm, p_i, w_out.astype(BF16), row(attn_post_norm), row(ffn_pre_norm),
                    row(ffn_post_norm), w_gate.astype(BF16), w_up.astype(BF16),
                    w_down.astype(BF16), w_ple_gate.astype(BF16), w_ple_proj.astype(BF16))


@jax.jit
def kernel(x, p, positions, attn_pre_norm, attn_post_norm, w_in, q_norm, kv_norm, w_uq, w_ukv,
           conv_w, conv_b, gate_bias_i, gate_bias_f, mlstm_norm, w_out, ffn_pre_norm, ffn_post_norm,
           w_gate, w_up, w_down, w_ple_proj, w_ple_gate):
    batch, seq, _ = x.shape
    t = batch * seq
    depth = p.shape[0]
    assert x.shape[2] == D_MODEL and w_gate.shape[1:] == (D_MODEL, D_FF) and sum(FF_CHUNKS) == D_FF
    assert seq % (MLSTM_TILES * TILE_T) == 0 and TM_IN % TILE_T == 0 and TM_OUT % TILE_T == 0
    inv_freq = ROPE_THETA ** (-jnp.arange(0, QK_ROPE_DIM, 2, dtype=F32) / QK_ROPE_DIM)
    invf = inv_freq.reshape(ROPE_HALF, 1)
    pos_row = positions.astype(F32).reshape(1, t)
    h = x.reshape(t, D_MODEL)
    for i in range(depth):
        h = _layer(h, p[i].reshape(t, PLE_DIM), pos_row, invf, attn_pre_norm[i], attn_post_norm[i],
                   w_in[i], q_norm[i], kv_norm[i], w_uq[i], w_ukv[i], conv_w[i], conv_b[i],
                   gate_bias_i[i], gate_bias_f[i], mlstm_norm[i], w_out[i], ffn_pre_norm[i],
                   ffn_post_norm[i], w_gate[i], w_up[i], w_down[i], w_ple_proj[i], w_ple_gate[i],
                   batch, seq)
    return h.reshape(batch, seq, D_MODEL)
```

```python
import functools
import math

import jax
import jax.numpy as jnp
from jax import lax
from jax.experimental import pallas as pl
from jax.experimental.pallas import tpu as pltpu

F32 = jnp.float32
BF16 = jnp.bfloat16

D_MODEL = 1024
PLE_DIM = 256
MLA_HEADS = 4
QK_NOPE_DIM = 128
QK_ROPE_DIM = 64
QK_HEAD_DIM = QK_NOPE_DIM + QK_ROPE_DIM
V_HEAD_DIM = 128
Q_LORA_RANK = 256
KV_LORA_RANK = 128
ROPE_THETA = 10000.0
MLA_WIDTH = MLA_HEADS * V_HEAD_DIM
MLSTM_HEADS = 4
MLSTM_HEAD_DIM = 128
MLSTM_WIDTH = MLSTM_HEADS * MLSTM_HEAD_DIM
CONV_WIDTH = 4
CONV_STRIDE = 4
D_FF = 2816
EPS = 1e-6

LANES = 128
SUBLANES = 8
BF16_ROWS = 16
QK_PAD = 2 * LANES
V_ROWS = V_HEAD_DIM + BF16_ROWS
ROPE_HALF = QK_ROPE_DIM // 2
VMEM_LIMIT = 56 * 1024 * 1024

C_Q = 0
C_KV = C_Q + Q_LORA_RANK
C_MQK = C_KV + KV_LORA_RANK
C_MO = C_MQK + 2 * MLSTM_WIDTH
GATE_I = QK_ROPE_DIM
ROW_KEY, ROW_KMAX, ROW_B, GATE_ROWS = 0, SUBLANES, 2 * SUBLANES, 3 * SUBLANES

TILE_T = 512
TM_IN = 512
SUB_IN = 256
TQ = TILE_T
HEADS_PER_STEP = 4
MAX_STALE_EXCESS = 64.0
CHUNK = 256
MLSTM_TILES = 4
TM_OUT = 512
SUB_OUT = 256
FF_CHUNKS = (1024, 1024, 768)


def _rms(x, g):
    return x * lax.rsqrt(jnp.mean(x * x, axis=-1, keepdims=True) + EPS) * g


def _dot(a, b):
    return jnp.dot(a, b, preferred_element_type=F32)


def _dot_nt(a, b):
    return lax.dot_general(a, b, (((1,), (1,)), ((), ())), preferred_element_type=F32)


def _dot_tn(a, b):
    return lax.dot_general(a, b, (((0,), (0,)), ((), ())), preferred_element_type=F32)


def _const_spec(shape):
    return pl.BlockSpec(shape, lambda *_: (0,) * len(shape), pipeline_mode=pl.Buffered(1))


def _in_proj_body(x_ref, pos_ref, invf_ref, g_pre_ref, w_in_ref, w_tail_ref, qn_ref, kvn_ref,
                  w_uq_ref, w_uk_ref, w_uv_ref, w_mv_ref, vones_ref, conv_w_ref, conv_b_ref, gbias_ref,
                  qt_out, k_out, vt_out, mqt_out, mk_out, mvt_out, mo_out, grow_out, keycol_out,
                  ext_ref, stage_ref, *, tm, sub, tiles_per_seq, chunk):
    assert tm % sub == 0 and sub % chunk == 0
    i = pl.program_id(0)
    halo = SUBLANES

    @pl.when(lax.rem(i, tiles_per_seq) == 0)
    def _():
        ext_ref[:, 0:halo, :] = jnp.zeros((ext_ref.shape[0], halo, LANES), F32)

    for r0 in range(0, tm, sub):
        _in_proj_rows(r0, sub, halo, chunk, x_ref, pos_ref, invf_ref, g_pre_ref, w_in_ref, w_tail_ref,
                      qn_ref, kvn_ref, w_uq_ref, w_uk_ref, w_uv_ref, w_mv_ref, vones_ref, conv_w_ref,
                      conv_b_ref, gbias_ref, qt_out, k_out, vt_out, mqt_out, mk_out, mvt_out, mo_out,
                      grow_out, keycol_out, ext_ref, stage_ref)
    ext_ref[:, 0:halo, :] = ext_ref[:, tm:tm + halo, :]


def _in_proj_rows(r0, ts, halo, chunk, x_ref, pos_ref, invf_ref, g_pre_ref, w_in_ref, w_tail_ref,
                  qn_ref, kvn_ref, w_uq_ref, w_uk_ref, w_uv_ref, w_mv_ref, vones_ref, conv_w_ref,
                  conv_b_ref, gbias_ref, qt_out, k_out, vt_out, mqt_out, mk_out, mvt_out, mo_out,
                  grow_out, keycol_out, ext_ref, stage_ref):
    rows = slice(r0, r0 + ts)
    tile, lanes = r0 // TILE_T, slice(r0 % TILE_T, r0 % TILE_T + ts)
    u = _rms(x_ref[rows, :], g_pre_ref[...]).astype(BF16)

    cq_raw = _dot(u, w_in_ref[:, C_Q:C_Q + Q_LORA_RANK])
    ckv_raw = _dot(u, w_in_ref[:, C_KV:C_KV + KV_LORA_RANK])
    tail_t = _dot_nt(w_tail_ref[...], u)
    z = _dot(u, w_in_ref[:, C_MQK:C_MQK + 2 * MLSTM_WIDTH])
    for c in range(2 * MLSTM_WIDTH // LANES):
        ext_ref[c, halo + r0:halo + r0 + ts, :] = z[:, c * LANES:(c + 1) * LANES]
    mvt_out[tile, :, lanes] = (_dot_nt(w_mv_ref[...], u) + vones_ref[...]).astype(BF16)
    mo_out[rows, :] = jax.nn.sigmoid(_dot(u, w_in_ref[:, C_MO:C_MO + MLSTM_WIDTH])).astype(BF16)
    cq = _rms(cq_raw, qn_ref[...]).astype(BF16)
    ckv = _rms(ckv_raw, kvn_ref[...]).astype(BF16)
    qt = _dot_nt(w_uq_ref[...], cq)
    kn = _dot(ckv, w_uk_ref[...])
    vt_out[tile, :, lanes] = (_dot_nt(w_uv_ref[...], ckv) + vones_ref[...]).astype(BF16)

    ang = invf_ref[...] * pos_ref[:, rows]
    cos = jnp.cos(ang)
    sin = jnp.sin(ang)

    def rope_t(t1, t2):
        return t1 * cos - t2 * sin, t2 * cos + t1 * sin

    scale = QK_HEAD_DIM ** -0.5 * math.log2(math.e)
    for h in range(MLA_HEADS):
        o = h * QK_PAD
        pe = o + QK_NOPE_DIM
        r1, r2 = rope_t(qt[pe:pe + ROPE_HALF], qt[pe + ROPE_HALF:pe + QK_ROPE_DIM])
        qt_out[tile, o:pe, lanes] = (qt[o:pe] * scale).astype(BF16)
        qt_out[tile, pe:pe + ROPE_HALF, lanes] = (r1 * scale).astype(BF16)
        qt_out[tile, pe + ROPE_HALF:pe + QK_ROPE_DIM, lanes] = (r2 * scale).astype(BF16)
        qt_out[tile, pe + QK_ROPE_DIM:o + QK_PAD, lanes] = jnp.zeros((QK_PAD - QK_HEAD_DIM, ts), BF16)

    r1, r2 = rope_t(tail_t[0:ROPE_HALF], tail_t[ROPE_HALF:QK_ROPE_DIM])
    kpe_t = jnp.concatenate([r1, r2, jnp.zeros((LANES - QK_ROPE_DIM, ts), F32)], axis=0)
    kpe = kpe_t.T.astype(BF16)
    for h in range(MLA_HEADS):
        o = h * QK_PAD
        k_out[rows, o:o + LANES] = kn[:, h * LANES:(h + 1) * LANES].astype(BF16)
        k_out[rows, o + LANES:o + QK_PAD] = kpe

    gates = tail_t[GATE_I:GATE_I + 2 * SUBLANES] + gbias_ref[...]
    pos_in_chunk = lax.rem(r0 + lax.broadcasted_iota(jnp.int32, (SUBLANES, ts), 1), chunk)

    def chunk_scan(v, op, identity):
        shift = 1
        while shift < chunk:
            v = op(v, jnp.where(pos_in_chunk >= shift, pltpu.roll(v, shift, 1), identity))
            shift *= 2
        return v

    b8 = chunk_scan(jax.nn.log_sigmoid(gates[SUBLANES:]), jnp.add, 0.0)
    key8 = gates[:SUBLANES] - b8
    grow_out[:, rows] = jnp.concatenate([key8, chunk_scan(key8, jnp.maximum, -jnp.inf), b8], axis=0)
    keycol_out[rows, :] = jnp.concatenate([key8, jnp.zeros((LANES - SUBLANES, ts), F32)], axis=0).T

    n = ts // CONV_STRIDE
    for c in range(2 * MLSTM_WIDTH // LANES):
        cols = slice(c * LANES, (c + 1) * LANES)
        taps = {q: ext_ref[c, pl.ds(halo + r0 + q, n, stride=CONV_STRIDE), :]
                for q in range(1 - CONV_WIDTH, CONV_STRIDE)}
        for r in range(CONV_STRIDE):
            acc = conv_b_ref[:, cols]
            for j in range(CONV_WIDTH):
                acc = acc + conv_w_ref[j:j + 1, cols] * taps[r - (CONV_WIDTH - 1) + j]
            stage_ref[c, pl.ds(r0 + r, n, stride=CONV_STRIDE), :] = acc * jax.nn.sigmoid(acc)
    for c in range(MLSTM_WIDTH // LANES):
        cols = slice(c * LANES, (c + 1) * LANES)
        mqt_out[tile, cols, lanes] = stage_ref[c, rows, :].T.astype(BF16)
        k_slab = stage_ref[MLSTM_WIDTH // LANES + c, rows, :]
        mk_out[rows, cols] = (k_slab * (MLSTM_HEAD_DIM ** -0.5)).astype(BF16)


def _in_proj(x2, pos_row, invf, g_pre, w_in_p, w_tail, qn, kvn, w_uq_t, w_uk, w_uv_t, w_mv_t, vones,
             conv_w, conv_b, gbias, seq):
    t = x2.shape[0]
    tm = TM_IN
    nt = t // TILE_T
    row = lambda width: pl.BlockSpec((tm, width), lambda i: (i, 0))
    tile_t = lambda rows: pl.BlockSpec((tm // TILE_T, rows, TILE_T), lambda i: (i, 0, 0))
    bf = lambda *shape: jax.ShapeDtypeStruct(shape, BF16)
    out_shape = [
        bf(nt, MLA_HEADS * QK_PAD, TILE_T), bf(t, MLA_HEADS * QK_PAD), bf(nt, MLA_HEADS * V_ROWS, TILE_T),
        bf(nt, MLSTM_WIDTH, TILE_T), bf(t, MLSTM_WIDTH), bf(nt, MLSTM_HEADS * V_ROWS, TILE_T),
        bf(t, MLSTM_WIDTH), jax.ShapeDtypeStruct((GATE_ROWS, t), F32),
        jax.ShapeDtypeStruct((t, LANES), F32),
    ]
    out_specs = [
        tile_t(MLA_HEADS * QK_PAD), row(MLA_HEADS * QK_PAD), tile_t(MLA_HEADS * V_ROWS),
        tile_t(MLSTM_WIDTH), row(MLSTM_WIDTH), tile_t(MLSTM_HEADS * V_ROWS), row(MLSTM_WIDTH),
        pl.BlockSpec((GATE_ROWS, tm), lambda i: (0, i)), row(LANES),
    ]
    consts = (invf, g_pre, w_in_p, w_tail, qn, kvn, w_uq_t, w_uk, w_uv_t, w_mv_t, vones, conv_w,
              conv_b, gbias)
    return pl.pallas_call(
        functools.partial(_in_proj_body, tm=tm, sub=SUB_IN, tiles_per_seq=seq // tm, chunk=CHUNK),
        out_shape=out_shape,
        grid=(t // tm,),
        in_specs=[row(D_MODEL), pl.BlockSpec((1, tm), lambda i: (0, i))]
        + [_const_spec(c.shape) for c in consts],
        out_specs=out_specs,
        scratch_shapes=[pltpu.VMEM((2 * MLSTM_WIDTH // LANES, tm + 2 * SUBLANES, LANES), F32),
                        pltpu.VMEM((2 * MLSTM_WIDTH // LANES, tm, LANES), F32)],
        compiler_params=pltpu.CompilerParams(dimension_semantics=("arbitrary",),
                                             vmem_limit_bytes=VMEM_LIMIT),
        name="in_proj",
    )(x2, pos_row, *consts)


def _attn_body(qt_ref, k_ref, vt_ref, o_ref, m_ref, use_ref, viol_ref, acc_ref, pa_ref, pb_ref,
               ua_ref, ub_ref, *, tq, heads):
    qi = pl.program_id(2)
    hq = [slice(h * QK_PAD, (h + 1) * QK_PAD) for h in range(heads)]
    hv = [slice(h * V_ROWS, (h + 1) * V_ROWS) for h in range(heads)]
    stat = m_ref.shape[1:]

    def qk(ki, h):
        start = pl.multiple_of(ki * tq, tq)
        return _dot(k_ref[pl.ds(start, tq), hq[h]], qt_ref[0, hq[h], :])

    def causal(s):
        r = lax.broadcasted_iota(jnp.int32, s.shape, 0)
        c = lax.broadcasted_iota(jnp.int32, s.shape, 1)
        return jnp.where(r <= c, s, -jnp.inf)

    def fast_scores(ki, p_ref, u_ref):
        for h in range(heads):
            s = qk(ki, h)
            m_use = m_ref[h, 0:1, :]
            p_ref[h] = jnp.exp2(s - m_use).astype(BF16)
            cmax = jnp.max(s, axis=0, keepdims=True)
            excess = jnp.where(qi > 0, cmax - m_use, 0.0)
            u_ref[h] = jnp.broadcast_to(m_use, stat)
            viol_ref[h] = jnp.broadcast_to(jnp.maximum(viol_ref[h, 0:1, :], excess), stat)
            m_ref[h] = jnp.broadcast_to(jnp.maximum(m_use, cmax), stat)

    def fast_pv(ki, p_ref, u_ref):
        for h in range(heads):
            m_use = u_ref[h, 0:1, :]
            alpha = jnp.exp2(use_ref[h, 0:1, :] - m_use)
            acc_ref[h] = alpha * acc_ref[h] + _dot(vt_ref[ki, hv[h], :], p_ref[h])
            use_ref[h] = jnp.broadcast_to(m_use, stat)

    def exact_block(ki, diagonal):
        for h in range(heads):
            s = causal(qk(ki, h)) if diagonal else qk(ki, h)
            m_old = m_ref[h, 0:1, :]
            m_new = jnp.maximum(m_old, jnp.max(s, axis=0, keepdims=True))
            p = jnp.exp2(s - m_new).astype(BF16)
            acc_ref[h] = jnp.exp2(m_old - m_new) * acc_ref[h] + _dot(vt_ref[ki, hv[h], :], p)
            m_ref[h] = jnp.broadcast_to(m_new, stat)

    def start():
        viol_ref[...] = jnp.zeros(viol_ref.shape, F32)
        s_diag = [causal(qk(qi, h)) for h in range(heads)]
        for h in range(heads):
            cmax = jnp.broadcast_to(jnp.max(s_diag[h], axis=0, keepdims=True), stat)
            m_ref[h] = cmax
            use_ref[h] = cmax
        fast_scores(0, pa_ref, ua_ref)
        for h in range(heads):
            p = jnp.exp2(s_diag[h] - use_ref[h, 0:1, :]).astype(BF16)
            acc_ref[h] = _dot(vt_ref[qi, hv[h], :], p)

    bufs = ((pa_ref, ua_ref), (pb_ref, ub_ref))

    def run(first, count, score_next):
        for i in range(count):
            if i + 1 < count or score_next:
                fast_scores(first + i + 1, *bufs[(i + 1) % 2])
            fast_pv(first + i, *bufs[i % 2])

    def four(j, carry):
        run(4 * j, 4, True)
        return carry

    for short in range(5):
        @pl.when(qi == short)
        def _(short=short):
            start()
            run(0, short, False)

    @pl.when(qi > 4)
    def _():
        start()
        run(0, 4, True)

    n_loop = jnp.maximum(qi - 1, 0) // 4
    lax.fori_loop(1, n_loop, four, 0)
    for left in range(1, 5):
        @pl.when(jnp.logical_and(qi > 4, qi - 4 * n_loop == left))
        def _(left=left):
            run(4 * n_loop, left, False)

    @pl.when(jnp.max(viol_ref[...]) > MAX_STALE_EXCESS)
    def _():
        m_ref[...] = jnp.full(m_ref.shape, -jnp.inf, F32)
        acc_ref[...] = jnp.zeros(acc_ref.shape, F32)

        def blk(ki, carry):
            exact_block(ki, False)
            return carry

        lax.fori_loop(0, qi, blk, 0)
        exact_block(qi, True)

    for h in range(heads):
        acc = acc_ref[h]
        out_t = acc[0:V_HEAD_DIM] / acc[V_HEAD_DIM:V_HEAD_DIM + 1]
        o_ref[0, h * V_HEAD_DIM:(h + 1) * V_HEAD_DIM, :] = out_t.astype(o_ref.dtype)


def _attention(qt, k, vt, batch, seq):
    t = k.shape[0]
    tq = TQ
    nq = seq // tq
    heads = HEADS_PER_STEP
    return pl.pallas_call(
        functools.partial(_attn_body, tq=tq, heads=heads),
        out_shape=jax.ShapeDtypeStruct((t // tq, MLA_WIDTH, tq), BF16),
        grid=(batch, MLA_HEADS // heads, nq),
        in_specs=[
            pl.BlockSpec((1, heads * QK_PAD, tq), lambda b, g, i: (b * nq + i, g, 0)),
            pl.BlockSpec((seq, heads * QK_PAD), lambda b, g, i: (b, g), pipeline_mode=pl.Buffered(1)),
            pl.BlockSpec((nq, heads * V_ROWS, tq), lambda b, g, i: (b, g, 0)),
        ],
        out_specs=pl.BlockSpec((1, heads * V_HEAD_DIM, tq), lambda b, g, i: (b * nq + i, g, 0)),
        scratch_shapes=[pltpu.VMEM((heads, SUBLANES, tq), F32),
                        pltpu.VMEM((heads, SUBLANES, tq), F32),
                        pltpu.VMEM((heads, SUBLANES, tq), F32),
                        pltpu.VMEM((heads, V_ROWS, tq), F32),
                        pltpu.VMEM((heads, tq, tq), BF16),
                        pltpu.VMEM((heads, tq, tq), BF16),
                        pltpu.VMEM((heads, SUBLANES, tq), F32),
                        pltpu.VMEM((heads, SUBLANES, tq), F32)],
        compiler_params=pltpu.CompilerParams(
            dimension_semantics=("arbitrary", "arbitrary", "arbitrary"),
            vmem_limit_bytes=VMEM_LIMIT),
        name="mla_attention",
    )(qt, k, vt)


def _mlstm_body(qt_ref, k_ref, vt_ref, og_ref, grow_ref, keycol_ref, norm_ref, out_ref, ct_ref,
                m_ref, *, chunk, chunks_per_step):
    @pl.when(pl.program_id(1) == 0)
    def _():
        ct_ref[...] = jnp.zeros(ct_ref.shape, F32)
        m_ref[...] = jnp.zeros(m_ref.shape, F32)

    heads = range(MLSTM_HEADS)
    hs = [slice(h * MLSTM_HEAD_DIM, (h + 1) * MLSTM_HEAD_DIM) for h in heads]
    vs = [slice(h * V_ROWS, (h + 1) * V_ROWS) for h in heads]
    tok = [slice(c * chunk, (c + 1) * chunk) for c in range(chunks_per_step)]
    ttok = [(c * chunk // TILE_T, slice(c * chunk % TILE_T, c * chunk % TILE_T + chunk))
            for c in range(chunks_per_step)]
    r = lax.broadcasted_iota(jnp.int32, (chunk, chunk), 0)
    c = lax.broadcasted_iota(jnp.int32, (chunk, chunk), 1)
    upper = r <= c

    s_raw = [[_dot(k_ref[t, hs[h]], qt_ref[tile, hs[h], lanes]) for h in heads]
             for t, (tile, lanes) in zip(tok, ttok)]
    ct = [ct_ref[h] for h in heads]
    m_prev = [m_ref[h, 0:1, 0:1] for h in heads]
    for ci, (t, (tile, lanes)) in enumerate(zip(tok, ttok)):
        qt = [qt_ref[tile, hs[h], lanes] for h in heads]
        vt = [vt_ref[tile, vs[h], lanes] for h in heads]
        cq = [_dot(ct[h].astype(BF16), qt[h]) for h in heads]
        grow = grow_ref[:, t]
        mm = [jnp.maximum(m_prev[h], grow[ROW_KMAX + h:ROW_KMAX + h + 1]) for h in heads]
        b_row = [grow[ROW_B + h:ROW_B + h + 1] for h in heads]
        inter = [jnp.exp(m_prev[h] - mm[h]) for h in heads]
        for h in heads:
            mm_last = mm[h][:, chunk - 1:chunk]
            w_row = jnp.exp(grow[ROW_KEY + h:ROW_KEY + h + 1] - mm_last)
            vtw = (vt[h].astype(F32) * w_row).astype(BF16)
            ct[h] = jnp.exp(m_prev[h] - mm_last) * ct[h] + _dot(vtw, k_ref[t, hs[h]])
            m_prev[h] = b_row[h][:, chunk - 1:chunk] + mm_last
        keycol = keycol_ref[t, :]
        for h in heads:
            p = jnp.exp(jnp.where(upper, keycol[:, h:h + 1] - mm[h], -jnp.inf))
            st = (s_raw[ci][h] * p).astype(BF16)
            tot = _dot(vt[h], st) + inter[h] * cq[h]
            den = tot[MLSTM_HEAD_DIM:MLSTM_HEAD_DIM + 1]
            ht = tot[0:MLSTM_HEAD_DIM] / jnp.maximum(jnp.abs(den), jnp.exp(-(b_row[h] + mm[h])))
            hn_t = ht * lax.rsqrt(jnp.mean(ht * ht, axis=0, keepdims=True) + EPS)
            out_ref[t, hs[h]] = (hn_t.T * norm_ref[0:1, hs[h]]
                                 * og_ref[t, hs[h]].astype(F32)).astype(out_ref.dtype)
    for h in heads:
        ct_ref[h] = ct[h]
        m_ref[h] = jnp.broadcast_to(m_prev[h], m_ref.shape[1:])


def _mlstm(mqt, mk, mvt, mo, grow, keycol, norm, batch, seq):
    t = mk.shape[0]
    tile = MLSTM_TILES * TILE_T
    ns = seq // tile
    row = lambda width: pl.BlockSpec((tile, width), lambda b, j: (b * ns + j, 0))
    tile_t = lambda rows: pl.BlockSpec((MLSTM_TILES, rows, TILE_T), lambda b, j: (b * ns + j, 0, 0))
    return pl.pallas_call(
        functools.partial(_mlstm_body, chunk=CHUNK, chunks_per_step=tile // CHUNK),
        out_shape=jax.ShapeDtypeStruct((t, MLSTM_WIDTH), BF16),
        grid=(batch, ns),
        in_specs=[tile_t(MLSTM_WIDTH), row(MLSTM_WIDTH), tile_t(MLSTM_HEADS * V_ROWS),
                  row(MLSTM_WIDTH), pl.BlockSpec((GATE_ROWS, tile), lambda b, j: (0, b * ns + j)),
                  row(LANES), _const_spec(norm.shape)],
        out_specs=row(MLSTM_WIDTH),
        scratch_shapes=[
            pltpu.VMEM((MLSTM_HEADS, V_ROWS, MLSTM_HEAD_DIM), F32),
            pltpu.VMEM((MLSTM_HEADS, SUBLANES, LANES), F32),
        ],
        compiler_params=pltpu.CompilerParams(dimension_semantics=("arbitrary", "arbitrary"),
                                             vmem_limit_bytes=VMEM_LIMIT),
        name="mlstm",
    )(mqt, mk, mvt, mo, grow, keycol, norm)


def _out_ffn_body(x_ref, mla_ref, mlstm_ref, p_ref, w_out_ref, g_post_ref, g_fpre_ref, g_fpost_ref,
                  w_gate_ref, w_up_ref, w_down_ref, w_pg_ref, w_pp_ref, o_ref):
    tm = x_ref.shape[0]
    subs = [slice(r0, r0 + SUB_OUT) for r0 in range(0, tm, SUB_OUT)]
    mla_t = [mla_ref[s.start // TILE_T, :, s.start % TILE_T:s.start % TILE_T + SUB_OUT] for s in subs]
    mix = [_dot_tn(a, w_out_ref[0:MLA_WIDTH, :])
           + _dot(mlstm_ref[s, :], w_out_ref[MLA_WIDTH:MLA_WIDTH + MLSTM_WIDTH, :])
           for s, a in zip(subs, mla_t)]
    h1 = [x_ref[s, :] + _rms(m, g_post_ref[...]) for s, m in zip(subs, mix)]
    f = [_rms(h, g_fpre_ref[...]).astype(BF16) for h in h1]
    ffn = [None] * len(subs)
    off = 0
    for width in FF_CHUNKS:
        cols = slice(off, off + width)
        gate = [_dot(v, w_gate_ref[:, cols]) for v in f]
        up = [_dot(v, w_up_ref[:, cols]) for v in f]
        act = [(g * jax.nn.sigmoid(g) * u).astype(BF16) for g, u in zip(gate, up)]
        part = [_dot(a, w_down_ref[cols, :]) for a in act]
        ffn = [p if acc is None else acc + p for acc, p in zip(ffn, part)]
        off += width
    h2 = [h + _rms(v, g_fpost_ref[...]) for h, v in zip(h1, ffn)]
    pgate = [jax.nn.sigmoid(_dot(h.astype(BF16), w_pg_ref[...])) for h in h2]
    for s, h, g in zip(subs, h2, pgate):
        o_ref[s, :] = h + g * _dot(p_ref[s, :].astype(BF16), w_pp_ref[...])


def _out_ffn(x2, mla, mlstm, p2, w_out, g_post, g_fpre, g_fpost, w_gate, w_up, w_down, w_pg, w_pp):
    t = x2.shape[0]
    tm = TM_OUT
    row = lambda width: pl.BlockSpec((tm, width), lambda i: (i, 0))
    consts = (w_out, g_post, g_fpre, g_fpost, w_gate, w_up, w_down, w_pg, w_pp)
    return pl.pallas_call(
        _out_ffn_body,
        out_shape=jax.ShapeDtypeStruct((t, D_MODEL), F32),
        grid=(t // tm,),
        in_specs=[row(D_MODEL), pl.BlockSpec((tm // TILE_T, MLA_WIDTH, TILE_T), lambda i: (i, 0, 0)),
                  row(MLSTM_WIDTH), row(PLE_DIM)]
        + [_const_spec(c.shape) for c in consts],
        out_specs=row(D_MODEL),
        compiler_params=pltpu.CompilerParams(dimension_semantics=("arbitrary",),
                                             vmem_limit_bytes=VMEM_LIMIT),
        name="out_ffn",
    )(x2, mla, mlstm, p2, *consts)


def _pack_w_in(w_in):
    o_krope = Q_LORA_RANK + KV_LORA_RANK
    o_mq = o_krope + QK_ROPE_DIM
    o_mv = o_mq + 2 * MLSTM_WIDTH
    o_mo = o_mv + MLSTM_WIDTH
    o_gates = o_mo + MLSTM_WIDTH
    main = jnp.concatenate([w_in[:, :o_krope], w_in[:, o_mq:o_mv], w_in[:, o_mo:o_gates]], axis=1)
    gap = jnp.zeros((D_MODEL, SUBLANES - MLSTM_HEADS), w_in.dtype)
    tail = jnp.concatenate([w_in[:, o_krope:o_mq], w_in[:, o_gates:o_gates + MLSTM_HEADS], gap,
                            w_in[:, o_gates + MLSTM_HEADS:], gap], axis=1).T
    tail = jnp.pad(tail, ((0, LANES - tail.shape[0]), (0, 0)))
    mv_t = w_in[:, o_mv:o_mo].T.reshape(MLSTM_HEADS, MLSTM_HEAD_DIM, D_MODEL)
    mv_t = jnp.pad(mv_t, ((0, 0), (0, V_ROWS - MLSTM_HEAD_DIM), (0, 0)))
    return main.astype(BF16), tail.astype(BF16), mv_t.reshape(MLSTM_HEADS * V_ROWS, D_MODEL).astype(BF16)


def _pack_w_uq_t(w_uq):
    w = w_uq.T.reshape(MLA_HEADS, QK_HEAD_DIM, Q_LORA_RANK)
    w = jnp.pad(w, ((0, 0), (0, QK_PAD - QK_HEAD_DIM), (0, 0)))
    return w.reshape(MLA_HEADS * QK_PAD, Q_LORA_RANK).astype(BF16)


def _pack_w_ukv(w_ukv):
    w = w_ukv.reshape(KV_LORA_RANK, MLA_HEADS, QK_NOPE_DIM + V_HEAD_DIM)
    w_uk = w[:, :, :QK_NOPE_DIM].reshape(KV_LORA_RANK, MLA_HEADS * QK_NOPE_DIM)
    w_uv_t = jnp.transpose(w[:, :, QK_NOPE_DIM:], (1, 2, 0))
    w_uv_t = jnp.pad(w_uv_t, ((0, 0), (0, V_ROWS - V_HEAD_DIM), (0, 0)))
    return w_uk.astype(BF16), w_uv_t.reshape(MLA_HEADS * V_ROWS, KV_LORA_RANK).astype(BF16)


def _layer(h, p_i, pos_row, invf, attn_pre_norm, attn_post_norm, w_in, q_norm, kv_norm, w_uq, w_ukv,
           conv_w, conv_b, gate_bias_i, gate_bias_f, mlstm_norm, w_out, ffn_pre_norm, ffn_post_norm,
           w_gate, w_up, w_down, w_ple_proj, w_ple_gate, batch, seq):
    row = lambda a: a.reshape(1, -1)
    w_in_p, w_tail, w_mv_t = _pack_w_in(w_in)
    w_uk, w_uv_t = _pack_w_ukv(w_ukv)
    gap = jnp.zeros((SUBLANES - MLSTM_HEADS,), F32)
    gbias = jnp.concatenate([gate_bias_i, gap, gate_bias_f, gap]).astype(F32).reshape(2 * SUBLANES, 1)
    vones = jnp.zeros((MLA_HEADS, V_ROWS, 1), F32).at[:, V_HEAD_DIM, 0].set(1.0)
    qt, k, vt, mqt, mk, mvt, mo, grow, keycol = _in_proj(
        h, pos_row, invf, row(attn_pre_norm), w_in_p, w_tail, row(q_norm), row(kv_norm),
        _pack_w_uq_t(w_uq), w_uk, w_uv_t, w_mv_t, vones.reshape(MLA_HEADS * V_ROWS, 1), conv_w,
        row(conv_b), gbias, seq)
    mla = _attention(qt, k, vt, batch, seq)
    mlstm = _mlstm(mqt, mk, mvt, mo, grow, keycol, row(mlstm_norm), batch, seq)
    return _out_ffn(h, mla, mlstm, p_i, w_out.astype(BF16), row(attn_post_norm), row(ffn_pre_norm),
                    row(ffn_post_norm), w_gate.astype(BF16), w_up.astype(BF16),
                    w_down.astype(BF16), w_ple_gate.astype(BF16), w_ple_proj.astype(BF16))


@jax.jit
def kernel(x, p, positions, attn_pre_norm, attn_post_norm, w_in, q_norm, kv_norm, w_uq, w_ukv,
           conv_w, conv_b, gate_bias_i, gate_bias_f, mlstm_norm, w_out, ffn_pre_norm, ffn_post_norm,
           w_gate, w_up, w_down, w_ple_proj, w_ple_gate):
    batch, seq, _ = x.shape
    t = batch * seq
    depth = p.shape[0]
    assert x.shape[2] == D_MODEL and w_gate.shape[1:] == (D_MODEL, D_FF) and sum(FF_CHUNKS) == D_FF
    assert seq % (MLSTM_TILES * TILE_T) == 0 and TM_IN % TILE_T == 0 and TM_OUT % TILE_T == 0
    inv_freq = ROPE_THETA ** (-jnp.arange(0, QK_ROPE_DIM, 2, dtype=F32) / QK_ROPE_DIM)
    invf = inv_freq.reshape(ROPE_HALF, 1)
    pos_row = positions.astype(F32).reshape(1, t)
    h = x.reshape(t, D_MODEL)
    for i in range(depth):
        h = _layer(h, p[i].reshape(t, PLE_DIM), pos_row, invf, attn_pre_norm[i], attn_post_norm[i],
                   w_in[i], q_norm[i], kv_norm[i], w_uq[i], w_ukv[i], conv_w[i], conv_b[i],
                   gate_bias_i[i], gate_bias_f[i], mlstm_norm[i], w_out[i], ffn_pre_norm[i],
                   ffn_post_norm[i], w_gate[i], w_up[i], w_down[i], w_ple_proj[i], w_ple_gate[i],
                   batch, seq)
    return h.reshape(batch, seq, D_MODEL)
```

```python
import functools
import math

import jax
import jax.numpy as jnp
from jax import lax
from jax.experimental import pallas as pl
from jax.experimental.pallas import tpu as pltpu

F32 = jnp.float32
BF16 = jnp.bfloat16

D_MODEL = 1024
PLE_DIM = 256
MLA_HEADS = 4
QK_NOPE_DIM = 128
QK_ROPE_DIM = 64
QK_HEAD_DIM = QK_NOPE_DIM + QK_ROPE_DIM
V_HEAD_DIM = 128
Q_LORA_RANK = 256
KV_LORA_RANK = 128
ROPE_THETA = 10000.0
MLA_WIDTH = MLA_HEADS * V_HEAD_DIM
MLSTM_HEADS = 4
MLSTM_HEAD_DIM = 128
MLSTM_WIDTH = MLSTM_HEADS * MLSTM_HEAD_DIM
CONV_WIDTH = 4
CONV_STRIDE = 4
D_FF = 2816
EPS = 1e-6

LANES = 128
SUBLANES = 8
BF16_ROWS = 16
QK_PAD = 2 * LANES
V_ROWS = V_HEAD_DIM + BF16_ROWS
ROPE_HALF = QK_ROPE_DIM // 2
VMEM_LIMIT = 56 * 1024 * 1024

C_Q = 0
C_KV = C_Q + Q_LORA_RANK
C_MQK = C_KV + KV_LORA_RANK
C_MO = C_MQK + 2 * MLSTM_WIDTH
GATE_I = QK_ROPE_DIM
ROW_KEY, ROW_KMAX, ROW_B, GATE_ROWS = 0, SUBLANES, 2 * SUBLANES, 3 * SUBLANES

TILE_T = 512
TM_IN = 512
SUB_IN = 256
TQ = TILE_T
HEADS_PER_STEP = 4
MAX_STALE_EXCESS = 64.0
CHUNK = 256
MLSTM_TILES = 4
TM_OUT = 512
SUB_OUT = 256
FF_CHUNKS = (1024, 1024, 768)


def _rms(x, g):
    return x * lax.rsqrt(jnp.mean(x * x, axis=-1, keepdims=True) + EPS) * g


def _dot(a, b):
    return jnp.dot(a, b, preferred_element_type=F32)


def _dot_nt(a, b):
    return lax.dot_general(a, b, (((1,), (1,)), ((), ())), preferred_element_type=F32)


def _dot_tn(a, b):
    return lax.dot_general(a, b, (((0,), (0,)), ((), ())), preferred_element_type=F32)


def _const_spec(shape):
    return pl.BlockSpec(shape, lambda *_: (0,) * len(shape), pipeline_mode=pl.Buffered(1))


def _in_proj_body(x_ref, pos_ref, invf_ref, g_pre_ref, w_in_ref, w_tail_ref, qn_ref, kvn_ref,
                  w_uq_ref, w_uk_ref, w_uv_ref, w_mv_ref, vones_ref, conv_w_ref, conv_b_ref, gbias_ref,
                  qt_out, k_out, vt_out, mqt_out, mk_out, mvt_out, mo_out, grow_out, keycol_out,
                  ext_ref, stage_ref, *, tm, sub, tiles_per_seq, chunk):
    assert tm % sub == 0 and sub % chunk == 0
    i = pl.program_id(0)
    halo = SUBLANES

    @pl.when(lax.rem(i, tiles_per_seq) == 0)
    def _():
        ext_ref[:, 0:halo, :] = jnp.zeros((ext_ref.shape[0], halo, LANES), F32)

    for r0 in range(0, tm, sub):
        _in_proj_rows(r0, sub, halo, chunk, x_ref, pos_ref, invf_ref, g_pre_ref, w_in_ref, w_tail_ref,
                      qn_ref, kvn_ref, w_uq_ref, w_uk_ref, w_uv_ref, w_mv_ref, vones_ref, conv_w_ref,
                      conv_b_ref, gbias_ref, qt_out, k_out, vt_out, mqt_out, mk_out, mvt_out, mo_out,
                      grow_out, keycol_out, ext_ref, stage_ref)
    ext_ref[:, 0:halo, :] = ext_ref[:, tm:tm + halo, :]


def _in_proj_rows(r0, ts, halo, chunk, x_ref, pos_ref, invf_ref, g_pre_ref, w_in_ref, w_tail_ref,
                  qn_ref, kvn_ref, w_uq_ref, w_uk_ref, w_uv_ref, w_mv_ref, vones_ref, conv_w_ref,
                  conv_b_ref, gbias_ref, qt_out, k_out, vt_out, mqt_out, mk_out, mvt_out, mo_out,
                  grow_out, keycol_out, ext_ref, stage_ref):
    rows = slice(r0, r0 + ts)
    tile, lanes = r0 // TILE_T, slice(r0 % TILE_T, r0 % TILE_T + ts)
    u = _rms(x_ref[rows, :], g_pre_ref[...]).astype(BF16)

    cq_raw = _dot(u, w_in_ref[:, C_Q:C_Q + Q_LORA_RANK])
    ckv_raw = _dot(u, w_in_ref[:, C_KV:C_KV + KV_LORA_RANK])
    tail_t = _dot_nt(w_tail_ref[...], u)
    z = _dot(u, w_in_ref[:, C_MQK:C_MQK + 2 * MLSTM_WIDTH])
    for c in range(2 * MLSTM_WIDTH // LANES):
        ext_ref[c, halo + r0:halo + r0 + ts, :] = z[:, c * LANES:(c + 1) * LANES]
    mvt_out[tile, :, lanes] = (_dot_nt(w_mv_ref[...], u) + vones_ref[...]).astype(BF16)
    mo_out[rows, :] = jax.nn.sigmoid(_dot(u, w_in_ref[:, C_MO:C_MO + MLSTM_WIDTH])).astype(BF16)
    cq = _rms(cq_raw, qn_ref[...]).astype(BF16)
    ckv = _rms(ckv_raw, kvn_ref[...]).astype(BF16)
    qt = _dot_nt(w_uq_ref[...], cq)
    kn = _dot(ckv, w_uk_ref[...])
    vt_out[tile, :, lanes] = (_dot_nt(w_uv_ref[...], ckv) + vones_ref[...]).astype(BF16)

    ang = invf_ref[...] * pos_ref[:, rows]
    cos = jnp.cos(ang)
    sin = jnp.sin(ang)

    def rope_t(t1, t2):
        return t1 * cos - t2 * sin, t2 * cos + t1 * sin

    scale = QK_HEAD_DIM ** -0.5 * math.log2(math.e)
    for h in range(MLA_HEADS):
        o = h * QK_PAD
        pe = o + QK_NOPE_DIM
        r1, r2 = rope_t(qt[pe:pe + ROPE_HALF], qt[pe + ROPE_HALF:pe + QK_ROPE_DIM])
        qt_out[tile, o:pe, lanes] = (qt[o:pe] * scale).astype(BF16)
        qt_out[tile, pe:pe + ROPE_HALF, lanes] = (r1 * scale).astype(BF16)
        qt_out[tile, pe + ROPE_HALF:pe + QK_ROPE_DIM, lanes] = (r2 * scale).astype(BF16)
        qt_out[tile, pe + QK_ROPE_DIM:o + QK_PAD, lanes] = jnp.zeros((QK_PAD - QK_HEAD_DIM, ts), BF16)

    r1, r2 = rope_t(tail_t[0:ROPE_HALF], tail_t[ROPE_HALF:QK_ROPE_DIM])
    kpe_t = jnp.concatenate([r1, r2, jnp.zeros((LANES - QK_ROPE_DIM, ts), F32)], axis=0)
    kpe = kpe_t.T.astype(BF16)
    for h in range(MLA_HEADS):
        o = h * QK_PAD
        k_out[rows, o:o + LANES] = kn[:, h * LANES:(h + 1) * LANES].astype(BF16)
        k_out[rows, o + LANES:o + QK_PAD] = kpe

    gates = tail_t[GATE_I:GATE_I + 2 * SUBLANES] + gbias_ref[...]
    pos_in_chunk = lax.rem(r0 + lax.broadcasted_iota(jnp.int32, (SUBLANES, ts), 1), chunk)

    def chunk_scan(v, op, identity):
        shift = 1
        while shift < chunk:
            v = op(v, jnp.where(pos_in_chunk >= shift, pltpu.roll(v, shift, 1), identity))
            shift *= 2
        return v

    b8 = chunk_scan(jax.nn.log_sigmoid(gates[SUBLANES:]), jnp.add, 0.0)
    key8 = gates[:SUBLANES] - b8
    grow_out[:, rows] = jnp.concatenate([key8, chunk_scan(key8, jnp.maximum, -jnp.inf), b8], axis=0)
    keycol_out[rows, :] = jnp.concatenate([key8, jnp.zeros((LANES - SUBLANES, ts), F32)], axis=0).T

    n = ts // CONV_STRIDE
    for c in range(2 * MLSTM_WIDTH // LANES):
        cols = slice(c * LANES, (c + 1) * LANES)
        taps = {q: ext_ref[c, pl.ds(halo + r0 + q, n, stride=CONV_STRIDE), :]
                for q in range(1 - CONV_WIDTH, CONV_STRIDE)}
        for r in range(CONV_STRIDE):
            acc = conv_b_ref[:, cols]
            for j in range(CONV_WIDTH):
                acc = acc + conv_w_ref[j:j + 1, cols] * taps[r - (CONV_WIDTH - 1) + j]
            stage_ref[c, pl.ds(r0 + r, n, stride=CONV_STRIDE), :] = acc * jax.nn.sigmoid(acc)
    for c in range(MLSTM_WIDTH // LANES):
        cols = slice(c * LANES, (c + 1) * LANES)
        mqt_out[tile, cols, lanes] = stage_ref[c, rows, :].T.astype(BF16)
        k_slab = stage_ref[MLSTM_WIDTH // LANES + c, rows, :]
        mk_out[rows, cols] = (k_slab * (MLSTM_HEAD_DIM ** -0.5)).astype(BF16)


def _in_proj(x2, pos_row, invf, g_pre, w_in_p, w_tail, qn, kvn, w_uq_t, w_uk, w_uv_t, w_mv_t, vones,
             conv_w, conv_b, gbias, seq):
    t = x2.shape[0]
    tm = TM_IN
    nt = t // TILE_T
    row = lambda width: pl.BlockSpec((tm, width), lambda i: (i, 0))
    tile_t = lambda rows: pl.BlockSpec((tm // TILE_T, rows, TILE_T), lambda i: (i, 0, 0))
    bf = lambda *shape: jax.ShapeDtypeStruct(shape, BF16)
    out_shape = [
        bf(nt, MLA_HEADS * QK_PAD, TILE_T), bf(t, MLA_HEADS * QK_PAD), bf(nt, MLA_HEADS * V_ROWS, TILE_T),
        bf(nt, MLSTM_WIDTH, TILE_T), bf(t, MLSTM_WIDTH), bf(nt, MLSTM_HEADS * V_ROWS, TILE_T),
        bf(t, MLSTM_WIDTH), jax.ShapeDtypeStruct((GATE_ROWS, t), F32),
        jax.ShapeDtypeStruct((t, LANES), F32),
    ]
    out_specs = [
        tile_t(MLA_HEADS * QK_PAD), row(MLA_HEADS * QK_PAD), tile_t(MLA_HEADS * V_ROWS),
        tile_t(MLSTM_WIDTH), row(MLSTM_WIDTH), tile_t(MLSTM_HEADS * V_ROWS), row(MLSTM_WIDTH),
        pl.BlockSpec((GATE_ROWS, tm), lambda i: (0, i)), row(LANES),
    ]
    consts = (invf, g_pre, w_in_p, w_tail, qn, kvn, w_uq_t, w_uk, w_uv_t, w_mv_t, vones, conv_w,
              conv_b, gbias)
    return pl.pallas_call(
        functools.partial(_in_proj_body, tm=tm, sub=SUB_IN, tiles_per_seq=seq // tm, chunk=CHUNK),
        out_shape=out_shape,
        grid=(t // tm,),
        in_specs=[row(D_MODEL), pl.BlockSpec((1, tm), lambda i: (0, i))]
        + [_const_spec(c.shape) for c in consts],
        out_specs=out_specs,
        scratch_shapes=[pltpu.VMEM((2 * MLSTM_WIDTH // LANES, tm + 2 * SUBLANES, LANES), F32),
                        pltpu.VMEM((2 * MLSTM_WIDTH // LANES, tm, LANES), F32)],
        compiler_params=pltpu.CompilerParams(dimension_semantics=("arbitrary",),
                                             vmem_limit_bytes=VMEM_LIMIT),
        name="in_proj",
    )(x2, pos_row, *consts)


def _attn_body(qt_ref, k_ref, vt_ref, o_ref, m_ref, use_ref, viol_ref, acc_ref, pa_ref, pb_ref,
               ua_ref, ub_ref, *, tq, heads):
    qi = pl.program_id(2)
    hq = [slice(h * QK_PAD, (h + 1) * QK_PAD) for h in range(heads)]
    hv = [slice(h * V_ROWS, (h + 1) * V_ROWS) for h in range(heads)]
    stat = m_ref.shape[1:]

    def qk(ki, h):
        start = pl.multiple_of(ki * tq, tq)
        return _dot(k_ref[pl.ds(start, tq), hq[h]], qt_ref[0, hq[h], :])

    def causal(s):
        r = lax.broadcasted_iota(jnp.int32, s.shape, 0)
        c = lax.broadcasted_iota(jnp.int32, s.shape, 1)
        return jnp.where(r <= c, s, -jnp.inf)

    def fast_scores(ki, p_ref, u_ref):
        for h in range(heads):
            s = qk(ki, h)
            m_use = m_ref[h, 0:1, :]
            p_ref[h] = jnp.exp2(s - m_use).astype(BF16)
            cmax = jnp.max(s, axis=0, keepdims=True)
            excess = jnp.where(qi > 0, cmax - m_use, 0.0)
            u_ref[h] = jnp.broadcast_to(m_use, stat)
            viol_ref[h] = jnp.broadcast_to(jnp.maximum(viol_ref[h, 0:1, :], excess), stat)
            m_ref[h] = jnp.broadcast_to(jnp.maximum(m_use, cmax), stat)

    def fast_pv(ki, p_ref, u_ref):
        for h in range(heads):
            m_use = u_ref[h, 0:1, :]
            alpha = jnp.exp2(use_ref[h, 0:1, :] - m_use)
            acc_ref[h] = alpha * acc_ref[h] + _dot(vt_ref[ki, hv[h], :], p_ref[h])
            use_ref[h] = jnp.broadcast_to(m_use, stat)

    def exact_block(ki, diagonal):
        for h in range(heads):
            s = causal(qk(ki, h)) if diagonal else qk(ki, h)
            m_old = m_ref[h, 0:1, :]
            m_new = jnp.maximum(m_old, jnp.max(s, axis=0, keepdims=True))
            p = jnp.exp2(s - m_new).astype(BF16)
            acc_ref[h] = jnp.exp2(m_old - m_new) * acc_ref[h] + _dot(vt_ref[ki, hv[h], :], p)
            m_ref[h] = jnp.broadcast_to(m_new, stat)

    def start():
        viol_ref[...] = jnp.zeros(viol_ref.shape, F32)
        s_diag = [causal(qk(qi, h)) for h in range(heads)]
        for h in range(heads):
            cmax = jnp.broadcast_to(jnp.max(s_diag[h], axis=0, keepdims=True), stat)
            m_ref[h] = cmax
            use_ref[h] = cmax
        fast_scores(0, pa_ref, ua_ref)
        for h in range(heads):
            p = jnp.exp2(s_diag[h] - use_ref[h, 0:1, :]).astype(BF16)
            acc_ref[h] = _dot(vt_ref[qi, hv[h], :], p)

    bufs = ((pa_ref, ua_ref), (pb_ref, ub_ref))

    def run(first, count, score_next):
        for i in range(count):
            if i + 1 < count or score_next:
                fast_scores(first + i + 1, *bufs[(i + 1) % 2])
            fast_pv(first + i, *bufs[i % 2])

    def four(j, carry):
        run(4 * j, 4, True)
        return carry

    for short in range(5):
        @pl.when(qi == short)
        def _(short=short):
            start()
            run(0, short, False)

    @pl.when(qi > 4)
    def _():
        start()
        run(0, 4, True)

    n_loop = jnp.maximum(qi - 1, 0) // 4
    lax.fori_loop(1, n_loop, four, 0)
    for left in range(1, 5):
        @pl.when(jnp.logical_and(qi > 4, qi - 4 * n_loop == left))
        def _(left=left):
            run(4 * n_loop, left, False)

    @pl.when(jnp.max(viol_ref[...]) > MAX_STALE_EXCESS)
    def _():
        m_ref[...] = jnp.full(m_ref.shape, -jnp.inf, F32)
        acc_ref[...] = jnp.zeros(acc_ref.shape, F32)

        def blk(ki, carry):
            exact_block(ki, False)
            return carry

        lax.fori_loop(0, qi, blk, 0)
        exact_block(qi, True)

    for h in range(heads):
        acc = acc_ref[h]
        out_t = acc[0:V_HEAD_DIM] / acc[V_HEAD_DIM:V_HEAD_DIM + 1]
        o_ref[0, h * V_HEAD_DIM:(h + 1) * V_HEAD_DIM, :] = out_t.astype(o_ref.dtype)


def _attention(qt, k, vt, batch, seq):
    t = k.shape[0]
    tq = TQ
    nq = seq // tq
    heads = HEADS_PER_STEP
    return pl.pallas_call(
        functools.partial(_attn_body, tq=tq, heads=heads),
        out_shape=jax.ShapeDtypeStruct((t // tq, MLA_WIDTH, tq), BF16),
        grid=(batch, MLA_HEADS // heads, nq),
        in_specs=[
            pl.BlockSpec((1, heads * QK_PAD, tq), lambda b, g, i: (b * nq + i, g, 0)),
            pl.BlockSpec((seq, heads * QK_PAD), lambda b, g, i: (b, g)),
            pl.BlockSpec((nq, heads * V_ROWS, tq), lambda b, g, i: (b, g, 0),
                         pipeline_mode=pl.Buffered(1)),
        ],
        out_specs=pl.BlockSpec((1, heads * V_HEAD_DIM, tq), lambda b, g, i: (b * nq + i, g, 0)),
        scratch_shapes=[pltpu.VMEM((heads, SUBLANES, tq), F32),
                        pltpu.VMEM((heads, SUBLANES, tq), F32),
                        pltpu.VMEM((heads, SUBLANES, tq), F32),
                        pltpu.VMEM((heads, V_ROWS, tq), F32),
                        pltpu.VMEM((heads, tq, tq), BF16),
                        pltpu.VMEM((heads, tq, tq), BF16),
                        pltpu.VMEM((heads, SUBLANES, tq), F32),
                        pltpu.VMEM((heads, SUBLANES, tq), F32)],
        compiler_params=pltpu.CompilerParams(
            dimension_semantics=("arbitrary", "arbitrary", "arbitrary"),
            vmem_limit_bytes=VMEM_LIMIT),
        name="mla_attention",
    )(qt, k, vt)


def _mlstm_body(qt_ref, k_ref, vt_ref, og_ref, grow_ref, keycol_ref, norm_ref, out_ref, ct_ref,
                m_ref, *, chunk, chunks_per_step):
    @pl.when(pl.program_id(1) == 0)
    def _():
        ct_ref[...] = jnp.zeros(ct_ref.shape, F32)
        m_ref[...] = jnp.zeros(m_ref.shape, F32)

    heads = range(MLSTM_HEADS)
    hs = [slice(h * MLSTM_HEAD_DIM, (h + 1) * MLSTM_HEAD_DIM) for h in heads]
    vs = [slice(h * V_ROWS, (h + 1) * V_ROWS) for h in heads]
    tok = [slice(c * chunk, (c + 1) * chunk) for c in range(chunks_per_step)]
    ttok = [(c * chunk // TILE_T, slice(c * chunk % TILE_T, c * chunk % TILE_T + chunk))
            for c in range(chunks_per_step)]
    r = lax.broadcasted_iota(jnp.int32, (chunk, chunk), 0)
    c = lax.broadcasted_iota(jnp.int32, (chunk, chunk), 1)
    upper = r <= c

    s_raw = [[_dot(k_ref[t, hs[h]], qt_ref[tile, hs[h], lanes]) for h in heads]
             for t, (tile, lanes) in zip(tok, ttok)]
    ct = [ct_ref[h] for h in heads]
    m_prev = [m_ref[h, 0:1, 0:1] for h in heads]
    for ci, (t, (tile, lanes)) in enumerate(zip(tok, ttok)):
        qt = [qt_ref[tile, hs[h], lanes] for h in heads]
        vt = [vt_ref[tile, vs[h], lanes] for h in heads]
        cq = [_dot(ct[h].astype(BF16), qt[h]) for h in heads]
        grow = grow_ref[:, t]
        mm = [jnp.maximum(m_prev[h], grow[ROW_KMAX + h:ROW_KMAX + h + 1]) for h in heads]
        b_row = [grow[ROW_B + h:ROW_B + h + 1] for h in heads]
        inter = [jnp.exp(m_prev[h] - mm[h]) for h in heads]
        for h in heads:
            mm_last = mm[h][:, chunk - 1:chunk]
            w_row = jnp.exp(grow[ROW_KEY + h:ROW_KEY + h + 1] - mm_last)
            vtw = (vt[h].astype(F32) * w_row).astype(BF16)
            ct[h] = jnp.exp(m_prev[h] - mm_last) * ct[h] + _dot(vtw, k_ref[t, hs[h]])
            m_prev[h] = b_row[h][:, chunk - 1:chunk] + mm_last
        keycol = keycol_ref[t, :]
        for h in heads:
            p = jnp.exp(jnp.where(upper, keycol[:, h:h + 1] - mm[h], -jnp.inf))
            st = (s_raw[ci][h] * p).astype(BF16)
            tot = _dot(vt[h], st) + inter[h] * cq[h]
            den = tot[MLSTM_HEAD_DIM:MLSTM_HEAD_DIM + 1]
            ht = tot[0:MLSTM_HEAD_DIM] / jnp.maximum(jnp.abs(den), jnp.exp(-(b_row[h] + mm[h])))
            hn_t = ht * lax.rsqrt(jnp.mean(ht * ht, axis=0, keepdims=True) + EPS)
            out_ref[t, hs[h]] = (hn_t.T * norm_ref[0:1, hs[h]]
                                 * og_ref[t, hs[h]].astype(F32)).astype(out_ref.dtype)
    for h in heads:
        ct_ref[h] = ct[h]
        m_ref[h] = jnp.broadcast_to(m_prev[h], m_ref.shape[1:])


def _mlstm(mqt, mk, mvt, mo, grow, keycol, norm, batch, seq):
    t = mk.shape[0]
    tile = MLSTM_TILES * TILE_T
    ns = seq // tile
    row = lambda width: pl.BlockSpec((tile, width), lambda b, j: (b * ns + j, 0))
    tile_t = lambda rows: pl.BlockSpec((MLSTM_TILES, rows, TILE_T), lambda b, j: (b * ns + j, 0, 0))
    return pl.pallas_call(
        functools.partial(_mlstm_body, chunk=CHUNK, chunks_per_step=tile // CHUNK),
        out_shape=jax.ShapeDtypeStruct((t, MLSTM_WIDTH), BF16),
        grid=(batch, ns),
        in_specs=[tile_t(MLSTM_WIDTH), row(MLSTM_WIDTH), tile_t(MLSTM_HEADS * V_ROWS),
                  row(MLSTM_WIDTH), pl.BlockSpec((GATE_ROWS, tile), lambda b, j: (0, b * ns + j)),
                  row(LANES), _const_spec(norm.shape)],
        out_specs=row(MLSTM_WIDTH),
        scratch_shapes=[
            pltpu.VMEM((MLSTM_HEADS, V_ROWS, MLSTM_HEAD_DIM), F32),
            pltpu.VMEM((MLSTM_HEADS, SUBLANES, LANES), F32),
        ],
        compiler_params=pltpu.CompilerParams(dimension_semantics=("arbitrary", "arbitrary"),
                                             vmem_limit_bytes=VMEM_LIMIT),
        name="mlstm",
    )(mqt, mk, mvt, mo, grow, keycol, norm)


def _out_ffn_body(x_ref, mla_ref, mlstm_ref, p_ref, w_out_ref, g_post_ref, g_fpre_ref, g_fpost_ref,
                  w_gate_ref, w_up_ref, w_down_ref, w_pg_ref, w_pp_ref, o_ref):
    tm = x_ref.shape[0]
    subs = [slice(r0, r0 + SUB_OUT) for r0 in range(0, tm, SUB_OUT)]
    mla_t = [mla_ref[s.start // TILE_T, :, s.start % TILE_T:s.start % TILE_T + SUB_OUT] for s in subs]
    mix = [_dot_tn(a, w_out_ref[0:MLA_WIDTH, :])
           + _dot(mlstm_ref[s, :], w_out_ref[MLA_WIDTH:MLA_WIDTH + MLSTM_WIDTH, :])
           for s, a in zip(subs, mla_t)]
    h1 = [x_ref[s, :] + _rms(m, g_post_ref[...]) for s, m in zip(subs, mix)]
    f = [_rms(h, g_fpre_ref[...]).astype(BF16) for h in h1]
    ffn = [None] * len(subs)
    off = 0
    for width in FF_CHUNKS:
        cols = slice(off, off + width)
        gate = [_dot(v, w_gate_ref[:, cols]) for v in f]
        up = [_dot(v, w_up_ref[:, cols]) for v in f]
        act = [(g * jax.nn.sigmoid(g) * u).astype(BF16) for g, u in zip(gate, up)]
        part = [_dot(a, w_down_ref[cols, :]) for a in act]
        ffn = [p if acc is None else acc + p for acc, p in zip(ffn, part)]
        off += width
    h2 = [h + _rms(v, g_fpost_ref[...]) for h, v in zip(h1, ffn)]
    pgate = [jax.nn.sigmoid(_dot(h.astype(BF16), w_pg_ref[...])) for h in h2]
    for s, h, g in zip(subs, h2, pgate):
        o_ref[s, :] = h + g * _dot(p_ref[s, :].astype(BF16), w_pp_ref[...])


def _out_ffn(x2, mla, mlstm, p2, w_out, g_post, g_fpre, g_fpost, w_gate, w_up, w_down, w_pg, w_pp):
    t = x2.shape[0]
    tm = TM_OUT
    row = lambda width: pl.BlockSpec((tm, width), lambda i: (i, 0))
    consts = (w_out, g_post, g_fpre, g_fpost, w_gate, w_up, w_down, w_pg, w_pp)
    return pl.pallas_call(
        _out_ffn_body,
        out_shape=jax.ShapeDtypeStruct((t, D_MODEL), F32),
        grid=(t // tm,),
        in_specs=[row(D_MODEL), pl.BlockSpec((tm // TILE_T, MLA_WIDTH, TILE_T), lambda i: (i, 0, 0)),
                  row(MLSTM_WIDTH), row(PLE_DIM)]
        + [_const_spec(c.shape) for c in consts],
        out_specs=row(D_MODEL),
        compiler_params=pltpu.CompilerParams(dimension_semantics=("arbitrary",),
                                             vmem_limit_bytes=VMEM_LIMIT),
        name="out_ffn",
    )(x2, mla, mlstm, p2, *consts)


def _pack_w_in(w_in):
    o_krope = Q_LORA_RANK + KV_LORA_RANK
    o_mq = o_krope + QK_ROPE_DIM
    o_mv = o_mq + 2 * MLSTM_WIDTH
    o_mo = o_mv + MLSTM_WIDTH
    o_gates = o_mo + MLSTM_WIDTH
    main = jnp.concatenate([w_in[:, :o_krope], w_in[:, o_mq:o_mv], w_in[:, o_mo:o_gates]], axis=1)
    gap = jnp.zeros((D_MODEL, SUBLANES - MLSTM_HEADS), w_in.dtype)
    tail = jnp.concatenate([w_in[:, o_krope:o_mq], w_in[:, o_gates:o_gates + MLSTM_HEADS], gap,
                            w_in[:, o_gates + MLSTM_HEADS:], gap], axis=1).T
    tail = jnp.pad(tail, ((0, LANES - tail.shape[0]), (0, 0)))
    mv_t = w_in[:, o_mv:o_mo].T.reshape(MLSTM_HEADS, MLSTM_HEAD_DIM, D_MODEL)
    mv_t = jnp.pad(mv_t, ((0, 0), (0, V_ROWS - MLSTM_HEAD_DIM), (0, 0)))
    return main.astype(BF16), tail.astype(BF16), mv_t.reshape(MLSTM_HEADS * V_ROWS, D_MODEL).astype(BF16)


def _pack_w_uq_t(w_uq):
    w = w_uq.T.reshape(MLA_HEADS, QK_HEAD_DIM, Q_LORA_RANK)
    w = jnp.pad(w, ((0, 0), (0, QK_PAD - QK_HEAD_DIM), (0, 0)))
    return w.reshape(MLA_HEADS * QK_PAD, Q_LORA_RANK).astype(BF16)


def _pack_w_ukv(w_ukv):
    w = w_ukv.reshape(KV_LORA_RANK, MLA_HEADS, QK_NOPE_DIM + V_HEAD_DIM)
    w_uk = w[:, :, :QK_NOPE_DIM].reshape(KV_LORA_RANK, MLA_HEADS * QK_NOPE_DIM)
    w_uv_t = jnp.transpose(w[:, :, QK_NOPE_DIM:], (1, 2, 0))
    w_uv_t = jnp.pad(w_uv_t, ((0, 0), (0, V_ROWS - V_HEAD_DIM), (0, 0)))
    return w_uk.astype(BF16), w_uv_t.reshape(MLA_HEADS * V_ROWS, KV_LORA_RANK).astype(BF16)


def _layer(h, p_i, pos_row, invf, attn_pre_norm, attn_post_norm, w_in, q_norm, kv_norm, w_uq, w_ukv,
           conv_w, conv_b, gate_bias_i, gate_bias_f, mlstm_norm, w_out, ffn_pre_norm, ffn_post_norm,
           w_gate, w_up, w_down, w_ple_proj, w_ple_gate, batch, seq):
    row = lambda a: a.reshape(1, -1)
    w_in_p, w_tail, w_mv_t = _pack_w_in(w_in)
    w_uk, w_uv_t = _pack_w_ukv(w_ukv)
    gap = jnp.zeros((SUBLANES - MLSTM_HEADS,), F32)
    gbias = jnp.concatenate([gate_bias_i, gap, gate_bias_f, gap]).astype(F32).reshape(2 * SUBLANES, 1)
    vones = jnp.zeros((MLA_HEADS, V_ROWS, 1), F32).at[:, V_HEAD_DIM, 0].set(1.0)
    qt, k, vt, mqt, mk, mvt, mo, grow, keycol = _in_proj(
        h, pos_row, invf, row(attn_pre_norm), w_in_p, w_tail, row(q_norm), row(kv_norm),
        _pack_w_uq_t(w_uq), w_uk, w_uv_t, w_mv_t, vones.reshape(MLA_HEADS * V_ROWS, 1), conv_w,
        row(conv_b), gbias, seq)
    mla = _attention(qt, k, vt, batch, seq)
    mlstm = _mlstm(mqt, mk, mvt, mo, grow, keycol, row(mlstm_norm), batch, seq)
    return _out_ffn(h, mla, mlstm, p_i, w_out.astype(BF16), row(attn_post_norm), row(ffn_pre_norm),
                    row(ffn_post_norm), w_gate.astype(BF16), w_up.astype(BF16),
                    w_down.astype(BF16), w_ple_gate.astype(BF16), w_ple_proj.astype(BF16))


@jax.jit
def kernel(x, p, positions, attn_pre_norm, attn_post_norm, w_in, q_norm, kv_norm, w_uq, w_ukv,
           conv_w, conv_b, gate_bias_i, gate_bias_f, mlstm_norm, w_out, ffn_pre_norm, ffn_post_norm,
           w_gate, w_up, w_down, w_ple_proj, w_ple_gate):
    batch, seq, _ = x.shape
    t = batch * seq
    depth = p.shape[0]
    assert x.shape[2] == D_MODEL and w_gate.shape[1:] == (D_MODEL, D_FF) and sum(FF_CHUNKS) == D_FF
    assert seq % (MLSTM_TILES * TILE_T) == 0 and TM_IN % TILE_T == 0 and TM_OUT % TILE_T == 0
    inv_freq = ROPE_THETA ** (-jnp.arange(0, QK_ROPE_DIM, 2, dtype=F32) / QK_ROPE_DIM)
    invf = inv_freq.reshape(ROPE_HALF, 1)
    pos_row = positions.astype(F32).reshape(1, t)
    h = x.reshape(t, D_MODEL)
    for i in range(depth):
        h = _layer(h, p[i].reshape(t, PLE_DIM), pos_row, invf, attn_pre_norm[i], attn_post_norm[i],
                   w_in[i], q_norm[i], kv_norm[i], w_uq[i], w_ukv[i], conv_w[i], conv_b[i],
                   gate_bias_i[i], gate_bias_f[i], mlstm_norm[i], w_out[i], ffn_pre_norm[i],
                   ffn_post_norm[i], w_gate[i], w_up[i], w_down[i], w_ple_proj[i], w_ple_gate[i],
                   batch, seq)
    return h.reshape(batch, seq, D_MODEL)
```
